```python
import math
import jax, jax.numpy as jnp
from jax import lax
import numpy as np

D_MODEL = 1024
BATCH = 2
SEQ = 8192
DEPTH = 1

PLE_DIM = 256
SGU_CHUNK = 128
SGU_GROUPS = 8
SGU_WIDTH = D_MODEL
SGU_GROUP_DIM = SGU_WIDTH // SGU_GROUPS
ATT_HEADS = 16
HEAD_DIM = 64
ATT_WIDTH = ATT_HEADS * HEAD_DIM
MOBA_BLOCK = 256
MOBA_TOPK = 3
Q_CHUNK = 64
REL_BUCKETS = 32
REL_MAX_DIST = 1024
D_FF = 4 * D_MODEL
EPS = 1e-6
SPLITS = (SGU_WIDTH, SGU_WIDTH, ATT_WIDTH, ATT_WIDTH, ATT_WIDTH, D_MODEL, D_MODEL)
IN_COLS = sum(SPLITS)

kernel_name = "hybrid_gmlp_moba_gated_block"


def rms_norm(x, g):
    xf = x.astype(jnp.float32)
    y = xf * lax.rsqrt(jnp.mean(xf * xf, axis=-1, keepdims=True) + EPS)
    return (y * g.astype(jnp.float32)).astype(x.dtype)


def layer_norm(x, g, b):
    xf = x.astype(jnp.float32)
    mu = jnp.mean(xf, axis=-1, keepdims=True)
    xc = xf - mu
    y = xc * lax.rsqrt(jnp.mean(xc * xc, axis=-1, keepdims=True) + EPS)
    return (y * g.astype(jnp.float32) + b.astype(jnp.float32)).astype(x.dtype)


def rel_bucket(dist):
    n = jnp.maximum(dist, 0)
    max_exact = REL_BUCKETS // 2
    nf = jnp.maximum(n, max_exact).astype(jnp.float32)
    large = max_exact + (jnp.log(nf / max_exact) / math.log(REL_MAX_DIST / max_exact)
                         * (REL_BUCKETS - max_exact)).astype(jnp.int32)
    large = jnp.minimum(large, REL_BUCKETS - 1)
    return jnp.where(n < max_exact, n, large)


def sgu_mixer(u, v, w_s, b_s, ln_g, ln_b):
    B, S, _ = u.shape
    nc = S // SGU_CHUNK
    v = layer_norm(v, ln_g, ln_b)
    vg = v.reshape(B, nc, SGU_CHUNK, SGU_GROUPS, SGU_GROUP_DIM)
    causal = jnp.tril(jnp.ones((SGU_CHUNK, SGU_CHUNK), dtype=bool))
    w = jnp.where(causal, w_s, 0).astype(v.dtype)
    mixed = jnp.einsum('gts,bnsgc->bntgc', w, vg) + b_s.T[None, None, :, :, None].astype(v.dtype)
    return u * mixed.reshape(B, S, SGU_WIDTH)


_gather_blocks = jax.vmap(jax.vmap(lambda blocks, idx: blocks[idx]))


def moba_attention(q, k, v, rel_bias):
    B, H, S, dh = q.shape
    nb = S // MOBA_BLOCK
    topk = min(MOBA_TOPK, nb)
    kb = k.reshape(B, H, nb, MOBA_BLOCK, dh)
    vb = v.reshape(B, H, nb, MOBA_BLOCK, dh)
    k_mean = jnp.mean(kb.astype(jnp.float32), axis=3).astype(k.dtype)
    scale = dh ** -0.5
    table = rel_bias.T
    h_idx = jnp.arange(H)[None, :, None, None, None]
    pos_in_block = jnp.arange(MOBA_BLOCK)
    n_chunks = S // Q_CHUNK
    q_chunks = q.reshape(B, H, n_chunks, Q_CHUNK, dh).transpose(2, 0, 1, 3, 4)
    n_sel = topk * MOBA_BLOCK

    def one_chunk(args):
        c, q_c = args
        q_pos = c * Q_CHUNK + jnp.arange(Q_CHUNK)
        own = (c * Q_CHUNK) // MOBA_BLOCK
        gate = jnp.einsum('bhqd,bhnd->bhqn', q_c, k_mean).astype(jnp.float32)
        gate = jnp.where(jnp.arange(nb) < own, gate, -jnp.inf)
        _, sel = lax.top_k(gate, topk)
        sel_valid = sel < own
        k_sel = _gather_blocks(kb, sel)
        v_sel = _gather_blocks(vb, sel)
        s_sel = jnp.einsum('bhqd,bhqkmd->bhqkm', q_c, k_sel).astype(jnp.float32) * scale
        k_pos_sel = sel[..., None] * MOBA_BLOCK + pos_in_block
        dist_sel = q_pos[:, None, None] - k_pos_sel
        s_sel = s_sel + table[h_idx, rel_bucket(dist_sel)].astype(jnp.float32)
        s_sel = jnp.where(sel_valid[..., None], s_sel, -jnp.inf)
        k_own = lax.dynamic_index_in_dim(kb, own, axis=2, keepdims=False)
        v_own = lax.dynamic_index_in_dim(vb, own, axis=2, keepdims=False)
        s_own = jnp.einsum('bhqd,bhmd->bhqm', q_c, k_own).astype(jnp.float32) * scale
        dist_own = q_pos[:, None] - (own * MOBA_BLOCK + pos_in_block)[None, :]
        s_own = s_own + table[:, rel_bucket(dist_own)].astype(jnp.float32)
        s_own = jnp.where(dist_own >= 0, s_own, -jnp.inf)
        s_all = jnp.concatenate([s_sel.reshape(B, H, Q_CHUNK, n_sel), s_own], axis=-1)
        probs = jax.nn.softmax(s_all, axis=-1).astype(v.dtype)
        p_sel = probs[..., :n_sel].reshape(B, H, Q_CHUNK, topk, MOBA_BLOCK)
        p_own = probs[..., n_sel:]
        return (jnp.einsum('bhqkm,bhqkmd->bhqd', p_sel, v_sel)
                + jnp.einsum('bhqm,bhmd->bhqd', p_own, v_own))

    out = lax.map(one_chunk, (jnp.arange(n_chunks), q_chunks))
    return out.transpose(1, 2, 0, 3, 4).reshape(B, H, S, dh)


def setup_inputs(seed: int = 0) -> dict:
    key = jax.random.key(seed)
    ks = jax.random.split(key, 20)
    f32 = jnp.float32
    nrm = lambda k, shape, s: jax.random.normal(k, shape, f32) * s
    L = DEPTH
    return {
        "x": nrm(ks[0], (BATCH, SEQ, D_MODEL), 1.0),
        "p": nrm(ks[1], (DEPTH, BATCH, SEQ, PLE_DIM), 1.0),
        "norm_mix_g": 1.0 + nrm(ks[2], (L, D_MODEL), 0.05),
        "w_in": nrm(ks[3], (L, D_MODEL, IN_COLS), D_MODEL ** -0.5),
        "w_sgu_spatial": nrm(ks[4], (L, SGU_GROUPS, SGU_CHUNK, SGU_CHUNK), SGU_CHUNK ** -0.5),
        "b_sgu_spatial": 1.0 + nrm(ks[5], (L, SGU_GROUPS, SGU_CHUNK), 0.1),
        "ln_v_g": 1.0 + nrm(ks[6], (L, SGU_WIDTH), 0.05),
        "ln_v_b": nrm(ks[7], (L, SGU_WIDTH), 0.02),
        "rel_bias": nrm(ks[8], (REL_BUCKETS, ATT_HEADS), 0.5),
        "w_out": nrm(ks[9], (L, D_MODEL, D_MODEL), D_MODEL ** -0.5),
        "norm_ffn_g": 1.0 + nrm(ks[10], (L, D_MODEL), 0.05),
        "w_ff1": nrm(ks[11], (L, D_MODEL, D_FF), D_MODEL ** -0.5),
        "w_ff2": nrm(ks[12], (L, D_FF, D_MODEL), D_FF ** -0.5),
        "norm_ple_g": 1.0 + nrm(ks[13], (L, D_MODEL), 0.05),
        "w_ple_gate": nrm(ks[14], (L, D_MODEL, D_MODEL), D_MODEL ** -0.5),
        "w_ple_proj": nrm(ks[15], (L, PLE_DIM, D_MODEL), PLE_DIM ** -0.5),
        "norm_final_g": 1.0 + nrm(ks[16], (D_MODEL,), 0.05),
    }


def reference(x, p, norm_mix_g, w_in, w_sgu_spatial, b_sgu_spatial, ln_v_g, ln_v_b, rel_bias,
              w_out, norm_ffn_g, w_ff1, w_ff2, norm_ple_g, w_ple_gate, w_ple_proj, norm_final_g):
    B, S, _ = x.shape
    offsets = np.cumsum(SPLITS)[:-1].tolist()
    for i in range(DEPTH):
        h = rms_norm(x, norm_mix_g[i])
        proj = h @ w_in[i]
        za_u, za_v, q, k, v_att, g_a, g_b = jnp.split(proj, offsets, axis=-1)
        y_a = sgu_mixer(jax.nn.gelu(za_u), jax.nn.gelu(za_v), w_sgu_spatial[i], b_sgu_spatial[i],
                        ln_v_g[i], ln_v_b[i])
        heads = lambda t: t.reshape(B, S, ATT_HEADS, HEAD_DIM).transpose(0, 2, 1, 3)
        y_b = moba_attention(heads(q), heads(k), heads(v_att), rel_bias)
        y_b = y_b.transpose(0, 2, 1, 3).reshape(B, S, ATT_WIDTH)
        merged = jax.nn.sigmoid(g_a) * y_a + jax.nn.sigmoid(g_b) * y_b
        x = x + merged @ w_out[i]
        h = rms_norm(x, norm_ffn_g[i])
        x = x + jnp.square(jax.nn.relu(h @ w_ff1[i])) @ w_ff2[i]
        gate = jax.nn.sigmoid(rms_norm(x, norm_ple_g[i]) @ w_ple_gate[i])
        x = x + gate * (p[i] @ w_ple_proj[i])
    return rms_norm(x, norm_final_g)
```

```python
import functools
import math

import jax
import jax.numpy as jnp
from jax import lax
from jax.experimental import pallas as pl
from jax.experimental.pallas import tpu as pltpu

D_MODEL = 1024
PLE_DIM = 256
SGU_CHUNK = 128
SGU_GROUPS = 8
ATT_HEADS = 16
HEAD_DIM = 64
MOBA_BLOCK = 256
MOBA_TOPK = 3
REL_BUCKETS = 32
REL_MAX_DIST = 1024
D_FF = 4 * D_MODEL
EPS = 1e-6

LANES = 128
HEAD_PAIRS = ATT_HEADS // 2
PAIR_W = 2 * HEAD_DIM
NEG = -1e30
VMEM_LIMIT = 60 * 1024 * 1024

F32 = jnp.float32
BF16 = jnp.bfloat16


def _bucket_thresholds():
    max_exact = REL_BUCKETS // 2
    n_log = REL_BUCKETS - max_exact
    ratio = REL_MAX_DIST // max_exact
    out = []
    for k in range(1, n_log):
        d = max_exact
        target = (max_exact ** n_log) * (ratio ** k)
        while d ** n_log < target:
            d += 1
        out.append(d)
    return tuple(out)


BUCKET_THRESHOLDS = _bucket_thresholds()
NEAR_TILES = -(-(BUCKET_THRESHOLDS[-1] + MOBA_BLOCK - 1) // MOBA_BLOCK)
assert NEAR_TILES == 5


def _dot(a, b):
    return jnp.dot(a, b, preferred_element_type=F32)


def _dot_nt(a, b):
    return lax.dot_general(a, b, (((1,), (1,)), ((), ())), preferred_element_type=F32)


def _rms(x, g):
    return x * lax.rsqrt(jnp.mean(x * x, axis=-1, keepdims=True) + EPS) * g


def _in_proj_kernel(x_ref, g_ref, wa_ref, wvt_ref, wg_ref, lng_ref, lnb_ref,
                    u_ref, vn_ref, q_ref, k_ref, vt_ref, sga_ref, sgb_ref, *, tm):
    h = _rms(x_ref[...], g_ref[...]).astype(BF16)

    u_ref[...] = jax.nn.gelu(_dot(h, wa_ref[:, 0:D_MODEL])).astype(BF16)

    gv = jax.nn.gelu(_dot(h, wa_ref[:, D_MODEL:2 * D_MODEL]))
    mu = jnp.mean(gv, axis=-1, keepdims=True)
    gc = gv - mu
    vn = gc * lax.rsqrt(jnp.mean(gc * gc, axis=-1, keepdims=True) + EPS)
    vn_ref[...] = (vn * lng_ref[...] + lnb_ref[...]).astype(BF16)

    q = _dot(h, wa_ref[:, 2 * D_MODEL:3 * D_MODEL]) * (HEAD_DIM ** -0.5)
    k = _dot(h, wa_ref[:, 3 * D_MODEL:4 * D_MODEL])
    for p in range(HEAD_PAIRS):
        q_ref[0, p] = q[:, p * PAIR_W:(p + 1) * PAIR_W].astype(BF16)
        k_ref[0, p] = k[:, p * PAIR_W:(p + 1) * PAIR_W].astype(BF16)

    vt = _dot_nt(wvt_ref[...], h).astype(BF16)
    for p in range(HEAD_PAIRS):
        for jb in range(tm // MOBA_BLOCK):
            vt_ref[0, p, jb] = vt[p * PAIR_W:(p + 1) * PAIR_W, jb * MOBA_BLOCK:(jb + 1) * MOBA_BLOCK]

    sga_ref[...] = jax.nn.sigmoid(_dot(h, wg_ref[:, 0:D_MODEL])).astype(BF16)
    sgb_ref[...] = jax.nn.sigmoid(_dot(h, wg_ref[:, D_MODEL:2 * D_MODEL])).astype(BF16)


def _in_proj(x2, g, wa, wvt, wg, lng, lnb, *, batch, seq, tm):
    n = x2.shape[0]
    tiles_per_seq = seq // tm
    nb = seq // MOBA_BLOCK
    row = lambda r: (r, 0)
    const = lambda r: (0, 0)
    resident = functools.partial(pl.BlockSpec, pipeline_mode=pl.Buffered(1))
    tok = jax.ShapeDtypeStruct((n, D_MODEL), BF16)
    pair_major = jax.ShapeDtypeStruct((batch, HEAD_PAIRS, seq, PAIR_W), BF16)
    return pl.pallas_call(
        functools.partial(_in_proj_kernel, tm=tm),
        grid=(n // tm,),
        in_specs=[
            pl.BlockSpec((tm, D_MODEL), row),
            resident((1, D_MODEL), const),
            resident((D_MODEL, 4 * D_MODEL), const),
            resident((D_MODEL, D_MODEL), const),
            resident((D_MODEL, 2 * D_MODEL), const),
            resident((1, D_MODEL), const),
            resident((1, D_MODEL), const),
        ],
        out_specs=[
            pl.BlockSpec((tm, D_MODEL), row),
            pl.BlockSpec((tm, D_MODEL), row),
            pl.BlockSpec((1, HEAD_PAIRS, tm, PAIR_W), lambda r: (r // tiles_per_seq, 0, r % tiles_per_seq, 0)),
            pl.BlockSpec((1, HEAD_PAIRS, tm, PAIR_W), lambda r: (r // tiles_per_seq, 0, r % tiles_per_seq, 0)),
            pl.BlockSpec((1, HEAD_PAIRS, tm // MOBA_BLOCK, PAIR_W, MOBA_BLOCK),
                         lambda r: (r // tiles_per_seq, 0, r % tiles_per_seq, 0, 0)),
            pl.BlockSpec((tm, D_MODEL), row),
            pl.BlockSpec((tm, D_MODEL), row),
        ],
        out_shape=[tok, tok, pair_major, pair_major,
                   jax.ShapeDtypeStruct((batch, HEAD_PAIRS, nb, PAIR_W, MOBA_BLOCK), BF16),
                   tok, tok],
        compiler_params=pltpu.CompilerParams(dimension_semantics=("arbitrary",),
                                             vmem_limit_bytes=VMEM_LIMIT),
        name="in_proj",
    )(x2, g, wa, wvt, wg, lng, lnb)


def _bias_tiles_kernel(tab_ref, o_ref):
    h = pl.program_id(0)
    key = lax.broadcasted_iota(jnp.int32, (MOBA_BLOCK, MOBA_BLOCK), 0)
    qry = lax.broadcasted_iota(jnp.int32, (MOBA_BLOCK, MOBA_BLOCK), 1)
    max_exact = REL_BUCKETS // 2
    last = tab_ref[REL_BUCKETS - 1, h]
    for t in range(NEAR_TILES):
        dist = t * MOBA_BLOCK + qry - key
        n = jnp.maximum(dist, 0)
        val = jnp.full((MOBA_BLOCK, MOBA_BLOCK), tab_ref[0, h], F32)
        for b in range(1, max_exact + 1):
            val = jnp.where(n >= b, tab_ref[b, h], val)
        for kk, thr in enumerate(BUCKET_THRESHOLDS):
            val = jnp.where(n >= thr, tab_ref[max_exact + 1 + kk, h], val)
        val = val - last
        if t == 0:
            val = jnp.where(dist >= 0, val, NEG)
        o_ref[0, t] = val


def _bias_tiles(rel_bias):
    return pl.pallas_call(
        _bias_tiles_kernel,
        grid=(ATT_HEADS,),
        in_specs=[pl.BlockSpec(memory_space=pltpu.SMEM)],
        out_specs=pl.BlockSpec((1, NEAR_TILES, MOBA_BLOCK, MOBA_BLOCK), lambda h: (h, 0, 0, 0)),
        out_shape=jax.ShapeDtypeStruct((ATT_HEADS, NEAR_TILES, MOBA_BLOCK, MOBA_BLOCK), F32),
        compiler_params=pltpu.CompilerParams(dimension_semantics=("arbitrary",)),
        name="bias_tiles",
    )(rel_bias)


def _attn_kernel(q_ref, k_ref, vt_ref, bias_ref, o_ref,
                 km_ref, qa_ref, mask_ref, m_ref, l_ref, acc_ref, *, nb):
    i = pl.program_id(2)

    @pl.when(i == 0)
    def _():
        def body(j, c):
            kb = k_ref[0, 0, pl.ds(pl.multiple_of(j * MOBA_BLOCK, MOBA_BLOCK), MOBA_BLOCK), :]
            km_ref[pl.ds(j, 1), :] = jnp.sum(kb.astype(F32), axis=0, keepdims=True) * (1.0 / MOBA_BLOCK)
            return c
        lax.fori_loop(0, nb, body, 0)

    km = km_ref[...]
    km_hi = km.astype(BF16)
    km_lo = (km - km_hi.astype(F32)).astype(BF16)
    q = q_ref[0, 0]
    lane = lax.broadcasted_iota(jnp.int32, (1, PAIR_W), 1)
    blk = lax.broadcasted_iota(jnp.int32, (nb, MOBA_BLOCK), 0)
    for a in range(2):
        qa = jnp.where((lane >= a * HEAD_DIM) & (lane < (a + 1) * HEAD_DIM), q, jnp.zeros_like(q))
        qa_ref[a] = qa
        gate = _dot_nt(km_hi, qa) + _dot_nt(km_lo, qa)
        gate = jnp.where(blk < i, gate, -jnp.inf)
        sel = jnp.zeros((nb, MOBA_BLOCK), jnp.bool_)
        for _ in range(MOBA_TOPK):
            top = jnp.max(gate, axis=0, keepdims=True)
            first = jnp.min(jnp.where(gate == top, blk, nb), axis=0, keepdims=True)
            pick = (blk == first) & (top > -jnp.inf)
            sel = sel | pick
            gate = jnp.where(pick, -jnp.inf, gate)
        mask_ref[a] = jnp.where(sel, 0.0, NEG)

    def scores(j, a):
        kj = k_ref[0, 0, pl.ds(pl.multiple_of(j * MOBA_BLOCK, MOBA_BLOCK), MOBA_BLOCK), :]
        return _dot_nt(kj, qa_ref[a])

    def v_rows(j, a):
        return vt_ref[0, 0, j, a * HEAD_DIM:(a + 1) * HEAD_DIM, :]

    for a in range(2):
        s = scores(i, a) + bias_ref[a, 0]
        m = jnp.max(s, axis=0, keepdims=True)
        p = jnp.exp(s - m)
        m_ref[a] = m
        l_ref[a] = jnp.sum(p, axis=0, keepdims=True)
        acc_ref[a] = _dot(v_rows(i, a), p.astype(BF16))

    def update(j, a, s):
        m_old = m_ref[a]
        m_new = jnp.maximum(m_old, jnp.max(s, axis=0, keepdims=True))
        alpha = jnp.exp(m_old - m_new)
        p = jnp.exp(s - m_new)
        m_ref[a] = m_new
        l_ref[a] = alpha * l_ref[a] + jnp.sum(p, axis=0, keepdims=True)
        acc_ref[a] = alpha * acc_ref[a] + _dot(v_rows(j, a), p.astype(BF16))

    n_far = jnp.maximum(i - (NEAR_TILES - 1), 0)

    def far_body(j, c):
        for a in range(2):
            update(j, a, scores(j, a) + mask_ref[a, pl.ds(j, 1), :])
        return c
    lax.fori_loop(0, n_far, far_body, 0)

    def near_body(j, c):
        for a in range(2):
            update(j, a, scores(j, a) + mask_ref[a, pl.ds(j, 1), :] + bias_ref[a, i - j])
        return c
    lax.fori_loop(n_far, i, near_body, 0)

    out_t = jnp.concatenate([acc_ref[a] / l_ref[a] for a in range(2)], axis=0)
    o_ref[0, 0] = out_t.T.astype(BF16)


def _attention(q, k, vt, bias_t):
    batch, _, seq, _ = q.shape
    nb = seq // MOBA_BLOCK
    return pl.pallas_call(
        functools.partial(_attn_kernel, nb=nb),
        grid=(batch, HEAD_PAIRS, nb),
        in_specs=[
            pl.BlockSpec((1, 1, MOBA_BLOCK, PAIR_W), lambda b, p, i: (b, p, i, 0)),
            pl.BlockSpec((1, 1, seq, PAIR_W), lambda b, p, i: (b, p, 0, 0)),
            pl.BlockSpec((1, 1, nb, PAIR_W, MOBA_BLOCK), lambda b, p, i: (b, p, 0, 0, 0)),
            pl.BlockSpec((2, NEAR_TILES, MOBA_BLOCK, MOBA_BLOCK), lambda b, p, i: (p, 0, 0, 0)),
        ],
        out_specs=pl.BlockSpec((1, 1, MOBA_BLOCK, PAIR_W), lambda b, p, i: (b, p, i, 0)),
        out_shape=jax.ShapeDtypeStruct((batch, HEAD_PAIRS, seq, PAIR_W), BF16),
        scratch_shapes=[
            pltpu.VMEM((nb, PAIR_W), F32),
            pltpu.VMEM((2, MOBA_BLOCK, PAIR_W), BF16),
            pltpu.VMEM((2, nb, MOBA_BLOCK), F32),
            pltpu.VMEM((2, 1, MOBA_BLOCK), F32),
            pltpu.VMEM((2, 1, MOBA_BLOCK), F32),
            pltpu.VMEM((2, HEAD_DIM, MOBA_BLOCK), F32),
        ],
        compiler_params=pltpu.CompilerParams(dimension_semantics=("arbitrary", "arbitrary", "arbitrary"),
                                             vmem_limit_bytes=VMEM_LIMIT),
        name="moba_attention",
    )(q, k, vt, bias_t)


def _post_kernel(x_ref, u_ref, vn_ref, sga_ref, sgb_ref, yb_ref, p_ref,
                 ws_ref, bs_ref, wo_ref, gffn_ref, w1_ref, w2_ref, gple_ref, wpg_ref, wpp_ref, gfin_ref,
                 o_ref, merged_ref, *, tm, final_norm):
    row = lax.broadcasted_iota(jnp.int32, (SGU_CHUNK, SGU_CHUNK), 0)
    col = lax.broadcasted_iota(jnp.int32, (SGU_CHUNK, SGU_CHUNK), 1)
    for g in range(SGU_GROUPS):
        w = jnp.where(row >= col, ws_ref[g], 0.0).astype(BF16)
        b = bs_ref[g]
        cols = slice(g * LANES, (g + 1) * LANES)
        for t in range(tm // SGU_CHUNK):
            rows = slice(t * SGU_CHUNK, (t + 1) * SGU_CHUNK)
            mixed = _dot(w, vn_ref[rows, cols]) + b
            y_a = u_ref[rows, cols].astype(F32) * mixed
            y_b = yb_ref[0, g, rows, :].astype(F32)
            merged = sga_ref[rows, cols].astype(F32) * y_a + sgb_ref[rows, cols].astype(F32) * y_b
            merged_ref[rows, cols] = merged.astype(BF16)

    x1 = x_ref[...] + _dot(merged_ref[...], wo_ref[...])

    h = _rms(x1, gffn_ref[...]).astype(BF16)
    x2 = x1
    for c in range(D_FF // D_MODEL):
        cs = slice(c * D_MODEL, (c + 1) * D_MODEL)
        a = jnp.square(jnp.maximum(_dot(h, w1_ref[:, cs]), 0.0)).astype(BF16)
        x2 = x2 + _dot(a, w2_ref[cs, :])

    gate = jax.nn.sigmoid(_dot(_rms(x2, gple_ref[...]).astype(BF16), wpg_ref[...]))
    x3 = x2 + gate * _dot(p_ref[...].astype(BF16), wpp_ref[...])
    o_ref[...] = _rms(x3, gfin_ref[...]) if final_norm else x3


def _post(x2, u, vn, sga, sgb, yb, p2, ws, bs, wo, gffn, w1, w2, gple, wpg, wpp, gfin, *, seq, tm, final_norm):
    n = x2.shape[0]
    tiles_per_seq = seq // tm
    row = lambda r: (r, 0)
    resident = functools.partial(pl.BlockSpec, pipeline_mode=pl.Buffered(1))
    c2 = lambda r: (0, 0)
    c3 = lambda r: (0, 0, 0)
    tokb = pl.BlockSpec((tm, D_MODEL), row)
    return pl.pallas_call(
        functools.partial(_post_kernel, tm=tm, final_norm=final_norm),
        grid=(n // tm,),
        in_specs=[
            tokb, tokb, tokb, tokb, tokb,
            pl.BlockSpec((1, HEAD_PAIRS, tm, PAIR_W), lambda r: (r // tiles_per_seq, 0, r % tiles_per_seq, 0)),
            pl.BlockSpec((tm, PLE_DIM), row),
            resident((SGU_GROUPS, SGU_CHUNK, SGU_CHUNK), c3),
            resident((SGU_GROUPS, SGU_CHUNK, 1), c3),
            resident((D_MODEL, D_MODEL), c2),
            resident((1, D_MODEL), c2),
            resident((D_MODEL, D_FF), c2),
            resident((D_FF, D_MODEL), c2),
            resident((1, D_MODEL), c2),
            resident((D_MODEL, D_MODEL), c2),
            resident((PLE_DIM, D_MODEL), c2),
            resident((1, D_MODEL), c2),
        ],
        out_specs=pl.BlockSpec((tm, D_MODEL), row),
        out_shape=jax.ShapeDtypeStruct((n, D_MODEL), F32),
        scratch_shapes=[pltpu.VMEM((tm, D_MODEL), BF16)],
        compiler_params=pltpu.CompilerParams(dimension_semantics=("arbitrary",),
                                             vmem_limit_bytes=VMEM_LIMIT),
        name="post",
    )(x2, u, vn, sga, sgb, yb, p2, ws, bs, wo, gffn, w1, w2, gple, wpg, wpp, gfin)


def kernel(x, p, norm_mix_g, w_in, w_sgu_spatial, b_sgu_spatial, ln_v_g, ln_v_b, rel_bias, w_out, norm_ffn_g,
           w_ff1, w_ff2, norm_ple_g, w_ple_gate, w_ple_proj, norm_final_g):
    batch, seq, _ = x.shape
    depth = w_in.shape[0]
    n = batch * seq
    assert seq % MOBA_BLOCK == 0 and seq // MOBA_BLOCK >= MOBA_TOPK
    tm_in = 512
    tm_post = 256
    vec = lambda g: g.reshape(1, D_MODEL)

    bias_t = _bias_tiles(rel_bias)
    x2 = x.reshape(n, D_MODEL)
    for i in range(depth):
        wi = w_in[i].astype(BF16)
        wa = wi[:, :4 * D_MODEL]
        wvt = wi[:, 4 * D_MODEL:5 * D_MODEL].T
        wg = wi[:, 5 * D_MODEL:]
        u, vn, q, k, vt, sga, sgb = _in_proj(
            x2, vec(norm_mix_g[i]), wa, wvt, wg, vec(ln_v_g[i]), vec(ln_v_b[i]),
            batch=batch, seq=seq, tm=tm_in)
        yb = _attention(q, k, vt, bias_t)
        x2 = _post(
            x2, u, vn, sga, sgb, yb, p[i].reshape(n, PLE_DIM),
            w_sgu_spatial[i], b_sgu_spatial[i].reshape(SGU_GROUPS, SGU_CHUNK, 1),
            w_out[i].astype(BF16), vec(norm_ffn_g[i]), w_ff1[i].astype(BF16), w_ff2[i].astype(BF16),
            vec(norm_ple_g[i]), w_ple_gate[i].astype(BF16), w_ple_proj[i].astype(BF16),
            vec(norm_final_g), seq=seq, tm=tm_post, final_norm=(i == depth - 1))
    return x2.reshape(batch, seq, D_MODEL)
```

```python
import functools
import math

import jax
import jax.numpy as jnp
from jax import lax
from jax.experimental import pallas as pl
from jax.experimental.pallas import tpu as pltpu

D_MODEL = 1024
PLE_DIM = 256
SGU_CHUNK = 128
SGU_GROUPS = 8
ATT_HEADS = 16
HEAD_DIM = 64
MOBA_BLOCK = 256
MOBA_TOPK = 3
REL_BUCKETS = 32
REL_MAX_DIST = 1024
D_FF = 4 * D_MODEL
EPS = 1e-6

LANES = 128
BF16_ROWS = 16
HEAD_PAIRS = ATT_HEADS // 2
PAIR_W = 2 * HEAD_DIM
V_ROWS = HEAD_DIM + BF16_ROWS
BIG = 1e30
VMEM_LIMIT = 60 * 1024 * 1024
LOG2E = math.log2(math.e)

F32 = jnp.float32
BF16 = jnp.bfloat16


def _bucket_thresholds():
    max_exact = REL_BUCKETS // 2
    n_log = REL_BUCKETS - max_exact
    ratio = REL_MAX_DIST // max_exact
    out = []
    for k in range(1, n_log):
        d = max_exact
        target = (max_exact ** n_log) * (ratio ** k)
        while d ** n_log < target:
            d += 1
        out.append(d)
    return tuple(out)


BUCKET_THRESHOLDS = _bucket_thresholds()
NEAR_TILES = -(-(BUCKET_THRESHOLDS[-1] + MOBA_BLOCK - 1) // MOBA_BLOCK)
assert NEAR_TILES == 5


def _dot(a, b):
    return jnp.dot(a, b, preferred_element_type=F32)


def _dot_nt(a, b):
    return lax.dot_general(a, b, (((1,), (1,)), ((), ())), preferred_element_type=F32)


def _rms(x, g):
    return x * lax.rsqrt(jnp.mean(x * x, axis=-1, keepdims=True) + EPS) * g


def _in_proj_kernel(x_ref, g_ref, wa_ref, wvt_ref, wg_ref, lng_ref, lnb_ref,
                    u_ref, vn_ref, q_ref, k_ref, vt_ref, sga_ref, sgb_ref, *, tm):
    h = _rms(x_ref[...], g_ref[...]).astype(BF16)

    u_ref[...] = jax.nn.gelu(_dot(h, wa_ref[:, 0:D_MODEL])).astype(BF16)

    gv = jax.nn.gelu(_dot(h, wa_ref[:, D_MODEL:2 * D_MODEL]))
    mu = jnp.mean(gv, axis=-1, keepdims=True)
    gc = gv - mu
    vn = gc * lax.rsqrt(jnp.mean(gc * gc, axis=-1, keepdims=True) + EPS)
    vn_ref[...] = (vn * lng_ref[...] + lnb_ref[...]).astype(BF16)

    q = _dot(h, wa_ref[:, 2 * D_MODEL:3 * D_MODEL]) * (HEAD_DIM ** -0.5 * LOG2E)
    k = _dot(h, wa_ref[:, 3 * D_MODEL:4 * D_MODEL])
    for p in range(HEAD_PAIRS):
        q_ref[0, p] = q[:, p * PAIR_W:(p + 1) * PAIR_W].astype(BF16)
        k_ref[0, p] = k[:, p * PAIR_W:(p + 1) * PAIR_W].astype(BF16)

    vt = _dot_nt(wvt_ref[...], h).astype(BF16)
    ones = jnp.ones((BF16_ROWS, MOBA_BLOCK), BF16)
    for p in range(HEAD_PAIRS):
        for jb in range(tm // MOBA_BLOCK):
            for a in range(2):
                r0 = p * PAIR_W + a * HEAD_DIM
                vt_ref[0, p, jb, a, 0:HEAD_DIM, :] = vt[r0:r0 + HEAD_DIM, jb * MOBA_BLOCK:(jb + 1) * MOBA_BLOCK]
                vt_ref[0, p, jb, a, HEAD_DIM:V_ROWS, :] = ones

    sga_ref[...] = jax.nn.sigmoid(_dot(h, wg_ref[:, 0:D_MODEL])).astype(BF16)
    sgb_ref[...] = jax.nn.sigmoid(_dot(h, wg_ref[:, D_MODEL:2 * D_MODEL])).astype(BF16)


def _in_proj(x2, g, wa, wvt, wg, lng, lnb, *, batch, seq, tm):
    n = x2.shape[0]
    tiles_per_seq = seq // tm
    nb = seq // MOBA_BLOCK
    row = lambda r: (r, 0)
    const = lambda r: (0, 0)
    resident = functools.partial(pl.BlockSpec, pipeline_mode=pl.Buffered(1))
    tok = jax.ShapeDtypeStruct((n, D_MODEL), BF16)
    pair_major = jax.ShapeDtypeStruct((batch, HEAD_PAIRS, seq, PAIR_W), BF16)
    return pl.pallas_call(
        functools.partial(_in_proj_kernel, tm=tm),
        grid=(n // tm,),
        in_specs=[
            pl.BlockSpec((tm, D_MODEL), row),
            resident((1, D_MODEL), const),
            resident((D_MODEL, 4 * D_MODEL), const),
            resident((D_MODEL, D_MODEL), const),
            resident((D_MODEL, 2 * D_MODEL), const),
            resident((1, D_MODEL), const),
            resident((1, D_MODEL), const),
        ],
        out_specs=[
            pl.BlockSpec((tm, D_MODEL), row),
            pl.BlockSpec((tm, D_MODEL), row),
            pl.BlockSpec((1, HEAD_PAIRS, tm, PAIR_W), lambda r: (r // tiles_per_seq, 0, r % tiles_per_seq, 0)),
            pl.BlockSpec((1, HEAD_PAIRS, tm, PAIR_W), lambda r: (r // tiles_per_seq, 0, r % tiles_per_seq, 0)),
            pl.BlockSpec((1, HEAD_PAIRS, tm // MOBA_BLOCK, 2, V_ROWS, MOBA_BLOCK),
                         lambda r: (r // tiles_per_seq, 0, r % tiles_per_seq, 0, 0, 0)),
            pl.BlockSpec((tm, D_MODEL), row),
            pl.BlockSpec((tm, D_MODEL), row),
        ],
        out_shape=[tok, tok, pair_major, pair_major,
                   jax.ShapeDtypeStruct((batch, HEAD_PAIRS, nb, 2, V_ROWS, MOBA_BLOCK), BF16),
                   tok, tok],
        compiler_params=pltpu.CompilerParams(dimension_semantics=("arbitrary",),
                                             vmem_limit_bytes=VMEM_LIMIT),
        name="in_proj",
    )(x2, g, wa, wvt, wg, lng, lnb)


def _bias_tiles_kernel(tab_ref, o_ref):
    h = pl.program_id(0)
    key = lax.broadcasted_iota(jnp.int32, (MOBA_BLOCK, MOBA_BLOCK), 0)
    qry = lax.broadcasted_iota(jnp.int32, (MOBA_BLOCK, MOBA_BLOCK), 1)
    max_exact = REL_BUCKETS // 2
    last = tab_ref[REL_BUCKETS - 1, h]
    for t in range(NEAR_TILES):
        dist = t * MOBA_BLOCK + qry - key
        n = jnp.maximum(dist, 0)
        val = jnp.full((MOBA_BLOCK, MOBA_BLOCK), tab_ref[0, h], F32)
        for b in range(1, max_exact + 1):
            val = jnp.where(n >= b, tab_ref[b, h], val)
        for kk, thr in enumerate(BUCKET_THRESHOLDS):
            val = jnp.where(n >= thr, tab_ref[max_exact + 1 + kk, h], val)
        val = (val - last) * LOG2E
        if t == 0:
            val = jnp.where(dist >= 0, val, -BIG)
        o_ref[0, t] = val
    o_ref[0, NEAR_TILES] = jnp.zeros((MOBA_BLOCK, MOBA_BLOCK), F32)


def _bias_tiles(rel_bias):
    return pl.pallas_call(
        _bias_tiles_kernel,
        grid=(ATT_HEADS,),
        in_specs=[pl.BlockSpec(memory_space=pltpu.SMEM)],
        out_specs=pl.BlockSpec((1, NEAR_TILES + 1, MOBA_BLOCK, MOBA_BLOCK), lambda h: (h, 0, 0, 0)),
        out_shape=jax.ShapeDtypeStruct((ATT_HEADS, NEAR_TILES + 1, MOBA_BLOCK, MOBA_BLOCK), F32),
        compiler_params=pltpu.CompilerParams(dimension_semantics=("arbitrary",)),
        name="bias_tiles",
    )(rel_bias)


def _attn_kernel(q_ref, k_ref, vt_ref, bias_ref, o_ref, km_ref, qa_ref, sel_ref, acc_ref, s_ref, mt_ref, *, nb):
    i = pl.program_id(2)

    @pl.when(i == 0)
    def _():
        def body(j, c):
            kb = k_ref[0, 0, pl.ds(pl.multiple_of(j * MOBA_BLOCK, MOBA_BLOCK), MOBA_BLOCK), :]
            km_ref[pl.ds(j, 1), :] = jnp.sum(kb.astype(F32), axis=0, keepdims=True) * (1.0 / MOBA_BLOCK)
            return c
        lax.fori_loop(0, nb, body, 0)

    km = km_ref[...]
    km_hi = km.astype(BF16)
    km_lo = (km - km_hi.astype(F32)).astype(BF16)
    q = q_ref[0, 0]
    lane = lax.broadcasted_iota(jnp.int32, (1, PAIR_W), 1)
    gates = []
    for a in range(2):
        qa = jnp.where((lane >= a * HEAD_DIM) & (lane < (a + 1) * HEAD_DIM), q, jnp.zeros_like(q))
        qa_ref[a] = qa
        gates.append(_dot_nt(km_hi, qa) + _dot_nt(km_lo, qa))
    gate = jnp.concatenate(gates, axis=1)
    blk = lax.broadcasted_iota(jnp.int32, (nb, 2 * MOBA_BLOCK), 0)
    gate = jnp.where(blk < i, gate, -jnp.inf)
    sel = blk == i
    for _ in range(MOBA_TOPK):
        top = jnp.max(gate, axis=0, keepdims=True)
        first = jnp.min(jnp.where(gate == top, blk, nb), axis=0, keepdims=True)
        pick = (blk == first) & (top > -jnp.inf)
        sel = sel | pick
        gate = jnp.where(pick, -jnp.inf, gate)
    sel_f = jnp.where(sel, 1.0, 0.0)
    for a in range(2):
        sel_ref[a] = sel_f[:, a * MOBA_BLOCK:(a + 1) * MOBA_BLOCK]
    acc_ref[...] = jnp.zeros_like(acc_ref)

    def block_of(t):
        return jnp.maximum(i - t, 0)

    def score_stage(t, slot):
        j = block_of(t)
        kj = k_ref[0, 0, pl.ds(pl.multiple_of(j * MOBA_BLOCK, MOBA_BLOCK), MOBA_BLOCK), :]
        bt = jnp.minimum(t, NEAR_TILES)
        raw = [_dot_nt(kj, qa_ref[a]) for a in range(2)]
        for a in range(2):
            s = raw[a] + bias_ref[a, bt]
            s_ref[slot, a] = s
            mt_ref[slot, a] = jnp.max(s, axis=0, keepdims=True)

    def softmax_stage(t, slot, m_prev):
        j = block_of(t)
        sel_thr = jnp.where(t <= i, 0.5, 2.0)
        m_next, alphas, pvs = [], [], []
        for a in range(2):
            chosen = sel_ref[a, pl.ds(j, 1), :] > sel_thr
            m_new = jnp.where(chosen, jnp.maximum(m_prev[a], mt_ref[slot, a]), m_prev[a])
            alphas.append(jnp.exp2(m_prev[a] - m_new))
            p = jnp.exp2(s_ref[slot, a] - jnp.where(chosen, m_new, BIG)).astype(BF16)
            pvs.append(_dot(vt_ref[0, 0, j, a], p))
            m_next.append(m_new)
        for a in range(2):
            acc_ref[a] = alphas[a] * acc_ref[a] + pvs[a]
        return tuple(m_next)

    score_stage(0, 0)

    def pair_body(tt, m):
        t0 = 2 * tt
        score_stage(t0 + 1, 1)
        m = softmax_stage(t0, 0, m)
        score_stage(t0 + 2, 0)
        m = softmax_stage(t0 + 1, 1, m)
        return m

    m_init = jnp.full((1, MOBA_BLOCK), -BIG, F32)
    lax.fori_loop(0, (i + 2) // 2, pair_body, (m_init, m_init))

    outs = []
    for a in range(2):
        acc = acc_ref[a]
        outs.append(acc[0:HEAD_DIM] / acc[HEAD_DIM:HEAD_DIM + 1])
    o_ref[0, 0] = jnp.concatenate(outs, axis=0).T.astype(BF16)


def _attention(q, k, vt, bias_t):
    batch, _, seq, _ = q.shape
    nb = seq // MOBA_BLOCK
    return pl.pallas_call(
        functools.partial(_attn_kernel, nb=nb),
        grid=(batch, HEAD_PAIRS, nb),
        in_specs=[
            pl.BlockSpec((1, 1, MOBA_BLOCK, PAIR_W), lambda b, p, i: (b, p, i, 0)),
            pl.BlockSpec((1, 1, seq, PAIR_W), lambda b, p, i: (b, p, 0, 0)),
            pl.BlockSpec((1, 1, nb, 2, V_ROWS, MOBA_BLOCK), lambda b, p, i: (b, p, 0, 0, 0, 0)),
            pl.BlockSpec((2, NEAR_TILES + 1, MOBA_BLOCK, MOBA_BLOCK), lambda b, p, i: (p, 0, 0, 0)),
        ],
        out_specs=pl.BlockSpec((1, 1, MOBA_BLOCK, PAIR_W), lambda b, p, i: (b, p, i, 0)),
        out_shape=jax.ShapeDtypeStruct((batch, HEAD_PAIRS, seq, PAIR_W), BF16),
        scratch_shapes=[
            pltpu.VMEM((nb, PAIR_W), F32),
            pltpu.VMEM((2, MOBA_BLOCK, PAIR_W), BF16),
            pltpu.VMEM((2, nb, MOBA_BLOCK), F32),
            pltpu.VMEM((2, V_ROWS, MOBA_BLOCK), F32),
            pltpu.VMEM((2, 2, MOBA_BLOCK, MOBA_BLOCK), F32),
            pltpu.VMEM((2, 2, 1, MOBA_BLOCK), F32),
        ],
        compiler_params=pltpu.CompilerParams(dimension_semantics=("arbitrary", "arbitrary", "arbitrary"),
                                             vmem_limit_bytes=VMEM_LIMIT),
        name="moba_attention",
    )(q, k, vt, bias_t)


def _post_kernel(x_ref, u_ref, vn_ref, sga_ref, sgb_ref, yb_ref, p_ref,
                 ws_ref, bs_ref, wo_ref, gffn_ref, w1_ref, w2_ref, gple_ref, wpg_ref, wpp_ref, gfin_ref,
                 o_ref, merged_ref, *, tm, final_norm):
    row = lax.broadcasted_iota(jnp.int32, (SGU_CHUNK, SGU_CHUNK), 0)
    col = lax.broadcasted_iota(jnp.int32, (SGU_CHUNK, SGU_CHUNK), 1)
    for g in range(SGU_GROUPS):
        w = jnp.where(row >= col, ws_ref[g], 0.0).astype(BF16)
        b = bs_ref[g]
        cols = slice(g * LANES, (g + 1) * LANES)
        for t in range(tm // SGU_CHUNK):
            rows = slice(t * SGU_CHUNK, (t + 1) * SGU_CHUNK)
            mixed = _dot(w, vn_ref[rows, cols]) + b
            y_a = u_ref[rows, cols].astype(F32) * mixed
            y_b = yb_ref[0, g, rows, :].astype(F32)
            merged = sga_ref[rows, cols].astype(F32) * y_a + sgb_ref[rows, cols].astype(F32) * y_b
            merged_ref[rows, cols] = merged.astype(BF16)

    x1 = x_ref[...] + _dot(merged_ref[...], wo_ref[...])

    h = _rms(x1, gffn_ref[...]).astype(BF16)
    x2 = x1
    for c in range(D_FF // D_MODEL):
        cs = slice(c * D_MODEL, (c + 1) * D_MODEL)
        a = jnp.square(jnp.maximum(_dot(h, w1_ref[:, cs]), 0.0)).astype(BF16)
        x2 = x2 + _dot(a, w2_ref[cs, :])

    gate = jax.nn.sigmoid(_dot(_rms(x2, gple_ref[...]).astype(BF16), wpg_ref[...]))
    x3 = x2 + gate * _dot(p_ref[...].astype(BF16), wpp_ref[...])
    o_ref[...] = _rms(x3, gfin_ref[...]) if final_norm else x3


def _post(x2, u, vn, sga, sgb, yb, p2, ws, bs, wo, gffn, w1, w2, gple, wpg, wpp, gfin, *, seq, tm, final_norm):
    n = x2.shape[0]
    tiles_per_seq = seq // tm
    row = lambda r: (r, 0)
    resident = functools.partial(pl.BlockSpec, pipeline_mode=pl.Buffered(1))
    c2 = lambda r: (0, 0)
    c3 = lambda r: (0, 0, 0)
    tokb = pl.BlockSpec((tm, D_MODEL), row)
    return pl.pallas_call(
        functools.partial(_post_kernel, tm=tm, final_norm=final_norm),
        grid=(n // tm,),
        in_specs=[
            tokb, tokb, tokb, tokb, tokb,
            pl.BlockSpec((1, HEAD_PAIRS, tm, PAIR_W), lambda r: (r // tiles_per_seq, 0, r % tiles_per_seq, 0)),
            pl.BlockSpec((tm, PLE_DIM), row),
            resident((SGU_GROUPS, SGU_CHUNK, SGU_CHUNK), c3),
            resident((SGU_GROUPS, SGU_CHUNK, 1), c3),
            resident((D_MODEL, D_MODEL), c2),
            resident((1, D_MODEL), c2),
            resident((D_MODEL, D_FF), c2),
            resident((D_FF, D_MODEL), c2),
            resident((1, D_MODEL), c2),
            resident((D_MODEL, D_MODEL), c2),
            resident((PLE_DIM, D_MODEL), c2),
            resident((1, D_MODEL), c2),
        ],
        out_specs=pl.BlockSpec((tm, D_MODEL), row),
        out_shape=jax.ShapeDtypeStruct((n, D_MODEL), F32),
        scratch_shapes=[pltpu.VMEM((tm, D_MODEL), BF16)],
        compiler_params=pltpu.CompilerParams(dimension_semantics=("arbitrary",),
                                             vmem_limit_bytes=VMEM_LIMIT),
        name="post",
    )(x2, u, vn, sga, sgb, yb, p2, ws, bs, wo, gffn, w1, w2, gple, wpg, wpp, gfin)


def kernel(x, p, norm_mix_g, w_in, w_sgu_spatial, b_sgu_spatial, ln_v_g, ln_v_b, rel_bias, w_out, norm_ffn_g,
           w_ff1, w_ff2, norm_ple_g, w_ple_gate, w_ple_proj, norm_final_g):
    batch, seq, _ = x.shape
    depth = w_in.shape[0]
    n = batch * seq
    assert seq % MOBA_BLOCK == 0 and seq // MOBA_BLOCK >= MOBA_TOPK
    tm_in = 512
    tm_post = 256
    vec = lambda g: g.reshape(1, D_MODEL)

    bias_t = _bias_tiles(rel_bias)
    x2 = x.reshape(n, D_MODEL)
    for i in range(depth):
        wi = w_in[i].astype(BF16)
        wa = wi[:, :4 * D_MODEL]
        wvt = wi[:, 4 * D_MODEL:5 * D_MODEL].T
        wg = wi[:, 5 * D_MODEL:]
        u, vn, q, k, vt, sga, sgb = _in_proj(
            x2, vec(norm_mix_g[i]), wa, wvt, wg, vec(ln_v_g[i]), vec(ln_v_b[i]),
            batch=batch, seq=seq, tm=tm_in)
        yb = _attention(q, k, vt, bias_t)
        x2 = _post(
            x2, u, vn, sga, sgb, yb, p[i].reshape(n, PLE_DIM),
            w_sgu_spatial[i], b_sgu_spatial[i].reshape(SGU_GROUPS, SGU_CHUNK, 1),
            w_out[i].astype(BF16), vec(norm_ffn_g[i]), w_ff1[i].astype(BF16), w_ff2[i].astype(BF16),
            vec(norm_ple_g[i]), w_ple_gate[i].astype(BF16), w_ple_proj[i].astype(BF16),
            vec(norm_final_g), seq=seq, tm=tm_post, final_norm=(i == depth - 1))
    return x2.reshape(batch, seq, D_MODEL)
```

```python
import functools
import math

import jax
import jax.numpy as jnp
from jax import lax
from jax.experimental import pallas as pl
from jax.experimental.pallas import tpu as pltpu

D_MODEL = 1024
PLE_DIM = 256
SGU_CHUNK = 128
SGU_GROUPS = 8
ATT_HEADS = 16
HEAD_DIM = 64
MOBA_BLOCK = 256
MOBA_TOPK = 3
REL_BUCKETS = 32
REL_MAX_DIST = 1024
D_FF = 4 * D_MODEL
EPS = 1e-6

LANES = 128
BF16_ROWS = 16
HEAD_PAIRS = ATT_HEADS // 2
PAIR_W = 2 * HEAD_DIM
V_ROWS = HEAD_DIM + BF16_ROWS
Q_BLOCKS = 2
BIG = 1e30
VMEM_LIMIT = 60 * 1024 * 1024
LOG2E = math.log2(math.e)

F32 = jnp.float32
BF16 = jnp.bfloat16


def _bucket_thresholds():
    max_exact = REL_BUCKETS // 2
    n_log = REL_BUCKETS - max_exact
    ratio = REL_MAX_DIST // max_exact
    out = []
    for k in range(1, n_log):
        d = max_exact
        target = (max_exact ** n_log) * (ratio ** k)
        while d ** n_log < target:
            d += 1
        out.append(d)
    return tuple(out)


BUCKET_THRESHOLDS = _bucket_thresholds()
NEAR_TILES = -(-(BUCKET_THRESHOLDS[-1] + MOBA_BLOCK - 1) // MOBA_BLOCK)
assert NEAR_TILES == 5


def _dot(a, b):
    return jnp.dot(a, b, preferred_element_type=F32)


def _dot_nt(a, b):
    return lax.dot_general(a, b, (((1,), (1,)), ((), ())), preferred_element_type=F32)


def _rms(x, g):
    return x * lax.rsqrt(jnp.mean(x * x, axis=-1, keepdims=True) + EPS) * g


def _in_proj_kernel(x_ref, g_ref, wa_ref, wvt_ref, wg_ref, lng_ref, lnb_ref,
                    u_ref, vn_ref, q_ref, k_ref, vt_ref, sga_ref, sgb_ref, *, tm):
    h = _rms(x_ref[...], g_ref[...]).astype(BF16)

    u_ref[...] = jax.nn.gelu(_dot(h, wa_ref[:, 0:D_MODEL])).astype(BF16)

    gv = jax.nn.gelu(_dot(h, wa_ref[:, D_MODEL:2 * D_MODEL]))
    mu = jnp.mean(gv, axis=-1, keepdims=True)
    gc = gv - mu
    vn = gc * lax.rsqrt(jnp.mean(gc * gc, axis=-1, keepdims=True) + EPS)
    vn_ref[...] = (vn * lng_ref[...] + lnb_ref[...]).astype(BF16)

    q = _dot(h, wa_ref[:, 2 * D_MODEL:3 * D_MODEL]) * (HEAD_DIM ** -0.5 * LOG2E)
    k = _dot(h, wa_ref[:, 3 * D_MODEL:4 * D_MODEL])
    for p in range(HEAD_PAIRS):
        q_ref[0, p] = q[:, p * PAIR_W:(p + 1) * PAIR_W].astype(BF16)
        k_ref[0, p] = k[:, p * PAIR_W:(p + 1) * PAIR_W].astype(BF16)

    vt = _dot_nt(wvt_ref[...], h).astype(BF16)
    ones = jnp.ones((BF16_ROWS, MOBA_BLOCK), BF16)
    for p in range(HEAD_PAIRS):
        for jb in range(tm // MOBA_BLOCK):
            for a in range(2):
                r0 = p * PAIR_W + a * HEAD_DIM
                vt_ref[0, p, jb, a, 0:HEAD_DIM, :] = vt[r0:r0 + HEAD_DIM, jb * MOBA_BLOCK:(jb + 1) * MOBA_BLOCK]
                vt_ref[0, p, jb, a, HEAD_DIM:V_ROWS, :] = ones

    sga_ref[...] = jax.nn.sigmoid(_dot(h, wg_ref[:, 0:D_MODEL])).astype(BF16)
    sgb_ref[...] = jax.nn.sigmoid(_dot(h, wg_ref[:, D_MODEL:2 * D_MODEL])).astype(BF16)


def _in_proj(x2, g, wa, wvt, wg, lng, lnb, *, batch, seq, tm):
    n = x2.shape[0]
    tiles_per_seq = seq // tm
    nb = seq // MOBA_BLOCK
    row = lambda r: (r, 0)
    const = lambda r: (0, 0)
    resident = functools.partial(pl.BlockSpec, pipeline_mode=pl.Buffered(1))
    tok = jax.ShapeDtypeStruct((n, D_MODEL), BF16)
    pair_major = jax.ShapeDtypeStruct((batch, HEAD_PAIRS, seq, PAIR_W), BF16)
    return pl.pallas_call(
        functools.partial(_in_proj_kernel, tm=tm),
        grid=(n // tm,),
        in_specs=[
            pl.BlockSpec((tm, D_MODEL), row),
            resident((1, D_MODEL), const),
            resident((D_MODEL, 4 * D_MODEL), const),
            resident((D_MODEL, D_MODEL), const),
            resident((D_MODEL, 2 * D_MODEL), const),
            resident((1, D_MODEL), const),
            resident((1, D_MODEL), const),
        ],
        out_specs=[
            pl.BlockSpec((tm, D_MODEL), row),
            pl.BlockSpec((tm, D_MODEL), row),
            pl.BlockSpec((1, HEAD_PAIRS, tm, PAIR_W), lambda r: (r // tiles_per_seq, 0, r % tiles_per_seq, 0)),
            pl.BlockSpec((1, HEAD_PAIRS, tm, PAIR_W), lambda r: (r // tiles_per_seq, 0, r % tiles_per_seq, 0)),
            pl.BlockSpec((1, HEAD_PAIRS, tm // MOBA_BLOCK, 2, V_ROWS, MOBA_BLOCK),
                         lambda r: (r // tiles_per_seq, 0, r % tiles_per_seq, 0, 0, 0)),
            pl.BlockSpec((tm, D_MODEL), row),
            pl.BlockSpec((tm, D_MODEL), row),
        ],
        out_shape=[tok, tok, pair_major, pair_major,
                   jax.ShapeDtypeStruct((batch, HEAD_PAIRS, nb, 2, V_ROWS, MOBA_BLOCK), BF16),
                   tok, tok],
        compiler_params=pltpu.CompilerParams(dimension_semantics=("arbitrary",),
                                             vmem_limit_bytes=VMEM_LIMIT),
        name="in_proj",
    )(x2, g, wa, wvt, wg, lng, lnb)


def _bias_tiles_kernel(tab_ref, o_ref):
    h = pl.program_id(0)
    key = lax.broadcasted_iota(jnp.int32, (MOBA_BLOCK, MOBA_BLOCK), 0)
    qry = lax.broadcasted_iota(jnp.int32, (MOBA_BLOCK, MOBA_BLOCK), 1)
    max_exact = REL_BUCKETS // 2
    last = tab_ref[REL_BUCKETS - 1, h]
    for t in range(NEAR_TILES):
        dist = t * MOBA_BLOCK + qry - key
        n = jnp.maximum(dist, 0)
        val = jnp.full((MOBA_BLOCK, MOBA_BLOCK), tab_ref[0, h], F32)
        for b in range(1, max_exact + 1):
            val = jnp.where(n >= b, tab_ref[b, h], val)
        for kk, thr in enumerate(BUCKET_THRESHOLDS):
            val = jnp.where(n >= thr, tab_ref[max_exact + 1 + kk, h], val)
        val = (val - last) * LOG2E
        if t == 0:
            val = jnp.where(dist >= 0, val, -BIG)
        o_ref[0, t] = val
    o_ref[0, NEAR_TILES] = jnp.zeros((MOBA_BLOCK, MOBA_BLOCK), F32)


def _bias_tiles(rel_bias):
    return pl.pallas_call(
        _bias_tiles_kernel,
        grid=(ATT_HEADS,),
        in_specs=[pl.BlockSpec(memory_space=pltpu.SMEM)],
        out_specs=pl.BlockSpec((1, NEAR_TILES + 1, MOBA_BLOCK, MOBA_BLOCK), lambda h: (h, 0, 0, 0)),
        out_shape=jax.ShapeDtypeStruct((ATT_HEADS, NEAR_TILES + 1, MOBA_BLOCK, MOBA_BLOCK), F32),
        compiler_params=pltpu.CompilerParams(dimension_semantics=("arbitrary",)),
        name="bias_tiles",
    )(rel_bias)


def _attn_kernel(q_ref, k_ref, vt_ref, bias_ref, o_ref, km_ref, qa_ref, sel_ref, acc_ref, s_ref, mt_ref, *, nb):
    step = pl.program_id(2)
    j_last = step * Q_BLOCKS + (Q_BLOCKS - 1)
    chains = [(h, a) for h in range(Q_BLOCKS) for a in range(2)]

    @pl.when(step == 0)
    def _():
        def body(j, c):
            kb = k_ref[0, 0, pl.ds(pl.multiple_of(j * MOBA_BLOCK, MOBA_BLOCK), MOBA_BLOCK), :]
            km_ref[pl.ds(j, 1), :] = jnp.sum(kb.astype(F32), axis=0, keepdims=True) * (1.0 / MOBA_BLOCK)
            return c
        lax.fori_loop(0, nb, body, 0)

    km = km_ref[...]
    km_hi = km.astype(BF16)
    km_lo = (km - km_hi.astype(F32)).astype(BF16)
    lane = lax.broadcasted_iota(jnp.int32, (1, PAIR_W), 1)
    gates, owns = [], []
    for c, (h, a) in enumerate(chains):
        q = q_ref[0, 0, h * MOBA_BLOCK:(h + 1) * MOBA_BLOCK, :]
        qa = jnp.where((lane >= a * HEAD_DIM) & (lane < (a + 1) * HEAD_DIM), q, jnp.zeros_like(q))
        qa_ref[c] = qa
        gates.append(_dot_nt(km_hi, qa) + _dot_nt(km_lo, qa))
        owns.append(jnp.full((1, MOBA_BLOCK), step * Q_BLOCKS + h, jnp.int32))
    gate = jnp.concatenate(gates, axis=1)
    own = jnp.concatenate(owns, axis=1)
    blk = lax.broadcasted_iota(jnp.int32, gate.shape, 0)
    gate = jnp.where(blk < own, gate, -jnp.inf)
    sel = blk == own
    for _ in range(MOBA_TOPK):
        top = jnp.max(gate, axis=0, keepdims=True)
        first = jnp.min(jnp.where(gate == top, blk, nb), axis=0, keepdims=True)
        pick = (blk == first) & (top > -jnp.inf)
        sel = sel | pick
        gate = jnp.where(pick, -jnp.inf, gate)
    sel_f = jnp.where(sel, 1.0, 0.0)
    for c in range(len(chains)):
        sel_ref[c] = sel_f[:, c * MOBA_BLOCK:(c + 1) * MOBA_BLOCK]
    acc_ref[...] = jnp.zeros_like(acc_ref)

    def block_of(t):
        return jnp.maximum(j_last - t, 0)

    def score_stage(t, slot):
        j = block_of(t)
        kj = k_ref[0, 0, pl.ds(pl.multiple_of(j * MOBA_BLOCK, MOBA_BLOCK), MOBA_BLOCK), :]
        raw = [_dot_nt(kj, qa_ref[c]) for c in range(len(chains))]
        for c, (h, a) in enumerate(chains):
            bt = jnp.clip(t - (Q_BLOCKS - 1 - h), 0, NEAR_TILES)
            s = raw[c] + bias_ref[a, bt]
            s_ref[slot, c] = s
            mt_ref[slot, c] = jnp.max(s, axis=0, keepdims=True)

    def softmax_stage(t, slot, m_prev):
        j = block_of(t)
        sel_thr = jnp.where(t <= j_last, 0.5, 2.0)
        m_next, alphas, pvs = [], [], []
        for c, (h, a) in enumerate(chains):
            chosen = sel_ref[c, pl.ds(j, 1), :] > sel_thr
            m_new = jnp.where(chosen, jnp.maximum(m_prev[c], mt_ref[slot, c]), m_prev[c])
            alphas.append(jnp.exp2(m_prev[c] - m_new))
            p = jnp.exp2(s_ref[slot, c] - jnp.where(chosen, m_new, BIG)).astype(BF16)
            pvs.append(_dot(vt_ref[0, 0, j, a], p))
            m_next.append(m_new)
        for c in range(len(chains)):
            acc_ref[c] = alphas[c] * acc_ref[c] + pvs[c]
        return tuple(m_next)

    score_stage(0, 0)

    def pair_body(tt, m):
        t0 = 2 * tt
        score_stage(t0 + 1, 1)
        m = softmax_stage(t0, 0, m)
        score_stage(t0 + 2, 0)
        m = softmax_stage(t0 + 1, 1, m)
        return m

    m_init = jnp.full((1, MOBA_BLOCK), -BIG, F32)
    lax.fori_loop(0, (j_last + 2) // 2, pair_body, (m_init,) * len(chains))

    for h in range(Q_BLOCKS):
        outs = []
        for a in range(2):
            acc = acc_ref[2 * h + a]
            outs.append(acc[0:HEAD_DIM] / acc[HEAD_DIM:HEAD_DIM + 1])
        o_ref[0, 0, h * MOBA_BLOCK:(h + 1) * MOBA_BLOCK, :] = jnp.concatenate(outs, axis=0).T.astype(BF16)


def _attention(q, k, vt, bias_t):
    batch, _, seq, _ = q.shape
    nb = seq // MOBA_BLOCK
    tq = Q_BLOCKS * MOBA_BLOCK
    n_chains = 2 * Q_BLOCKS
    return pl.pallas_call(
        functools.partial(_attn_kernel, nb=nb),
        grid=(batch, HEAD_PAIRS, seq // tq),
        in_specs=[
            pl.BlockSpec((1, 1, tq, PAIR_W), lambda b, p, i: (b, p, i, 0)),
            pl.BlockSpec((1, 1, seq, PAIR_W), lambda b, p, i: (b, p, 0, 0)),
            pl.BlockSpec((1, 1, nb, 2, V_ROWS, MOBA_BLOCK), lambda b, p, i: (b, p, 0, 0, 0, 0)),
            pl.BlockSpec((2, NEAR_TILES + 1, MOBA_BLOCK, MOBA_BLOCK), lambda b, p, i: (p, 0, 0, 0)),
        ],
        out_specs=pl.BlockSpec((1, 1, tq, PAIR_W), lambda b, p, i: (b, p, i, 0)),
        out_shape=jax.ShapeDtypeStruct((batch, HEAD_PAIRS, seq, PAIR_W), BF16),
        scratch_shapes=[
            pltpu.VMEM((nb, PAIR_W), F32),
            pltpu.VMEM((n_chains, MOBA_BLOCK, PAIR_W), BF16),
            pltpu.VMEM((n_chains, nb, MOBA_BLOCK), F32),
            pltpu.VMEM((n_chains, V_ROWS, MOBA_BLOCK), F32),
            pltpu.VMEM((2, n_chains, MOBA_BLOCK, MOBA_BLOCK), F32),
            pltpu.VMEM((2, n_chains, 1, MOBA_BLOCK), F32),
        ],
        compiler_params=pltpu.CompilerParams(dimension_semantics=("arbitrary", "arbitrary", "arbitrary"),
                                             vmem_limit_bytes=VMEM_LIMIT),
        name="moba_attention",
    )(q, k, vt, bias_t)


def _post_kernel(x_ref, u_ref, vn_ref, sga_ref, sgb_ref, yb_ref, p_ref,
                 ws_ref, bs_ref, wo_ref, gffn_ref, w1_ref, w2_ref, gple_ref, wpg_ref, wpp_ref, gfin_ref,
                 o_ref, merged_ref, *, tm, final_norm):
    row = lax.broadcasted_iota(jnp.int32, (SGU_CHUNK, SGU_CHUNK), 0)
    col = lax.broadcasted_iota(jnp.int32, (SGU_CHUNK, SGU_CHUNK), 1)
    for g in range(SGU_GROUPS):
        w = jnp.where(row >= col, ws_ref[g], 0.0).astype(BF16)
        b = bs_ref[g]
        cols = slice(g * LANES, (g + 1) * LANES)
        for t in range(tm // SGU_CHUNK):
            rows = slice(t * SGU_CHUNK, (t + 1) * SGU_CHUNK)
            mixed = _dot(w, vn_ref[rows, cols]) + b
            y_a = u_ref[rows, cols].astype(F32) * mixed
            y_b = yb_ref[0, g, rows, :].astype(F32)
            merged = sga_ref[rows, cols].astype(F32) * y_a + sgb_ref[rows, cols].astype(F32) * y_b
            merged_ref[rows, cols] = merged.astype(BF16)

    x1 = x_ref[...] + _dot(merged_ref[...], wo_ref[...])

    h = _rms(x1, gffn_ref[...]).astype(BF16)
    x2 = x1
    for c in range(D_FF // D_MODEL):
        cs = slice(c * D_MODEL, (c + 1) * D_MODEL)
        a = jnp.square(jnp.maximum(_dot(h, w1_ref[:, cs]), 0.0)).astype(BF16)
        x2 = x2 + _dot(a, w2_ref[cs, :])

    gate = jax.nn.sigmoid(_dot(_rms(x2, gple_ref[...]).astype(BF16), wpg_ref[...]))
    x3 = x2 + gate * _dot(p_ref[...].astype(BF16), wpp_ref[...])
    o_ref[...] = _rms(x3, gfin_ref[...]) if final_norm else x3


def _post(x2, u, vn, sga, sgb, yb, p2, ws, bs, wo, gffn, w1, w2, gple, wpg, wpp, gfin, *, seq, tm, final_norm):
    n = x2.shape[0]
    tiles_per_seq = seq // tm
    row = lambda r: (r, 0)
    resident = functools.partial(pl.BlockSpec, pipeline_mode=pl.Buffered(1))
    c2 = lambda r: (0, 0)
    c3 = lambda r: (0, 0, 0)
    tokb = pl.BlockSpec((tm, D_MODEL), row)
    return pl.pallas_call(
        functools.partial(_post_kernel, tm=tm, final_norm=final_norm),
        grid=(n // tm,),
        in_specs=[
            tokb, tokb, tokb, tokb, tokb,
            pl.BlockSpec((1, HEAD_PAIRS, tm, PAIR_W), lambda r: (r // tiles_per_seq, 0, r % tiles_per_seq, 0)),
            pl.BlockSpec((tm, PLE_DIM), row),
            resident((SGU_GROUPS, SGU_CHUNK, SGU_CHUNK), c3),
            resident((SGU_GROUPS, SGU_CHUNK, 1), c3),
            resident((D_MODEL, D_MODEL), c2),
            resident((1, D_MODEL), c2),
            resident((D_MODEL, D_FF), c2),
            resident((D_FF, D_MODEL), c2),
            resident((1, D_MODEL), c2),
            resident((D_MODEL, D_MODEL), c2),
            resident((PLE_DIM, D_MODEL), c2),
            resident((1, D_MODEL), c2),
        ],
        out_specs=pl.BlockSpec((tm, D_MODEL), row),
        out_shape=jax.ShapeDtypeStruct((n, D_MODEL), F32),
        scratch_shapes=[pltpu.VMEM((tm, D_MODEL), BF16)],
        compiler_params=pltpu.CompilerParams(dimension_semantics=("arbitrary",),
                                             vmem_limit_bytes=VMEM_LIMIT),
        name="post",
    )(x2, u, vn, sga, sgb, yb, p2, ws, bs, wo, gffn, w1, w2, gple, wpg, wpp, gfin)


def kernel(x, p, norm_mix_g, w_in, w_sgu_spatial, b_sgu_spatial, ln_v_g, ln_v_b, rel_bias, w_out, norm_ffn_g,
           w_ff1, w_ff2, norm_ple_g, w_ple_gate, w_ple_proj, norm_final_g):
    batch, seq, _ = x.shape
    depth = w_in.shape[0]
    n = batch * seq
    assert seq % (Q_BLOCKS * MOBA_BLOCK) == 0 and seq // MOBA_BLOCK >= MOBA_TOPK
    tm_in = 512
    tm_post = 256
    vec = lambda g: g.reshape(1, D_MODEL)

    bias_t = _bias_tiles(rel_bias)
    x2 = x.reshape(n, D_MODEL)
    for i in range(depth):
        wi = w_in[i].astype(BF16)
        wa = wi[:, :4 * D_MODEL]
        wvt = wi[:, 4 * D_MODEL:5 * D_MODEL].T
        wg = wi[:, 5 * D_MODEL:]
        u, vn, q, k, vt, sga, sgb = _in_proj(
            x2, vec(norm_mix_g[i]), wa, wvt, wg, vec(ln_v_g[i]), vec(ln_v_b[i]),
            batch=batch, seq=seq, tm=tm_in)
        yb = _attention(q, k, vt, bias_t)
        x2 = _post(
            x2, u, vn, sga, sgb, yb, p[i].reshape(n, PLE_DIM),
            w_sgu_spatial[i], b_sgu_spatial[i].reshape(SGU_GROUPS, SGU_CHUNK, 1),
            w_out[i].astype(BF16), vec(norm_ffn_g[i]), w_ff1[i].astype(BF16), w_ff2[i].astype(BF16),
            vec(norm_ple_g[i]), w_ple_gate[i].astype(BF16), w_ple_proj[i].astype(BF16),
            vec(norm_final_g), seq=seq, tm=tm_post, final_norm=(i == depth - 1))
    return x2.reshape(batch, seq, D_MODEL)
```

```python
import functools
import math

import jax
import jax.numpy as jnp
from jax import lax
from jax.experimental import pallas as pl
from jax.experimental.pallas import tpu as pltpu

D_MODEL = 1024
PLE_DIM = 256
SGU_CHUNK = 128
SGU_GROUPS = 8
ATT_HEADS = 16
HEAD_DIM = 64
MOBA_BLOCK = 256
MOBA_TOPK = 3
REL_BUCKETS = 32
REL_MAX_DIST = 1024
D_FF = 4 * D_MODEL
EPS = 1e-6

LANES = 128
BF16_ROWS = 16
HEAD_PAIRS = ATT_HEADS // 2
PAIR_W = 2 * HEAD_DIM
V_ROWS = HEAD_DIM + BF16_ROWS
Q_BLOCKS = 2
BIG = 1e30
VMEM_LIMIT = 60 * 1024 * 1024
LOG2E = math.log2(math.e)

F32 = jnp.float32
BF16 = jnp.bfloat16


def _bucket_thresholds():
    max_exact = REL_BUCKETS // 2
    n_log = REL_BUCKETS - max_exact
    ratio = REL_MAX_DIST // max_exact
    out = []
    for k in range(1, n_log):
        d = max_exact
        target = (max_exact ** n_log) * (ratio ** k)
        while d ** n_log < target:
            d += 1
        out.append(d)
    return tuple(out)


BUCKET_THRESHOLDS = _bucket_thresholds()
NEAR_TILES = -(-(BUCKET_THRESHOLDS[-1] + MOBA_BLOCK - 1) // MOBA_BLOCK)
assert NEAR_TILES == 5


def _dot(a, b):
    return jnp.dot(a, b, preferred_element_type=F32)


def _dot_nt(a, b):
    return lax.dot_general(a, b, (((1,), (1,)), ((), ())), preferred_element_type=F32)


def _rms(x, g):
    return x * lax.rsqrt(jnp.mean(x * x, axis=-1, keepdims=True) + EPS) * g


def _in_proj_kernel(x_ref, g_ref, wa_ref, wvt_ref, wg_ref, lng_ref, lnb_ref,
                    u_ref, vn_ref, q_ref, k_ref, vt_ref, sga_ref, sgb_ref, *, tm):
    h = _rms(x_ref[...], g_ref[...]).astype(BF16)

    u_ref[...] = jax.nn.gelu(_dot(h, wa_ref[:, 0:D_MODEL])).astype(BF16)

    gv = jax.nn.gelu(_dot(h, wa_ref[:, D_MODEL:2 * D_MODEL]))
    mu = jnp.mean(gv, axis=-1, keepdims=True)
    gc = gv - mu
    vn = gc * lax.rsqrt(jnp.mean(gc * gc, axis=-1, keepdims=True) + EPS)
    vn_ref[...] = (vn * lng_ref[...] + lnb_ref[...]).astype(BF16)

    q = _dot(h, wa_ref[:, 2 * D_MODEL:3 * D_MODEL]) * (HEAD_DIM ** -0.5 * LOG2E)
    k = _dot(h, wa_ref[:, 3 * D_MODEL:4 * D_MODEL])
    for p in range(HEAD_PAIRS):
        q_ref[0, p] = q[:, p * PAIR_W:(p + 1) * PAIR_W].astype(BF16)
        k_ref[0, p] = k[:, p * PAIR_W:(p + 1) * PAIR_W].astype(BF16)

    vt = _dot_nt(wvt_ref[...], h).astype(BF16)
    ones = jnp.ones((BF16_ROWS, MOBA_BLOCK), BF16)
    for p in range(HEAD_PAIRS):
        for jb in range(tm // MOBA_BLOCK):
            for a in range(2):
                r0 = p * PAIR_W + a * HEAD_DIM
                vt_ref[0, p, jb, a, 0:HEAD_DIM, :] = vt[r0:r0 + HEAD_DIM, jb * MOBA_BLOCK:(jb + 1) * MOBA_BLOCK]
                vt_ref[0, p, jb, a, HEAD_DIM:V_ROWS, :] = ones

    sga_ref[...] = jax.nn.sigmoid(_dot(h, wg_ref[:, 0:D_MODEL])).astype(BF16)
    sgb_ref[...] = jax.nn.sigmoid(_dot(h, wg_ref[:, D_MODEL:2 * D_MODEL])).astype(BF16)


def _in_proj(x2, g, wa, wvt, wg, lng, lnb, *, batch, seq, tm):
    n = x2.shape[0]
    tiles_per_seq = seq // tm
    nb = seq // MOBA_BLOCK
    row = lambda r: (r, 0)
    const = lambda r: (0, 0)
    resident = functools.partial(pl.BlockSpec, pipeline_mode=pl.Buffered(1))
    tok = jax.ShapeDtypeStruct((n, D_MODEL), BF16)
    pair_major = jax.ShapeDtypeStruct((batch, HEAD_PAIRS, seq, PAIR_W), BF16)
    return pl.pallas_call(
        functools.partial(_in_proj_kernel, tm=tm),
        grid=(n // tm,),
        in_specs=[
            pl.BlockSpec((tm, D_MODEL), row),
            resident((1, D_MODEL), const),
            resident((D_MODEL, 4 * D_MODEL), const),
            resident((D_MODEL, D_MODEL), const),
            resident((D_MODEL, 2 * D_MODEL), const),
            resident((1, D_MODEL), const),
            resident((1, D_MODEL), const),
        ],
        out_specs=[
            pl.BlockSpec((tm, D_MODEL), row),
            pl.BlockSpec((tm, D_MODEL), row),
            pl.BlockSpec((1, HEAD_PAIRS, tm, PAIR_W), lambda r: (r // tiles_per_seq, 0, r % tiles_per_seq, 0)),
            pl.BlockSpec((1, HEAD_PAIRS, tm, PAIR_W), lambda r: (r // tiles_per_seq, 0, r % tiles_per_seq, 0)),
            pl.BlockSpec((1, HEAD_PAIRS, tm // MOBA_BLOCK, 2, V_ROWS, MOBA_BLOCK),
                         lambda r: (r // tiles_per_seq, 0, r % tiles_per_seq, 0, 0, 0)),
            pl.BlockSpec((tm, D_MODEL), row),
            pl.BlockSpec((tm, D_MODEL), row),
        ],
        out_shape=[tok, tok, pair_major, pair_major,
                   jax.ShapeDtypeStruct((batch, HEAD_PAIRS, nb, 2, V_ROWS, MOBA_BLOCK), BF16),
                   tok, tok],
        compiler_params=pltpu.CompilerParams(dimension_semantics=("arbitrary",),
                                             vmem_limit_bytes=VMEM_LIMIT),
        name="in_proj",
    )(x2, g, wa, wvt, wg, lng, lnb)


def _bias_tiles_kernel(tab_ref, o_ref):
    h = pl.program_id(0)
    key = lax.broadcasted_iota(jnp.int32, (MOBA_BLOCK, MOBA_BLOCK), 0)
    qry = lax.broadcasted_iota(jnp.int32, (MOBA_BLOCK, MOBA_BLOCK), 1)
    max_exact = REL_BUCKETS // 2
    last = tab_ref[REL_BUCKETS - 1, h]
    for t in range(NEAR_TILES):
        dist = t * MOBA_BLOCK + qry - key
        n = jnp.maximum(dist, 0)
        val = jnp.full((MOBA_BLOCK, MOBA_BLOCK), tab_ref[0, h], F32)
        for b in range(1, max_exact + 1):
            val = jnp.where(n >= b, tab_ref[b, h], val)
        for kk, thr in enumerate(BUCKET_THRESHOLDS):
            val = jnp.where(n >= thr, tab_ref[max_exact + 1 + kk, h], val)
        val = (val - last) * LOG2E
        if t == 0:
            val = jnp.where(dist >= 0, val, -BIG)
        o_ref[0, t] = val
    o_ref[0, NEAR_TILES] = jnp.zeros((MOBA_BLOCK, MOBA_BLOCK), F32)


def _bias_tiles(rel_bias):
    return pl.pallas_call(
        _bias_tiles_kernel,
        grid=(ATT_HEADS,),
        in_specs=[pl.BlockSpec(memory_space=pltpu.SMEM)],
        out_specs=pl.BlockSpec((1, NEAR_TILES + 1, MOBA_BLOCK, MOBA_BLOCK), lambda h: (h, 0, 0, 0)),
        out_shape=jax.ShapeDtypeStruct((ATT_HEADS, NEAR_TILES + 1, MOBA_BLOCK, MOBA_BLOCK), F32),
        compiler_params=pltpu.CompilerParams(dimension_semantics=("arbitrary",)),
        name="bias_tiles",
    )(rel_bias)


def _attn_kernel(q_ref, k_ref, vt_ref, bias_ref, o_ref,
                 km_ref, kms_ref, qa_ref, sel_ref, acc_ref, s_ref, mt_ref, *, nb):
    step = pl.program_id(2)
    j_last = step * Q_BLOCKS + (Q_BLOCKS - 1)
    chains = [(h, a) for h in range(Q_BLOCKS) for a in range(2)]
    lane = lax.broadcasted_iota(jnp.int32, (1, PAIR_W), 1)

    @pl.when(step == 0)
    def _():
        def body(j, c):
            kb = k_ref[0, 0, pl.ds(pl.multiple_of(j * MOBA_BLOCK, MOBA_BLOCK), MOBA_BLOCK), :]
            km_ref[pl.ds(j, 1), :] = jnp.sum(kb.astype(F32), axis=0, keepdims=True) * (1.0 / MOBA_BLOCK)
            return c
        lax.fori_loop(0, nb, body, 0)
        km = km_ref[...]
        km_hi = km.astype(BF16)
        km_lo = (km - km_hi.astype(F32)).astype(BF16)
        for a in range(2):
            in_head = (lane >= a * HEAD_DIM) & (lane < (a + 1) * HEAD_DIM)
            kms_ref[(2 * a) * nb:(2 * a + 1) * nb, :] = jnp.where(in_head, km_hi, jnp.zeros_like(km_hi))
            kms_ref[(2 * a + 1) * nb:(2 * a + 2) * nb, :] = jnp.where(in_head, km_lo, jnp.zeros_like(km_lo))

    gates, owns = [], []
    for h in range(Q_BLOCKS):
        q = q_ref[0, 0, h * MOBA_BLOCK:(h + 1) * MOBA_BLOCK, :]
        g = _dot_nt(kms_ref[...], q)
        for a in range(2):
            qa_ref[2 * h + a] = jnp.where((lane >= a * HEAD_DIM) & (lane < (a + 1) * HEAD_DIM), q, jnp.zeros_like(q))
            gates.append(g[(2 * a) * nb:(2 * a + 1) * nb] + g[(2 * a + 1) * nb:(2 * a + 2) * nb])
            owns.append(jnp.full((1, MOBA_BLOCK), step * Q_BLOCKS + h, jnp.int32))
    gate = jnp.concatenate(gates, axis=1)
    own = jnp.concatenate(owns, axis=1)
    blk = lax.broadcasted_iota(jnp.int32, gate.shape, 0)
    gate = jnp.where(blk < own, gate, -jnp.inf)
    sel = blk == own
    for _ in range(MOBA_TOPK):
        top = jnp.max(gate, axis=0, keepdims=True)
        first = jnp.min(jnp.where(gate == top, blk, nb), axis=0, keepdims=True)
        pick = (blk == first) & (top > -jnp.inf)
        sel = sel | pick
        gate = jnp.where(pick, -jnp.inf, gate)
    sel_f = jnp.where(sel, 1.0, 0.0)
    for c in range(len(chains)):
        sel_ref[c] = sel_f[:, c * MOBA_BLOCK:(c + 1) * MOBA_BLOCK]
    acc_ref[...] = jnp.zeros_like(acc_ref)

    def block_of(t):
        return jnp.maximum(j_last - t, 0)

    def score_stage(t, slot, near):
        j = block_of(t)
        kj = k_ref[0, 0, pl.ds(pl.multiple_of(j * MOBA_BLOCK, MOBA_BLOCK), MOBA_BLOCK), :]
        for c, (h, a) in enumerate(chains):
            s = _dot_nt(kj, qa_ref[c])
            if near:
                s = s + bias_ref[a, jnp.clip(t - (Q_BLOCKS - 1 - h), 0, NEAR_TILES)]
            s_ref[slot, c] = s
            mt_ref[slot, c] = jnp.max(s, axis=0, keepdims=True)

    def softmax_stage(t, slot, m_prev):
        j = block_of(t)
        sel_thr = jnp.where(t <= j_last, 0.5, 2.0)
        m_next, alphas, pvs = [], [], []
        for c, (h, a) in enumerate(chains):
            chosen = sel_ref[c, pl.ds(j, 1), :] > sel_thr
            m_new = jnp.where(chosen, jnp.maximum(m_prev[c], mt_ref[slot, c]), m_prev[c])
            alphas.append(jnp.exp2(m_prev[c] - m_new))
            p = jnp.exp2(s_ref[slot, c] - jnp.where(chosen, m_new, BIG)).astype(BF16)
            pvs.append(_dot(vt_ref[0, 0, j, a], p))
            m_next.append(m_new)
        for c in range(len(chains)):
            acc_ref[c] = alphas[c] * acc_ref[c] + pvs[c]
        return tuple(m_next)

    score_stage(0, 0, True)

    def pair_body(tt, m, *, near):
        t0 = 2 * tt
        score_stage(t0 + 1, 1, near)
        m = softmax_stage(t0, 0, m)
        score_stage(t0 + 2, 0, near)
        m = softmax_stage(t0 + 1, 1, m)
        return m

    near_pairs = (NEAR_TILES + Q_BLOCKS) // 2
    n_pairs = (j_last + 2) // 2
    m = (jnp.full((1, MOBA_BLOCK), -BIG, F32),) * len(chains)
    m = lax.fori_loop(0, jnp.minimum(near_pairs, n_pairs), functools.partial(pair_body, near=True), m)
    lax.fori_loop(near_pairs, n_pairs, functools.partial(pair_body, near=False), m)

    for h in range(Q_BLOCKS):
        outs = []
        for a in range(2):
            acc = acc_ref[2 * h + a]
            outs.append(acc[0:HEAD_DIM] / acc[HEAD_DIM:HEAD_DIM + 1])
        o_ref[0, 0, h * MOBA_BLOCK:(h + 1) * MOBA_BLOCK, :] = jnp.concatenate(outs, axis=0).T.astype(BF16)


def _attention(q, k, vt, bias_t):
    batch, _, seq, _ = q.shape
    nb = seq // MOBA_BLOCK
    tq = Q_BLOCKS * MOBA_BLOCK
    n_chains = 2 * Q_BLOCKS
    return pl.pallas_call(
        functools.partial(_attn_kernel, nb=nb),
        grid=(batch, HEAD_PAIRS, seq // tq),
        in_specs=[
            pl.BlockSpec((1, 1, tq, PAIR_W), lambda b, p, i: (b, p, i, 0)),
            pl.BlockSpec((1, 1, seq, PAIR_W), lambda b, p, i: (b, p, 0, 0)),
            pl.BlockSpec((1, 1, nb, 2, V_ROWS, MOBA_BLOCK), lambda b, p, i: (b, p, 0, 0, 0, 0)),
            pl.BlockSpec((2, NEAR_TILES + 1, MOBA_BLOCK, MOBA_BLOCK), lambda b, p, i: (p, 0, 0, 0)),
        ],
        out_specs=pl.BlockSpec((1, 1, tq, PAIR_W), lambda b, p, i: (b, p, i, 0)),
        out_shape=jax.ShapeDtypeStruct((batch, HEAD_PAIRS, seq, PAIR_W), BF16),
        scratch_shapes=[
            pltpu.VMEM((nb, PAIR_W), F32),
            pltpu.VMEM((4 * nb, PAIR_W), BF16),
            pltpu.VMEM((n_chains, MOBA_BLOCK, PAIR_W), BF16),
            pltpu.VMEM((n_chains, nb, MOBA_BLOCK), F32),
            pltpu.VMEM((n_chains, V_ROWS, MOBA_BLOCK), F32),
            pltpu.VMEM((2, n_chains, MOBA_BLOCK, MOBA_BLOCK), F32),
            pltpu.VMEM((2, n_chains, 1, MOBA_BLOCK), F32),
        ],
        compiler_params=pltpu.CompilerParams(dimension_semantics=("arbitrary", "arbitrary", "arbitrary"),
                                             vmem_limit_bytes=VMEM_LIMIT),
        name="moba_attention",
    )(q, k, vt, bias_t)


def _post_kernel(x_ref, u_ref, vn_ref, sga_ref, sgb_ref, yb_ref, p_ref,
                 ws_ref, bs_ref, wo_ref, gffn_ref, w1_ref, w2_ref, gple_ref, wpg_ref, wpp_ref, gfin_ref,
                 o_ref, merged_ref, *, tm, final_norm):
    row = lax.broadcasted_iota(jnp.int32, (SGU_CHUNK, SGU_CHUNK), 0)
    col = lax.broadcasted_iota(jnp.int32, (SGU_CHUNK, SGU_CHUNK), 1)
    for g in range(SGU_GROUPS):
        w = jnp.where(row >= col, ws_ref[g], 0.0).astype(BF16)
        b = bs_ref[g]
        cols = slice(g * LANES, (g + 1) * LANES)
        for t in range(tm // SGU_CHUNK):
            rows = slice(t * SGU_CHUNK, (t + 1) * SGU_CHUNK)
            mixed = _dot(w, vn_ref[rows, cols]) + b
            y_a = u_ref[rows, cols].astype(F32) * mixed
            y_b = yb_ref[0, g, rows, :].astype(F32)
            merged = sga_ref[rows, cols].astype(F32) * y_a + sgb_ref[rows, cols].astype(F32) * y_b
            merged_ref[rows, cols] = merged.astype(BF16)

    x1 = x_ref[...] + _dot(merged_ref[...], wo_ref[...])

    h = _rms(x1, gffn_ref[...]).astype(BF16)
    x2 = x1
    for c in range(D_FF // D_MODEL):
        cs = slice(c * D_MODEL, (c + 1) * D_MODEL)
        a = jnp.square(jnp.maximum(_dot(h, w1_ref[:, cs]), 0.0)).astype(BF16)
        x2 = x2 + _dot(a, w2_ref[cs, :])

    gate = jax.nn.sigmoid(_dot(_rms(x2, gple_ref[...]).astype(BF16), wpg_ref[...]))
    x3 = x2 + gate * _dot(p_ref[...].astype(BF16), wpp_ref[...])
    o_ref[...] = _rms(x3, gfin_ref[...]) if final_norm else x3


def _post(x2, u, vn, sga, sgb, yb, p2, ws, bs, wo, gffn, w1, w2, gple, wpg, wpp, gfin, *, seq, tm, final_norm):
    n = x2.shape[0]
    tiles_per_seq = seq // tm
    row = lambda r: (r, 0)
    resident = functools.partial(pl.BlockSpec, pipeline_mode=pl.Buffered(1))
    c2 = lambda r: (0, 0)
    c3 = lambda r: (0, 0, 0)
    tokb = pl.BlockSpec((tm, D_MODEL), row)
    return pl.pallas_call(
        functools.partial(_post_kernel, tm=tm, final_norm=final_norm),
        grid=(n // tm,),
        in_specs=[
            tokb, tokb, tokb, tokb, tokb,
            pl.BlockSpec((1, HEAD_PAIRS, tm, PAIR_W), lambda r: (r // tiles_per_seq, 0, r % tiles_per_seq, 0)),
            pl.BlockSpec((tm, PLE_DIM), row),
            resident((SGU_GROUPS, SGU_CHUNK, SGU_CHUNK), c3),
            resident((SGU_GROUPS, SGU_CHUNK, 1), c3),
            resident((D_MODEL, D_MODEL), c2),
            resident((1, D_MODEL), c2),
            resident((D_MODEL, D_FF), c2),
            resident((D_FF, D_MODEL), c2),
            resident((1, D_MODEL), c2),
            resident((D_MODEL, D_MODEL), c2),
            resident((PLE_DIM, D_MODEL), c2),
            resident((1, D_MODEL), c2),
        ],
        out_specs=pl.BlockSpec((tm, D_MODEL), row),
        out_shape=jax.ShapeDtypeStruct((n, D_MODEL), F32),
        scratch_shapes=[pltpu.VMEM((tm, D_MODEL), BF16)],
        compiler_params=pltpu.CompilerParams(dimension_semantics=("arbitrary",),
                                             vmem_limit_bytes=VMEM_LIMIT),
        name="post",
    )(x2, u, vn, sga, sgb, yb, p2, ws, bs, wo, gffn, w1, w2, gple, wpg, wpp, gfin)


def kernel(x, p, norm_mix_g, w_in, w_sgu_spatial, b_sgu_spatial, ln_v_g, ln_v_b, rel_bias, w_out, norm_ffn_g,
           w_ff1, w_ff2, norm_ple_g, w_ple_gate, w_ple_proj, norm_final_g):
    batch, seq, _ = x.shape
    depth = w_in.shape[0]
    n = batch * seq
    assert seq % (Q_BLOCKS * MOBA_BLOCK) == 0 and seq // MOBA_BLOCK >= MOBA_TOPK
    tm_in = 512
    tm_post = 256
    vec = lambda g: g.reshape(1, D_MODEL)

    bias_t = _bias_tiles(rel_bias)
    x2 = x.reshape(n, D_MODEL)
    for i in range(depth):
        wi = w_in[i].astype(BF16)
        wa = wi[:, :4 * D_MODEL]
        wvt = wi[:, 4 * D_MODEL:5 * D_MODEL].T
        wg = wi[:, 5 * D_MODEL:]
        u, vn, q, k, vt, sga, sgb = _in_proj(
            x2, vec(norm_mix_g[i]), wa, wvt, wg, vec(ln_v_g[i]), vec(ln_v_b[i]),
            batch=batch, seq=seq, tm=tm_in)
        yb = _attention(q, k, vt, bias_t)
        x2 = _post(
            x2, u, vn, sga, sgb, yb, p[i].reshape(n, PLE_DIM),
            w_sgu_spatial[i], b_sgu_spatial[i].reshape(SGU_GROUPS, SGU_CHUNK, 1),
            w_out[i].astype(BF16), vec(norm_ffn_g[i]), w_ff1[i].astype(BF16), w_ff2[i].astype(BF16),
            vec(norm_ple_g[i]), w_ple_gate[i].astype(BF16), w_ple_proj[i].astype(BF16),
            vec(norm_final_g), seq=seq, tm=tm_post, final_norm=(i == depth - 1))
    return x2.reshape(batch, seq, D_MODEL)
```

```python
import functools
import math

import jax
import jax.numpy as jnp
from jax import lax
from jax.experimental import pallas as pl
from jax.experimental.pallas import tpu as pltpu

D_MODEL = 1024
PLE_DIM = 256
SGU_CHUNK = 128
SGU_GROUPS = 8
ATT_HEADS = 16
HEAD_DIM = 64
MOBA_BLOCK = 256
MOBA_TOPK = 3
REL_BUCKETS = 32
REL_MAX_DIST = 1024
D_FF = 4 * D_MODEL
EPS = 1e-6

LANES = 128
BF16_ROWS = 16
HEAD_PAIRS = ATT_HEADS // 2
PAIR_W = 2 * HEAD_DIM
V_ROWS = HEAD_DIM + BF16_ROWS
Q_BLOCKS = 2
FAR_UNROLL = 2
BIG = 1e30
VMEM_LIMIT = 60 * 1024 * 1024
LOG2E = math.log2(math.e)

F32 = jnp.float32
BF16 = jnp.bfloat16


def _bucket_thresholds():
    max_exact = REL_BUCKETS // 2
    n_log = REL_BUCKETS - max_exact
    ratio = REL_MAX_DIST // max_exact
    out = []
    for k in range(1, n_log):
        d = max_exact
        target = (max_exact ** n_log) * (ratio ** k)
        while d ** n_log < target:
            d += 1
        out.append(d)
    return tuple(out)


BUCKET_THRESHOLDS = _bucket_thresholds()
NEAR_TILES = -(-(BUCKET_THRESHOLDS[-1] + MOBA_BLOCK - 1) // MOBA_BLOCK)
assert NEAR_TILES == 5


def _dot(a, b):
    return jnp.dot(a, b, preferred_element_type=F32)


def _dot_nt(a, b):
    return lax.dot_general(a, b, (((1,), (1,)), ((), ())), preferred_element_type=F32)


def _rms(x, g):
    return x * lax.rsqrt(jnp.mean(x * x, axis=-1, keepdims=True) + EPS) * g


def _in_proj_kernel(x_ref, g_ref, wa_ref, wvt_ref, wg_ref, lng_ref, lnb_ref,
                    u_ref, vn_ref, q_ref, k_ref, vt_ref, sga_ref, sgb_ref, *, tm):
    h = _rms(x_ref[...], g_ref[...]).astype(BF16)

    u_ref[...] = jax.nn.gelu(_dot(h, wa_ref[:, 0:D_MODEL])).astype(BF16)

    gv = jax.nn.gelu(_dot(h, wa_ref[:, D_MODEL:2 * D_MODEL]))
    mu = jnp.mean(gv, axis=-1, keepdims=True)
    gc = gv - mu
    vn = gc * lax.rsqrt(jnp.mean(gc * gc, axis=-1, keepdims=True) + EPS)
    vn_ref[...] = (vn * lng_ref[...] + lnb_ref[...]).astype(BF16)

    q = _dot(h, wa_ref[:, 2 * D_MODEL:3 * D_MODEL]) * (HEAD_DIM ** -0.5 * LOG2E)
    k = _dot(h, wa_ref[:, 3 * D_MODEL:4 * D_MODEL])
    for p in range(HEAD_PAIRS):
        q_ref[0, p] = q[:, p * PAIR_W:(p + 1) * PAIR_W].astype(BF16)
        k_ref[0, p] = k[:, p * PAIR_W:(p + 1) * PAIR_W].astype(BF16)

    vt = _dot_nt(wvt_ref[...], h).astype(BF16)
    ones = jnp.ones((BF16_ROWS, MOBA_BLOCK), BF16)
    for p in range(HEAD_PAIRS):
        for jb in range(tm // MOBA_BLOCK):
            for a in range(2):
                r0 = p * PAIR_W + a * HEAD_DIM
                vt_ref[0, p, jb, a, 0:HEAD_DIM, :] = vt[r0:r0 + HEAD_DIM, jb * MOBA_BLOCK:(jb + 1) * MOBA_BLOCK]
                vt_ref[0, p, jb, a, HEAD_DIM:V_ROWS, :] = ones

    sga_ref[...] = jax.nn.sigmoid(_dot(h, wg_ref[:, 0:D_MODEL])).astype(BF16)
    sgb_ref[...] = jax.nn.sigmoid(_dot(h, wg_ref[:, D_MODEL:2 * D_MODEL])).astype(BF16)


def _in_proj(x2, g, wa, wvt, wg, lng, lnb, *, batch, seq, tm):
    n = x2.shape[0]
    tiles_per_seq = seq // tm
    nb = seq // MOBA_BLOCK
    row = lambda r: (r, 0)
    const = lambda r: (0, 0)
    resident = functools.partial(pl.BlockSpec, pipeline_mode=pl.Buffered(1))
    tok = jax.ShapeDtypeStruct((n, D_MODEL), BF16)
    pair_major = jax.ShapeDtypeStruct((batch, HEAD_PAIRS, seq, PAIR_W), BF16)
    return pl.pallas_call(
        functools.partial(_in_proj_kernel, tm=tm),
        grid=(n // tm,),
        in_specs=[
            pl.BlockSpec((tm, D_MODEL), row),
            resident((1, D_MODEL), const),
            resident((D_MODEL, 4 * D_MODEL), const),
            resident((D_MODEL, D_MODEL), const),
            resident((D_MODEL, 2 * D_MODEL), const),
            resident((1, D_MODEL), const),
            resident((1, D_MODEL), const),
        ],
        out_specs=[
            pl.BlockSpec((tm, D_MODEL), row),
            pl.BlockSpec((tm, D_MODEL), row),
            pl.BlockSpec((1, HEAD_PAIRS, tm, PAIR_W), lambda r: (r // tiles_per_seq, 0, r % tiles_per_seq, 0)),
            pl.BlockSpec((1, HEAD_PAIRS, tm, PAIR_W), lambda r: (r // tiles_per_seq, 0, r % tiles_per_seq, 0)),
            pl.BlockSpec((1, HEAD_PAIRS, tm // MOBA_BLOCK, 2, V_ROWS, MOBA_BLOCK),
                         lambda r: (r // tiles_per_seq, 0, r % tiles_per_seq, 0, 0, 0)),
            pl.BlockSpec((tm, D_MODEL), row),
            pl.BlockSpec((tm, D_MODEL), row),
        ],
        out_shape=[tok, tok, pair_major, pair_major,
                   jax.ShapeDtypeStruct((batch, HEAD_PAIRS, nb, 2, V_ROWS, MOBA_BLOCK), BF16),
                   tok, tok],
        compiler_params=pltpu.CompilerParams(dimension_semantics=("arbitrary",),
                                             vmem_limit_bytes=VMEM_LIMIT),
        name="in_proj",
    )(x2, g, wa, wvt, wg, lng, lnb)


def _bias_tiles_kernel(tab_ref, o_ref):
    h = pl.program_id(0)
    key = lax.broadcasted_iota(jnp.int32, (MOBA_BLOCK, MOBA_BLOCK), 0)
    qry = lax.broadcasted_iota(jnp.int32, (MOBA_BLOCK, MOBA_BLOCK), 1)
    max_exact = REL_BUCKETS // 2
    last = tab_ref[REL_BUCKETS - 1, h]
    for t in range(NEAR_TILES):
        dist = t * MOBA_BLOCK + qry - key
        n = jnp.maximum(dist, 0)
        val = jnp.full((MOBA_BLOCK, MOBA_BLOCK), tab_ref[0, h], F32)
        for b in range(1, max_exact + 1):
            val = jnp.where(n >= b, tab_ref[b, h], val)
        for kk, thr in enumerate(BUCKET_THRESHOLDS):
            val = jnp.where(n >= thr, tab_ref[max_exact + 1 + kk, h], val)
        val = (val - last) * LOG2E
        if t == 0:
            val = jnp.where(dist >= 0, val, -BIG)
        o_ref[0, t] = val
    o_ref[0, NEAR_TILES] = jnp.zeros((MOBA_BLOCK, MOBA_BLOCK), F32)


def _bias_tiles(rel_bias):
    return pl.pallas_call(
        _bias_tiles_kernel,
        grid=(ATT_HEADS,),
        in_specs=[pl.BlockSpec(memory_space=pltpu.SMEM)],
        out_specs=pl.BlockSpec((1, NEAR_TILES + 1, MOBA_BLOCK, MOBA_BLOCK), lambda h: (h, 0, 0, 0)),
        out_shape=jax.ShapeDtypeStruct((ATT_HEADS, NEAR_TILES + 1, MOBA_BLOCK, MOBA_BLOCK), F32),
        compiler_params=pltpu.CompilerParams(dimension_semantics=("arbitrary",)),
        name="bias_tiles",
    )(rel_bias)


def _attn_kernel(q_ref, k_ref, vt_ref, bias_ref, o_ref,
                 km_ref, kms_ref, qa_ref, sel_ref, acc_ref, s_ref, mt_ref, *, nb):
    step = pl.program_id(2)
    j_last = step * Q_BLOCKS + (Q_BLOCKS - 1)
    chains = [(h, a) for h in range(Q_BLOCKS) for a in range(2)]
    lane = lax.broadcasted_iota(jnp.int32, (1, PAIR_W), 1)

    @pl.when(step == 0)
    def _():
        def body(j, c):
            kb = k_ref[0, 0, pl.ds(pl.multiple_of(j * MOBA_BLOCK, MOBA_BLOCK), MOBA_BLOCK), :]
            km_ref[pl.ds(j, 1), :] = jnp.sum(kb.astype(F32), axis=0, keepdims=True) * (1.0 / MOBA_BLOCK)
            return c
        lax.fori_loop(0, nb, body, 0)
        km = km_ref[...]
        km_hi = km.astype(BF16)
        km_lo = (km - km_hi.astype(F32)).astype(BF16)
        for a in range(2):
            in_head = (lane >= a * HEAD_DIM) & (lane < (a + 1) * HEAD_DIM)
            kms_ref[(2 * a) * nb:(2 * a + 1) * nb, :] = jnp.where(in_head, km_hi, jnp.zeros_like(km_hi))
            kms_ref[(2 * a + 1) * nb:(2 * a + 2) * nb, :] = jnp.where(in_head, km_lo, jnp.zeros_like(km_lo))

    gates, owns = [], []
    for h in range(Q_BLOCKS):
        q = q_ref[0, 0, h * MOBA_BLOCK:(h + 1) * MOBA_BLOCK, :]
        g = _dot_nt(kms_ref[...], q)
        for a in range(2):
            qa_ref[2 * h + a] = jnp.where((lane >= a * HEAD_DIM) & (lane < (a + 1) * HEAD_DIM), q, jnp.zeros_like(q))
            gates.append(g[(2 * a) * nb:(2 * a + 1) * nb] + g[(2 * a + 1) * nb:(2 * a + 2) * nb])
            owns.append(jnp.full((1, MOBA_BLOCK), step * Q_BLOCKS + h, jnp.int32))
    gate = jnp.concatenate(gates, axis=1)
    own = jnp.concatenate(owns, axis=1)
    blk = lax.broadcasted_iota(jnp.int32, gate.shape, 0)
    gate = jnp.where(blk < own, gate, -jnp.inf)
    sel = blk == own
    for _ in range(MOBA_TOPK):
        top = jnp.max(gate, axis=0, keepdims=True)
        first = jnp.min(jnp.where(gate == top, blk, nb), axis=0, keepdims=True)
        pick = (blk == first) & (top > -jnp.inf)
        sel = sel | pick
        gate = jnp.where(pick, -jnp.inf, gate)
    sel_f = jnp.where(sel, 1.0, 0.0)
    for c in range(len(chains)):
        sel_ref[c] = sel_f[:, c * MOBA_BLOCK:(c + 1) * MOBA_BLOCK]
    acc_ref[...] = jnp.zeros_like(acc_ref)

    def block_of(t):
        return jnp.maximum(j_last - t, 0)

    def score_stage(t, slot, near):
        j = block_of(t)
        kj = k_ref[0, 0, pl.ds(pl.multiple_of(j * MOBA_BLOCK, MOBA_BLOCK), MOBA_BLOCK), :]
        for c, (h, a) in enumerate(chains):
            s = _dot_nt(kj, qa_ref[c])
            if near:
                s = s + bias_ref[a, jnp.clip(t - (Q_BLOCKS - 1 - h), 0, NEAR_TILES)]
            s_ref[slot, c] = s
            mt_ref[slot, c] = jnp.max(s, axis=0, keepdims=True)

    def softmax_stage(t, slot, m_prev):
        j = block_of(t)
        sel_thr = jnp.where(t <= j_last, 0.5, 2.0)
        m_next, alphas, pvs = [], [], []
        for c, (h, a) in enumerate(chains):
            chosen = sel_ref[c, pl.ds(j, 1), :] > sel_thr
            m_new = jnp.where(chosen, jnp.maximum(m_prev[c], mt_ref[slot, c]), m_prev[c])
            alphas.append(jnp.exp2(m_prev[c] - m_new))
            p = jnp.exp2(s_ref[slot, c] - jnp.where(chosen, m_new, BIG)).astype(BF16)
            pvs.append(_dot(vt_ref[0, 0, j, a], p))
            m_next.append(m_new)
        for c in range(len(chains)):
            acc_ref[c] = alphas[c] * acc_ref[c] + pvs[c]
        return tuple(m_next)

    score_stage(0, 0, True)

    def pairs_body(it, m, *, near, first_pair, pairs_per_trip):
        t = 2 * (first_pair + it * pairs_per_trip)
        for _ in range(pairs_per_trip):
            score_stage(t + 1, 1, near)
            m = softmax_stage(t, 0, m)
            score_stage(t + 2, 0, near)
            m = softmax_stage(t + 1, 1, m)
            t = t + 2
        return m

    near_pairs = (NEAR_TILES + Q_BLOCKS) // 2
    n_pairs = (j_last + 2) // 2
    far_pairs = jnp.maximum(n_pairs - near_pairs, 0)
    m = (jnp.full((1, MOBA_BLOCK), -BIG, F32),) * len(chains)
    m = lax.fori_loop(0, jnp.where(n_pairs >= near_pairs, 1, 0),
                      functools.partial(pairs_body, near=True, first_pair=0, pairs_per_trip=near_pairs), m)
    m = lax.fori_loop(0, jnp.where(n_pairs >= near_pairs, 0, n_pairs),
                      functools.partial(pairs_body, near=True, first_pair=0, pairs_per_trip=1), m)
    m = lax.fori_loop(0, far_pairs // FAR_UNROLL,
                      functools.partial(pairs_body, near=False, first_pair=near_pairs, pairs_per_trip=FAR_UNROLL), m)
    lax.fori_loop(0, far_pairs % FAR_UNROLL,
                  functools.partial(pairs_body, near=False,
                                    first_pair=near_pairs + far_pairs // FAR_UNROLL * FAR_UNROLL, pairs_per_trip=1), m)

    for h in range(Q_BLOCKS):
        outs = []
        for a in range(2):
            acc = acc_ref[2 * h + a]
            outs.append(acc[0:HEAD_DIM] / acc[HEAD_DIM:HEAD_DIM + 1])
        o_ref[0, 0, h * MOBA_BLOCK:(h + 1) * MOBA_BLOCK, :] = jnp.concatenate(outs, axis=0).T.astype(BF16)


def _attention(q, k, vt, bias_t):
    batch, _, seq, _ = q.shape
    nb = seq // MOBA_BLOCK
    tq = Q_BLOCKS * MOBA_BLOCK
    n_chains = 2 * Q_BLOCKS
    return pl.pallas_call(
        functools.partial(_attn_kernel, nb=nb),
        grid=(batch, HEAD_PAIRS, seq // tq),
        in_specs=[
            pl.BlockSpec((1, 1, tq, PAIR_W), lambda b, p, i: (b, p, i, 0)),
            pl.BlockSpec((1, 1, seq, PAIR_W), lambda b, p, i: (b, p, 0, 0)),
            pl.BlockSpec((1, 1, nb, 2, V_ROWS, MOBA_BLOCK), lambda b, p, i: (b, p, 0, 0, 0, 0)),
            pl.BlockSpec((2, NEAR_TILES + 1, MOBA_BLOCK, MOBA_BLOCK), lambda b, p, i: (p, 0, 0, 0)),
        ],
        out_specs=pl.BlockSpec((1, 1, tq, PAIR_W), lambda b, p, i: (b, p, i, 0)),
        out_shape=jax.ShapeDtypeStruct((batch, HEAD_PAIRS, seq, PAIR_W), BF16),
        scratch_shapes=[
            pltpu.VMEM((nb, PAIR_W), F32),
            pltpu.VMEM((4 * nb, PAIR_W), BF16),
            pltpu.VMEM((n_chains, MOBA_BLOCK, PAIR_W), BF16),
            pltpu.VMEM((n_chains, nb, MOBA_BLOCK), F32),
            pltpu.VMEM((n_chains, V_ROWS, MOBA_BLOCK), F32),
            pltpu.VMEM((2, n_chains, MOBA_BLOCK, MOBA_BLOCK), F32),
            pltpu.VMEM((2, n_chains, 1, MOBA_BLOCK), F32),
        ],
        compiler_params=pltpu.CompilerParams(dimension_semantics=("arbitrary", "arbitrary", "arbitrary"),
                                             vmem_limit_bytes=VMEM_LIMIT),
        name="moba_attention",
    )(q, k, vt, bias_t)


def _post_kernel(x_ref, u_ref, vn_ref, sga_ref, sgb_ref, yb_ref, p_ref,
                 ws_ref, bs_ref, wo_ref, gffn_ref, w1_ref, w2_ref, gple_ref, wpg_ref, wpp_ref, gfin_ref,
                 o_ref, merged_ref, *, tm, final_norm):
    row = lax.broadcasted_iota(jnp.int32, (SGU_CHUNK, SGU_CHUNK), 0)
    col = lax.broadcasted_iota(jnp.int32, (SGU_CHUNK, SGU_CHUNK), 1)
    for g in range(SGU_GROUPS):
        w = jnp.where(row >= col, ws_ref[g], 0.0).astype(BF16)
        b = bs_ref[g]
        cols = slice(g * LANES, (g + 1) * LANES)
        for t in range(tm // SGU_CHUNK):
            rows = slice(t * SGU_CHUNK, (t + 1) * SGU_CHUNK)
            mixed = _dot(w, vn_ref[rows, cols]) + b
            y_a = u_ref[rows, cols].astype(F32) * mixed
            y_b = yb_ref[0, g, rows, :].astype(F32)
            merged = sga_ref[rows, cols].astype(F32) * y_a + sgb_ref[rows, cols].astype(F32) * y_b
            merged_ref[rows, cols] = merged.astype(BF16)

    x1 = x_ref[...] + _dot(merged_ref[...], wo_ref[...])

    h = _rms(x1, gffn_ref[...]).astype(BF16)
    x2 = x1
    for c in range(D_FF // D_MODEL):
        cs = slice(c * D_MODEL, (c + 1) * D_MODEL)
        a = jnp.square(jnp.maximum(_dot(h, w1_ref[:, cs]), 0.0)).astype(BF16)
        x2 = x2 + _dot(a, w2_ref[cs, :])

    gate = jax.nn.sigmoid(_dot(_rms(x2, gple_ref[...]).astype(BF16), wpg_ref[...]))
    x3 = x2 + gate * _dot(p_ref[...].astype(BF16), wpp_ref[...])
    o_ref[...] = _rms(x3, gfin_ref[...]) if final_norm else x3


def _post(x2, u, vn, sga, sgb, yb, p2, ws, bs, wo, gffn, w1, w2, gple, wpg, wpp, gfin, *, seq, tm, final_norm):
    n = x2.shape[0]
    tiles_per_seq = seq // tm
    row = lambda r: (r, 0)
    resident = functools.partial(pl.BlockSpec, pipeline_mode=pl.Buffered(1))
    c2 = lambda r: (0, 0)
    c3 = lambda r: (0, 0, 0)
    tokb = pl.BlockSpec((tm, D_MODEL), row)
    return pl.pallas_call(
        functools.partial(_post_kernel, tm=tm, final_norm=final_norm),
        grid=(n // tm,),
        in_specs=[
            tokb, tokb, tokb, tokb, tokb,
            pl.BlockSpec((1, HEAD_PAIRS, tm, PAIR_W), lambda r: (r // tiles_per_seq, 0, r % tiles_per_seq, 0)),
            pl.BlockSpec((tm, PLE_DIM), row),
            resident((SGU_GROUPS, SGU_CHUNK, SGU_CHUNK), c3),
            resident((SGU_GROUPS, SGU_CHUNK, 1), c3),
            resident((D_MODEL, D_MODEL), c2),
            resident((1, D_MODEL), c2),
            resident((D_MODEL, D_FF), c2),
            resident((D_FF, D_MODEL), c2),
            resident((1, D_MODEL), c2),
            resident((D_MODEL, D_MODEL), c2),
            resident((PLE_DIM, D_MODEL), c2),
            resident((1, D_MODEL), c2),
        ],
        out_specs=pl.BlockSpec((tm, D_MODEL), row),
        out_shape=jax.ShapeDtypeStruct((n, D_MODEL), F32),
        scratch_shapes=[pltpu.VMEM((tm, D_MODEL), BF16)],
        compiler_params=pltpu.CompilerParams(dimension_semantics=("arbitrary",),
                                             vmem_limit_bytes=VMEM_LIMIT),
        name="post",
    )(x2, u, vn, sga, sgb, yb, p2, ws, bs, wo, gffn, w1, w2, gple, wpg, wpp, gfin)


def kernel(x, p, norm_mix_g, w_in, w_sgu_spatial, b_sgu_spatial, ln_v_g, ln_v_b, rel_bias, w_out, norm_ffn_g,
           w_ff1, w_ff2, norm_ple_g, w_ple_gate, w_ple_proj, norm_final_g):
    batch, seq, _ = x.shape
    depth = w_in.shape[0]
    n = batch * seq
    assert seq % (Q_BLOCKS * MOBA_BLOCK) == 0 and seq // MOBA_BLOCK >= MOBA_TOPK
    tm_in = 512
    tm_post = 256
    vec = lambda g: g.reshape(1, D_MODEL)

    bias_t = _bias_tiles(rel_bias)
    x2 = x.reshape(n, D_MODEL)
    for i in range(depth):
        wi = w_in[i].astype(BF16)
        wa = wi[:, :4 * D_MODEL]
        wvt = wi[:, 4 * D_MODEL:5 * D_MODEL].T
        wg = wi[:, 5 * D_MODEL:]
        u, vn, q, k, vt, sga, sgb = _in_proj(
            x2, vec(norm_mix_g[i]), wa, wvt, wg, vec(ln_v_g[i]), vec(ln_v_b[i]),
            batch=batch, seq=seq, tm=tm_in)
        yb = _attention(q, k, vt, bias_t)
        x2 = _post(
            x2, u, vn, sga, sgb, yb, p[i].reshape(n, PLE_DIM),
            w_sgu_spatial[i], b_sgu_spatial[i].reshape(SGU_GROUPS, SGU_CHUNK, 1),
            w_out[i].astype(BF16), vec(norm_ffn_g[i]), w_ff1[i].astype(BF16), w_ff2[i].astype(BF16),
            vec(norm_ple_g[i]), w_ple_gate[i].astype(BF16), w_ple_proj[i].astype(BF16),
            vec(norm_final_g), seq=seq, tm=tm_post, final_norm=(i == depth - 1))
    return x2.reshape(batch, seq, D_MODEL)
```

```python
import functools
import math

import jax
import jax.numpy as jnp
from jax import lax
from jax.experimental import pallas as pl
from jax.experimental.pallas import tpu as pltpu

D_MODEL = 1024
PLE_DIM = 256
SGU_CHUNK = 128
SGU_GROUPS = 8
ATT_HEADS = 16
HEAD_DIM = 64
MOBA_BLOCK = 256
MOBA_TOPK = 3
REL_BUCKETS = 32
REL_MAX_DIST = 1024
D_FF = 4 * D_MODEL
EPS = 1e-6

LANES = 128
BF16_ROWS = 16
HEAD_PAIRS = ATT_HEADS // 2
PAIR_W = 2 * HEAD_DIM
V_ROWS = HEAD_DIM + BF16_ROWS
Q_BLOCKS = 2
FAR_UNROLL = 2
LAZY_MAX_EXCESS = 16.0
BIG = 1e30
VMEM_LIMIT = 60 * 1024 * 1024
LOG2E = math.log2(math.e)

F32 = jnp.float32
BF16 = jnp.bfloat16


def _bucket_thresholds():
    max_exact = REL_BUCKETS // 2
    n_log = REL_BUCKETS - max_exact
    ratio = REL_MAX_DIST // max_exact
    out = []
    for k in range(1, n_log):
        d = max_exact
        target = (max_exact ** n_log) * (ratio ** k)
        while d ** n_log < target:
            d += 1
        out.append(d)
    return tuple(out)


BUCKET_THRESHOLDS = _bucket_thresholds()
NEAR_TILES = -(-(BUCKET_THRESHOLDS[-1] + MOBA_BLOCK - 1) // MOBA_BLOCK)
assert NEAR_TILES == 5


def _dot(a, b):
    return jnp.dot(a, b, preferred_element_type=F32)


def _dot_nt(a, b):
    return lax.dot_general(a, b, (((1,), (1,)), ((), ())), preferred_element_type=F32)


def _rms(x, g):
    return x * lax.rsqrt(jnp.mean(x * x, axis=-1, keepdims=True) + EPS) * g


def _in_proj_kernel(x_ref, g_ref, wa_ref, wvt_ref, wg_ref, lng_ref, lnb_ref,
                    u_ref, vn_ref, q_ref, k_ref, vt_ref, sga_ref, sgb_ref, *, tm):
    h = _rms(x_ref[...], g_ref[...]).astype(BF16)

    u_ref[...] = jax.nn.gelu(_dot(h, wa_ref[:, 0:D_MODEL])).astype(BF16)

    gv = jax.nn.gelu(_dot(h, wa_ref[:, D_MODEL:2 * D_MODEL]))
    mu = jnp.mean(gv, axis=-1, keepdims=True)
    gc = gv - mu
    vn = gc * lax.rsqrt(jnp.mean(gc * gc, axis=-1, keepdims=True) + EPS)
    vn_ref[...] = (vn * lng_ref[...] + lnb_ref[...]).astype(BF16)

    q = _dot(h, wa_ref[:, 2 * D_MODEL:3 * D_MODEL]) * (HEAD_DIM ** -0.5 * LOG2E)
    k = _dot(h, wa_ref[:, 3 * D_MODEL:4 * D_MODEL])
    for p in range(HEAD_PAIRS):
        q_ref[0, p] = q[:, p * PAIR_W:(p + 1) * PAIR_W].astype(BF16)
        k_ref[0, p] = k[:, p * PAIR_W:(p + 1) * PAIR_W].astype(BF16)

    vt = _dot_nt(wvt_ref[...], h).astype(BF16)
    ones = jnp.ones((BF16_ROWS, MOBA_BLOCK), BF16)
    for p in range(HEAD_PAIRS):
        for jb in range(tm // MOBA_BLOCK):
            for a in range(2):
                r0 = p * PAIR_W + a * HEAD_DIM
                vt_ref[0, p, jb, a, 0:HEAD_DIM, :] = vt[r0:r0 + HEAD_DIM, jb * MOBA_BLOCK:(jb + 1) * MOBA_BLOCK]
                vt_ref[0, p, jb, a, HEAD_DIM:V_ROWS, :] = ones

    sga_ref[...] = jax.nn.sigmoid(_dot(h, wg_ref[:, 0:D_MODEL])).astype(BF16)
    sgb_ref[...] = jax.nn.sigmoid(_dot(h, wg_ref[:, D_MODEL:2 * D_MODEL])).astype(BF16)


def _in_proj(x2, g, wa, wvt, wg, lng, lnb, *, batch, seq, tm):
    n = x2.shape[0]
    tiles_per_seq = seq // tm
    nb = seq // MOBA_BLOCK
    row = lambda r: (r, 0)
    const = lambda r: (0, 0)
    resident = functools.partial(pl.BlockSpec, pipeline_mode=pl.Buffered(1))
    tok = jax.ShapeDtypeStruct((n, D_MODEL), BF16)
    pair_major = jax.ShapeDtypeStruct((batch, HEAD_PAIRS, seq, PAIR_W), BF16)
    return pl.pallas_call(
        functools.partial(_in_proj_kernel, tm=tm),
        grid=(n // tm,),
        in_specs=[
            pl.BlockSpec((tm, D_MODEL), row),
            resident((1, D_MODEL), const),
            resident((D_MODEL, 4 * D_MODEL), const),
            resident((D_MODEL, D_MODEL), const),
            resident((D_MODEL, 2 * D_MODEL), const),
            resident((1, D_MODEL), const),
            resident((1, D_MODEL), const),
        ],
        out_specs=[
            pl.BlockSpec((tm, D_MODEL), row),
            pl.BlockSpec((tm, D_MODEL), row),
            pl.BlockSpec((1, HEAD_PAIRS, tm, PAIR_W), lambda r: (r // tiles_per_seq, 0, r % tiles_per_seq, 0)),
            pl.BlockSpec((1, HEAD_PAIRS, tm, PAIR_W), lambda r: (r // tiles_per_seq, 0, r % tiles_per_seq, 0)),
            pl.BlockSpec((1, HEAD_PAIRS, tm // MOBA_BLOCK, 2, V_ROWS, MOBA_BLOCK),
                         lambda r: (r // tiles_per_seq, 0, r % tiles_per_seq, 0, 0, 0)),
            pl.BlockSpec((tm, D_MODEL), row),
            pl.BlockSpec((tm, D_MODEL), row),
        ],
        out_shape=[tok, tok, pair_major, pair_major,
                   jax.ShapeDtypeStruct((batch, HEAD_PAIRS, nb, 2, V_ROWS, MOBA_BLOCK), BF16),
                   tok, tok],
        compiler_params=pltpu.CompilerParams(dimension_semantics=("arbitrary",),
                                             vmem_limit_bytes=VMEM_LIMIT),
        name="in_proj",
    )(x2, g, wa, wvt, wg, lng, lnb)


def _bias_tiles_kernel(tab_ref, o_ref):
    h = pl.program_id(0)
    key = lax.broadcasted_iota(jnp.int32, (MOBA_BLOCK, MOBA_BLOCK), 0)
    qry = lax.broadcasted_iota(jnp.int32, (MOBA_BLOCK, MOBA_BLOCK), 1)
    max_exact = REL_BUCKETS // 2
    last = tab_ref[REL_BUCKETS - 1, h]
    for t in range(NEAR_TILES):
        dist = t * MOBA_BLOCK + qry - key
        n = jnp.maximum(dist, 0)
        val = jnp.full((MOBA_BLOCK, MOBA_BLOCK), tab_ref[0, h], F32)
        for b in range(1, max_exact + 1):
            val = jnp.where(n >= b, tab_ref[b, h], val)
        for kk, thr in enumerate(BUCKET_THRESHOLDS):
            val = jnp.where(n >= thr, tab_ref[max_exact + 1 + kk, h], val)
        val = (val - last) * LOG2E
        if t == 0:
            val = jnp.where(dist >= 0, val, -BIG)
        o_ref[0, t] = val
    o_ref[0, NEAR_TILES] = jnp.zeros((MOBA_BLOCK, MOBA_BLOCK), F32)


def _bias_tiles(rel_bias):
    return pl.pallas_call(
        _bias_tiles_kernel,
        grid=(ATT_HEADS,),
        in_specs=[pl.BlockSpec(memory_space=pltpu.SMEM)],
        out_specs=pl.BlockSpec((1, NEAR_TILES + 1, MOBA_BLOCK, MOBA_BLOCK), lambda h: (h, 0, 0, 0)),
        out_shape=jax.ShapeDtypeStruct((ATT_HEADS, NEAR_TILES + 1, MOBA_BLOCK, MOBA_BLOCK), F32),
        compiler_params=pltpu.CompilerParams(dimension_semantics=("arbitrary",)),
        name="bias_tiles",
    )(rel_bias)


def _attn_kernel(q_ref, k_ref, vt_ref, bias_ref, o_ref,
                 km_ref, kms_ref, qa_ref, sel_ref, acc_ref, s_ref, mt_ref, p_ref, alpha_ref, *, nb):
    step = pl.program_id(2)
    j_last = step * Q_BLOCKS + (Q_BLOCKS - 1)
    chains = [(h, a) for h in range(Q_BLOCKS) for a in range(2)]
    lane = lax.broadcasted_iota(jnp.int32, (1, PAIR_W), 1)

    @pl.when(step == 0)
    def _():
        def body(j, c):
            kb = k_ref[0, 0, pl.ds(pl.multiple_of(j * MOBA_BLOCK, MOBA_BLOCK), MOBA_BLOCK), :]
            km_ref[pl.ds(j, 1), :] = jnp.sum(kb.astype(F32), axis=0, keepdims=True) * (1.0 / MOBA_BLOCK)
            return c
        lax.fori_loop(0, nb, body, 0)
        km = km_ref[...]
        km_hi = km.astype(BF16)
        km_lo = (km - km_hi.astype(F32)).astype(BF16)
        for a in range(2):
            in_head = (lane >= a * HEAD_DIM) & (lane < (a + 1) * HEAD_DIM)
            kms_ref[(2 * a) * nb:(2 * a + 1) * nb, :] = jnp.where(in_head, km_hi, jnp.zeros_like(km_hi))
            kms_ref[(2 * a + 1) * nb:(2 * a + 2) * nb, :] = jnp.where(in_head, km_lo, jnp.zeros_like(km_lo))

    gates, owns = [], []
    for h in range(Q_BLOCKS):
        q = q_ref[0, 0, h * MOBA_BLOCK:(h + 1) * MOBA_BLOCK, :]
        g = _dot_nt(kms_ref[...], q)
        for a in range(2):
            qa_ref[2 * h + a] = jnp.where((lane >= a * HEAD_DIM) & (lane < (a + 1) * HEAD_DIM), q, jnp.zeros_like(q))
            gates.append(g[(2 * a) * nb:(2 * a + 1) * nb] + g[(2 * a + 1) * nb:(2 * a + 2) * nb])
            owns.append(jnp.full((1, MOBA_BLOCK), step * Q_BLOCKS + h, jnp.int32))
    gate = jnp.concatenate(gates, axis=1)
    own = jnp.concatenate(owns, axis=1)
    blk = lax.broadcasted_iota(jnp.int32, gate.shape, 0)
    gate = jnp.where(blk < own, gate, -jnp.inf)
    sel = blk == own
    for _ in range(MOBA_TOPK):
        top = jnp.max(gate, axis=0, keepdims=True)
        first = jnp.min(jnp.where(gate == top, blk, nb), axis=0, keepdims=True)
        pick = (blk == first) & (top > -jnp.inf)
        sel = sel | pick
        gate = jnp.where(pick, -jnp.inf, gate)
    sel_f = jnp.where(sel, 1.0, 0.0)
    for c in range(len(chains)):
        sel_ref[c] = sel_f[:, c * MOBA_BLOCK:(c + 1) * MOBA_BLOCK]
    def block_of(t):
        return jnp.maximum(j_last - t, 0)

    def key_block(j):
        return k_ref[0, 0, pl.ds(pl.multiple_of(j * MOBA_BLOCK, MOBA_BLOCK), MOBA_BLOCK), :]

    def scores(t, c, kj, near):
        h, a = chains[c]
        s = _dot_nt(kj, qa_ref[c])
        if near:
            s = s + bias_ref[a, jnp.clip(t - (Q_BLOCKS - 1 - h), 0, NEAR_TILES)]
        return s

    def chosen_row(t, c, j):
        return sel_ref[c, pl.ds(j, 1), :] > jnp.where(t <= j_last, 0.5, 2.0)

    def score_stage(t, slot, near):
        kj = key_block(block_of(t))
        for c in range(len(chains)):
            s = scores(t, c, kj, near)
            s_ref[slot, c] = s
            mt_ref[slot, c] = jnp.max(s, axis=0, keepdims=True)

    def softmax_stage(t, slot, m_prev):
        j = block_of(t)
        m_next, alphas, pvs = [], [], []
        for c, (h, a) in enumerate(chains):
            chosen = chosen_row(t, c, j)
            m_new = jnp.where(chosen, jnp.maximum(m_prev[c], mt_ref[slot, c]), m_prev[c])
            alphas.append(jnp.exp2(m_prev[c] - m_new))
            p = jnp.exp2(s_ref[slot, c] - jnp.where(chosen, m_new, BIG)).astype(BF16)
            pvs.append(_dot(vt_ref[0, 0, j, a], p))
            m_next.append(m_new)
        for c in range(len(chains)):
            acc_ref[c] = alphas[c] * acc_ref[c] + pvs[c]
        return tuple(m_next)

    def exact_pair(tt, m):
        t = 2 * tt
        score_stage(t + 1, 1, True)
        m = softmax_stage(t, 0, m)
        score_stage(t + 2, 0, True)
        m = softmax_stage(t + 1, 1, m)
        return m

    def lazy_score_stage(t, slot, carry, near):
        r, excess = carry
        j = block_of(t)
        kj = key_block(j)
        r_next, excess_next = [], []
        for c in range(len(chains)):
            s = scores(t, c, kj, near)
            chosen = chosen_row(t, c, j)
            tile_max = jnp.max(s, axis=0, keepdims=True)
            p_ref[slot, c] = jnp.exp2(s - jnp.where(chosen, r[c], BIG)).astype(BF16)
            r_new = jnp.where(chosen, jnp.maximum(r[c], tile_max), r[c])
            alpha_ref[slot, c] = jnp.exp2(r[c] - r_new)
            excess_next.append(jnp.maximum(excess[c], jnp.where(chosen, tile_max - r[c], -BIG)))
            r_next.append(r_new)
        return tuple(r_next), tuple(excess_next)

    def lazy_pv_stage(t, slot):
        j = block_of(t)
        pvs = [_dot(vt_ref[0, 0, j, a], p_ref[slot, c]) for c, (h, a) in enumerate(chains)]
        for c in range(len(chains)):
            acc_ref[c] = (acc_ref[c] + pvs[c]) * alpha_ref[slot, c]

    def lazy_pairs(it, carry, *, near, first_pair, pairs_per_trip):
        t = 2 * (first_pair + it * pairs_per_trip)
        for _ in range(pairs_per_trip):
            carry = lazy_score_stage(t + 1, 1, carry, near)
            lazy_pv_stage(t, 0)
            carry = lazy_score_stage(t + 2, 0, carry, near)
            lazy_pv_stage(t + 1, 1)
            t = t + 2
        return carry

    def finalize():
        for h in range(Q_BLOCKS):
            outs = []
            for a in range(2):
                acc = acc_ref[2 * h + a]
                outs.append(acc[0:HEAD_DIM] / acc[HEAD_DIM:HEAD_DIM + 1])
            o_ref[0, 0, h * MOBA_BLOCK:(h + 1) * MOBA_BLOCK, :] = jnp.concatenate(outs, axis=0).T.astype(BF16)

    near_pairs = (NEAR_TILES + Q_BLOCKS) // 2
    n_pairs = (j_last + 2) // 2
    far_pairs = jnp.maximum(n_pairs - near_pairs, 0)
    lowest = (jnp.full((1, MOBA_BLOCK), -BIG, F32),) * len(chains)

    acc_ref[...] = jnp.zeros_like(acc_ref)
    score_stage(0, 0, True)
    score_stage(1, 1, True)
    carry = (softmax_stage(1, 1, softmax_stage(0, 0, lowest)), lowest)
    carry = lazy_score_stage(2, 0, carry, True)
    carry = lax.fori_loop(0, jnp.where(n_pairs >= near_pairs, 1, 0),
                          functools.partial(lazy_pairs, near=True, first_pair=1, pairs_per_trip=near_pairs - 1), carry)
    carry = lax.fori_loop(0, jnp.where(n_pairs >= near_pairs, 0, n_pairs - 1),
                          functools.partial(lazy_pairs, near=True, first_pair=1, pairs_per_trip=1), carry)
    carry = lax.fori_loop(0, far_pairs // FAR_UNROLL,
                          functools.partial(lazy_pairs, near=False, first_pair=near_pairs,
                                            pairs_per_trip=FAR_UNROLL), carry)
    carry = lax.fori_loop(0, far_pairs % FAR_UNROLL,
                          functools.partial(lazy_pairs, near=False, pairs_per_trip=1,
                                            first_pair=near_pairs + far_pairs // FAR_UNROLL * FAR_UNROLL), carry)
    finalize()

    worst = functools.reduce(jnp.maximum, carry[1])
    @pl.when(jnp.max(worst) > LAZY_MAX_EXCESS)
    def _():
        acc_ref[...] = jnp.zeros_like(acc_ref)
        score_stage(0, 0, True)
        lax.fori_loop(0, n_pairs, exact_pair, lowest)
        finalize()


def _attention(q, k, vt, bias_t):
    batch, _, seq, _ = q.shape
    nb = seq // MOBA_BLOCK
    tq = Q_BLOCKS * MOBA_BLOCK
    n_chains = 2 * Q_BLOCKS
    return pl.pallas_call(
        functools.partial(_attn_kernel, nb=nb),
        grid=(batch, HEAD_PAIRS, seq // tq),
        in_specs=[
            pl.BlockSpec((1, 1, tq, PAIR_W), lambda b, p, i: (b, p, i, 0)),
            pl.BlockSpec((1, 1, seq, PAIR_W), lambda b, p, i: (b, p, 0, 0)),
            pl.BlockSpec((1, 1, nb, 2, V_ROWS, MOBA_BLOCK), lambda b, p, i: (b, p, 0, 0, 0, 0)),
            pl.BlockSpec((2, NEAR_TILES + 1, MOBA_BLOCK, MOBA_BLOCK), lambda b, p, i: (p, 0, 0, 0)),
        ],
        out_specs=pl.BlockSpec((1, 1, tq, PAIR_W), lambda b, p, i: (b, p, i, 0)),
        out_shape=jax.ShapeDtypeStruct((batch, HEAD_PAIRS, seq, PAIR_W), BF16),
        scratch_shapes=[
            pltpu.VMEM((nb, PAIR_W), F32),
            pltpu.VMEM((4 * nb, PAIR_W), BF16),
            pltpu.VMEM((n_chains, MOBA_BLOCK, PAIR_W), BF16),
            pltpu.VMEM((n_chains, nb, MOBA_BLOCK), F32),
            pltpu.VMEM((n_chains, V_ROWS, MOBA_BLOCK), F32),
            pltpu.VMEM((2, n_chains, MOBA_BLOCK, MOBA_BLOCK), F32),
            pltpu.VMEM((2, n_chains, 1, MOBA_BLOCK), F32),
            pltpu.VMEM((2, n_chains, MOBA_BLOCK, MOBA_BLOCK), BF16),
            pltpu.VMEM((2, n_chains, 1, MOBA_BLOCK), F32),
        ],
        compiler_params=pltpu.CompilerParams(dimension_semantics=("arbitrary", "arbitrary", "arbitrary"),
                                             vmem_limit_bytes=VMEM_LIMIT),
        name="moba_attention",
    )(q, k, vt, bias_t)


def _post_kernel(x_ref, u_ref, vn_ref, sga_ref, sgb_ref, yb_ref, p_ref,
                 ws_ref, bs_ref, wo_ref, gffn_ref, w1_ref, w2_ref, gple_ref, wpg_ref, wpp_ref, gfin_ref,
                 o_ref, merged_ref, *, tm, final_norm):
    row = lax.broadcasted_iota(jnp.int32, (SGU_CHUNK, SGU_CHUNK), 0)
    col = lax.broadcasted_iota(jnp.int32, (SGU_CHUNK, SGU_CHUNK), 1)
    for g in range(SGU_GROUPS):
        w = jnp.where(row >= col, ws_ref[g], 0.0).astype(BF16)
        b = bs_ref[g]
        cols = slice(g * LANES, (g + 1) * LANES)
        for t in range(tm // SGU_CHUNK):
            rows = slice(t * SGU_CHUNK, (t + 1) * SGU_CHUNK)
            mixed = _dot(w, vn_ref[rows, cols]) + b
            y_a = u_ref[rows, cols].astype(F32) * mixed
            y_b = yb_ref[0, g, rows, :].astype(F32)
            merged = sga_ref[rows, cols].astype(F32) * y_a + sgb_ref[rows, cols].astype(F32) * y_b
            merged_ref[rows, cols] = merged.astype(BF16)

    x1 = x_ref[...] + _dot(merged_ref[...], wo_ref[...])

    h = _rms(x1, gffn_ref[...]).astype(BF16)
    x2 = x1
    for c in range(D_FF // D_MODEL):
        cs = slice(c * D_MODEL, (c + 1) * D_MODEL)
        a = jnp.square(jnp.maximum(_dot(h, w1_ref[:, cs]), 0.0)).astype(BF16)
        x2 = x2 + _dot(a, w2_ref[cs, :])

    gate = jax.nn.sigmoid(_dot(_rms(x2, gple_ref[...]).astype(BF16), wpg_ref[...]))
    x3 = x2 + gate * _dot(p_ref[...].astype(BF16), wpp_ref[...])
    o_ref[...] = _rms(x3, gfin_ref[...]) if final_norm else x3


def _post(x2, u, vn, sga, sgb, yb, p2, ws, bs, wo, gffn, w1, w2, gple, wpg, wpp, gfin, *, seq, tm, final_norm):
    n = x2.shape[0]
    tiles_per_seq = seq // tm
    row = lambda r: (r, 0)
    resident = functools.partial(pl.BlockSpec, pipeline_mode=pl.Buffered(1))
    c2 = lambda r: (0, 0)
    c3 = lambda r: (0, 0, 0)
    tokb = pl.BlockSpec((tm, D_MODEL), row)
    return pl.pallas_call(
        functools.partial(_post_kernel, tm=tm, final_norm=final_norm),
        grid=(n // tm,),
        in_specs=[
            tokb, tokb, tokb, tokb, tokb,
            pl.BlockSpec((1, HEAD_PAIRS, tm, PAIR_W), lambda r: (r // tiles_per_seq, 0, r % tiles_per_seq, 0)),
            pl.BlockSpec((tm, PLE_DIM), row),
            resident((SGU_GROUPS, SGU_CHUNK, SGU_CHUNK), c3),
            resident((SGU_GROUPS, SGU_CHUNK, 1), c3),
            resident((D_MODEL, D_MODEL), c2),
            resident((1, D_MODEL), c2),
            resident((D_MODEL, D_FF), c2),
            resident((D_FF, D_MODEL), c2),
            resident((1, D_MODEL), c2),
            resident((D_MODEL, D_MODEL), c2),
            resident((PLE_DIM, D_MODEL), c2),
            resident((1, D_MODEL), c2),
        ],
        out_specs=pl.BlockSpec((tm, D_MODEL), row),
        out_shape=jax.ShapeDtypeStruct((n, D_MODEL), F32),
        scratch_shapes=[pltpu.VMEM((tm, D_MODEL), BF16)],
        compiler_params=pltpu.CompilerParams(dimension_semantics=("arbitrary",),
                                             vmem_limit_bytes=VMEM_LIMIT),
        name="post",
    )(x2, u, vn, sga, sgb, yb, p2, ws, bs, wo, gffn, w1, w2, gple, wpg, wpp, gfin)


def kernel(x, p, norm_mix_g, w_in, w_sgu_spatial, b_sgu_spatial, ln_v_g, ln_v_b, rel_bias, w_out, norm_ffn_g,
           w_ff1, w_ff2, norm_ple_g, w_ple_gate, w_ple_proj, norm_final_g):
    batch, seq, _ = x.shape
    depth = w_in.shape[0]
    n = batch * seq
    assert seq % (Q_BLOCKS * MOBA_BLOCK) == 0 and seq // MOBA_BLOCK >= MOBA_TOPK
    tm_in = 512
    tm_post = 256
    vec = lambda g: g.reshape(1, D_MODEL)

    bias_t = _bias_tiles(rel_bias)
    x2 = x.reshape(n, D_MODEL)
    for i in range(depth):
        wi = w_in[i].astype(BF16)
        wa = wi[:, :4 * D_MODEL]
        wvt = wi[:, 4 * D_MODEL:5 * D_MODEL].T
        wg = wi[:, 5 * D_MODEL:]
        u, vn, q, k, vt, sga, sgb = _in_proj(
            x2, vec(norm_mix_g[i]), wa, wvt, wg, vec(ln_v_g[i]), vec(ln_v_b[i]),
            batch=batch, seq=seq, tm=tm_in)
        yb = _attention(q, k, vt, bias_t)
        x2 = _post(
            x2, u, vn, sga, sgb, yb, p[i].reshape(n, PLE_DIM),
            w_sgu_spatial[i], b_sgu_spatial[i].reshape(SGU_GROUPS, SGU_CHUNK, 1),
            w_out[i].astype(BF16), vec(norm_ffn_g[i]), w_ff1[i].astype(BF16), w_ff2[i].astype(BF16),
            vec(norm_ple_g[i]), w_ple_gate[i].astype(BF16), w_ple_proj[i].astype(BF16),
            vec(norm_final_g), seq=seq, tm=tm_post, final_norm=(i == depth - 1))
    return x2.reshape(batch, seq, D_MODEL)
```

```python
import functools
import math

import jax
import jax.numpy as jnp
from jax import lax
from jax.experimental import pallas as pl
from jax.experimental.pallas import tpu as pltpu

D_MODEL = 1024
PLE_DIM = 256
SGU_CHUNK = 128
SGU_GROUPS = 8
ATT_HEADS = 16
HEAD_DIM = 64
MOBA_BLOCK = 256
MOBA_TOPK = 3
REL_BUCKETS = 32
REL_MAX_DIST = 1024
D_FF = 4 * D_MODEL
EPS = 1e-6

LANES = 128
BF16_ROWS = 16
HEAD_PAIRS = ATT_HEADS // 2
PAIR_W = 2 * HEAD_DIM
V_ROWS = HEAD_DIM + BF16_ROWS
Q_BLOCKS = 2
FAR_TRIP_PAIRS = (4, 2, 1)
LAZY_MAX_EXCESS = 16.0
BIG = 1e30
VMEM_LIMIT = 60 * 1024 * 1024
LOG2E = math.log2(math.e)

F32 = jnp.float32
BF16 = jnp.bfloat16


def _bucket_thresholds():
    max_exact = REL_BUCKETS // 2
    n_log = REL_BUCKETS - max_exact
    ratio = REL_MAX_DIST // max_exact
    out = []
    for k in range(1, n_log):
        d = max_exact
        target = (max_exact ** n_log) * (ratio ** k)
        while d ** n_log < target:
            d += 1
        out.append(d)
    return tuple(out)


BUCKET_THRESHOLDS = _bucket_thresholds()
NEAR_TILES = -(-(BUCKET_THRESHOLDS[-1] + MOBA_BLOCK - 1) // MOBA_BLOCK)
assert NEAR_TILES == 5


def _dot(a, b):
    return jnp.dot(a, b, preferred_element_type=F32)


def _dot_nt(a, b):
    return lax.dot_general(a, b, (((1,), (1,)), ((), ())), preferred_element_type=F32)


def _rms(x, g):
    return x * lax.rsqrt(jnp.mean(x * x, axis=-1, keepdims=True) + EPS) * g


def _in_proj_kernel(x_ref, g_ref, wa_ref, wvt_ref, wg_ref, lng_ref, lnb_ref,
                    u_ref, vn_ref, q_ref, k_ref, vt_ref, sga_ref, sgb_ref, *, tm):
    h = _rms(x_ref[...], g_ref[...]).astype(BF16)

    u_ref[...] = jax.nn.gelu(_dot(h, wa_ref[:, 0:D_MODEL])).astype(BF16)

    gv = jax.nn.gelu(_dot(h, wa_ref[:, D_MODEL:2 * D_MODEL]))
    mu = jnp.mean(gv, axis=-1, keepdims=True)
    gc = gv - mu
    vn = gc * lax.rsqrt(jnp.mean(gc * gc, axis=-1, keepdims=True) + EPS)
    vn_ref[...] = (vn * lng_ref[...] + lnb_ref[...]).astype(BF16)

    q = _dot(h, wa_ref[:, 2 * D_MODEL:3 * D_MODEL]) * (HEAD_DIM ** -0.5 * LOG2E)
    k = _dot(h, wa_ref[:, 3 * D_MODEL:4 * D_MODEL])
    for p in range(HEAD_PAIRS):
        q_ref[0, p] = q[:, p * PAIR_W:(p + 1) * PAIR_W].astype(BF16)
        k_ref[0, p] = k[:, p * PAIR_W:(p + 1) * PAIR_W].astype(BF16)

    vt = _dot_nt(wvt_ref[...], h).astype(BF16)
    ones = jnp.ones((BF16_ROWS, MOBA_BLOCK), BF16)
    for p in range(HEAD_PAIRS):
        for jb in range(tm // MOBA_BLOCK):
            for a in range(2):
                r0 = p * PAIR_W + a * HEAD_DIM
                vt_ref[0, p, jb, a, 0:HEAD_DIM, :] = vt[r0:r0 + HEAD_DIM, jb * MOBA_BLOCK:(jb + 1) * MOBA_BLOCK]
                vt_ref[0, p, jb, a, HEAD_DIM:V_ROWS, :] = ones

    sga_ref[...] = jax.nn.sigmoid(_dot(h, wg_ref[:, 0:D_MODEL])).astype(BF16)
    sgb_ref[...] = jax.nn.sigmoid(_dot(h, wg_ref[:, D_MODEL:2 * D_MODEL])).astype(BF16)


def _in_proj(x2, g, wa, wvt, wg, lng, lnb, *, batch, seq, tm):
    n = x2.shape[0]
    tiles_per_seq = seq // tm
    nb = seq // MOBA_BLOCK
    row = lambda r: (r, 0)
    const = lambda r: (0, 0)
    resident = functools.partial(pl.BlockSpec, pipeline_mode=pl.Buffered(1))
    tok = jax.ShapeDtypeStruct((n, D_MODEL), BF16)
    pair_major = jax.ShapeDtypeStruct((batch, HEAD_PAIRS, seq, PAIR_W), BF16)
    return pl.pallas_call(
        functools.partial(_in_proj_kernel, tm=tm),
        grid=(n // tm,),
        in_specs=[
            pl.BlockSpec((tm, D_MODEL), row),
            resident((1, D_MODEL), const),
            resident((D_MODEL, 4 * D_MODEL), const),
            resident((D_MODEL, D_MODEL), const),
            resident((D_MODEL, 2 * D_MODEL), const),
            resident((1, D_MODEL), const),
            resident((1, D_MODEL), const),
        ],
        out_specs=[
            pl.BlockSpec((tm, D_MODEL), row),
            pl.BlockSpec((tm, D_MODEL), row),
            pl.BlockSpec((1, HEAD_PAIRS, tm, PAIR_W), lambda r: (r // tiles_per_seq, 0, r % tiles_per_seq, 0)),
            pl.BlockSpec((1, HEAD_PAIRS, tm, PAIR_W), lambda r: (r // tiles_per_seq, 0, r % tiles_per_seq, 0)),
            pl.BlockSpec((1, HEAD_PAIRS, tm // MOBA_BLOCK, 2, V_ROWS, MOBA_BLOCK),
                         lambda r: (r // tiles_per_seq, 0, r % tiles_per_seq, 0, 0, 0)),
            pl.BlockSpec((tm, D_MODEL), row),
            pl.BlockSpec((tm, D_MODEL), row),
        ],
        out_shape=[tok, tok, pair_major, pair_major,
                   jax.ShapeDtypeStruct((batch, HEAD_PAIRS, nb, 2, V_ROWS, MOBA_BLOCK), BF16),
                   tok, tok],
        compiler_params=pltpu.CompilerParams(dimension_semantics=("arbitrary",),
                                             vmem_limit_bytes=VMEM_LIMIT),
        name="in_proj",
    )(x2, g, wa, wvt, wg, lng, lnb)


def _bias_tiles_kernel(tab_ref, o_ref):
    h = pl.program_id(0)
    key = lax.broadcasted_iota(jnp.int32, (MOBA_BLOCK, MOBA_BLOCK), 0)
    qry = lax.broadcasted_iota(jnp.int32, (MOBA_BLOCK, MOBA_BLOCK), 1)
    max_exact = REL_BUCKETS // 2
    last = tab_ref[REL_BUCKETS - 1, h]
    for t in range(NEAR_TILES):
        dist = t * MOBA_BLOCK + qry - key
        n = jnp.maximum(dist, 0)
        val = jnp.full((MOBA_BLOCK, MOBA_BLOCK), tab_ref[0, h], F32)
        for b in range(1, max_exact + 1):
            val = jnp.where(n >= b, tab_ref[b, h], val)
        for kk, thr in enumerate(BUCKET_THRESHOLDS):
            val = jnp.where(n >= thr, tab_ref[max_exact + 1 + kk, h], val)
        val = (val - last) * LOG2E
        if t == 0:
            val = jnp.where(dist >= 0, val, -BIG)
        o_ref[0, t] = val
    o_ref[0, NEAR_TILES] = jnp.zeros((MOBA_BLOCK, MOBA_BLOCK), F32)


def _bias_tiles(rel_bias):
    return pl.pallas_call(
        _bias_tiles_kernel,
        grid=(ATT_HEADS,),
        in_specs=[pl.BlockSpec(memory_space=pltpu.SMEM)],
        out_specs=pl.BlockSpec((1, NEAR_TILES + 1, MOBA_BLOCK, MOBA_BLOCK), lambda h: (h, 0, 0, 0)),
        out_shape=jax.ShapeDtypeStruct((ATT_HEADS, NEAR_TILES + 1, MOBA_BLOCK, MOBA_BLOCK), F32),
        compiler_params=pltpu.CompilerParams(dimension_semantics=("arbitrary",)),
        name="bias_tiles",
    )(rel_bias)


def _attn_kernel(q_ref, k_ref, vt_ref, bias_ref, o_ref,
                 km_ref, kms_ref, qa_ref, sel_ref, acc_ref, s_ref, mt_ref, p_ref, alpha_ref, *, nb):
    step = pl.program_id(2)
    j_last = step * Q_BLOCKS + (Q_BLOCKS - 1)
    chains = [(h, a) for h in range(Q_BLOCKS) for a in range(2)]
    lane = lax.broadcasted_iota(jnp.int32, (1, PAIR_W), 1)

    @pl.when(step == 0)
    def _():
        def body(j, c):
            kb = k_ref[0, 0, pl.ds(pl.multiple_of(j * MOBA_BLOCK, MOBA_BLOCK), MOBA_BLOCK), :]
            km_ref[pl.ds(j, 1), :] = jnp.sum(kb.astype(F32), axis=0, keepdims=True) * (1.0 / MOBA_BLOCK)
            return c
        lax.fori_loop(0, nb, body, 0)
        km = km_ref[...]
        km_hi = km.astype(BF16)
        km_lo = (km - km_hi.astype(F32)).astype(BF16)
        for a in range(2):
            in_head = (lane >= a * HEAD_DIM) & (lane < (a + 1) * HEAD_DIM)
            kms_ref[(2 * a) * nb:(2 * a + 1) * nb, :] = jnp.where(in_head, km_hi, jnp.zeros_like(km_hi))
            kms_ref[(2 * a + 1) * nb:(2 * a + 2) * nb, :] = jnp.where(in_head, km_lo, jnp.zeros_like(km_lo))

    gates, owns = [], []
    for h in range(Q_BLOCKS):
        q = q_ref[0, 0, h * MOBA_BLOCK:(h + 1) * MOBA_BLOCK, :]
        g = _dot_nt(kms_ref[...], q)
        for a in range(2):
            qa_ref[2 * h + a] = jnp.where((lane >= a * HEAD_DIM) & (lane < (a + 1) * HEAD_DIM), q, jnp.zeros_like(q))
            gates.append(g[(2 * a) * nb:(2 * a + 1) * nb] + g[(2 * a + 1) * nb:(2 * a + 2) * nb])
            owns.append(jnp.full((1, MOBA_BLOCK), step * Q_BLOCKS + h, jnp.int32))
    gate = jnp.concatenate(gates, axis=1)
    own = jnp.concatenate(owns, axis=1)
    blk = lax.broadcasted_iota(jnp.int32, gate.shape, 0)
    gate = jnp.where(blk < own, gate, -jnp.inf)
    sel = blk == own
    for _ in range(MOBA_TOPK):
        top = jnp.max(gate, axis=0, keepdims=True)
        first = jnp.min(jnp.where(gate == top, blk, nb), axis=0, keepdims=True)
        pick = (blk == first) & (top > -jnp.inf)
        sel = sel | pick
        gate = jnp.where(pick, -jnp.inf, gate)
    sel_f = jnp.where(sel, 1.0, 0.0)
    for c in range(len(chains)):
        sel_ref[c] = sel_f[:, c * MOBA_BLOCK:(c + 1) * MOBA_BLOCK]
    def block_of(t):
        return jnp.maximum(j_last - t, 0)

    def key_block(j):
        return k_ref[0, 0, pl.ds(pl.multiple_of(j * MOBA_BLOCK, MOBA_BLOCK), MOBA_BLOCK), :]

    def scores(t, c, kj, near):
        h, a = chains[c]
        s = _dot_nt(kj, qa_ref[c])
        if near:
            s = s + bias_ref[a, jnp.clip(t - (Q_BLOCKS - 1 - h), 0, NEAR_TILES)]
        return s

    def chosen_row(t, c, j):
        return sel_ref[c, pl.ds(j, 1), :] > jnp.where(t <= j_last, 0.5, 2.0)

    def score_stage(t, slot, near):
        kj = key_block(block_of(t))
        for c in range(len(chains)):
            s = scores(t, c, kj, near)
            s_ref[slot, c] = s
            mt_ref[slot, c] = jnp.max(s, axis=0, keepdims=True)

    def softmax_stage(t, slot, m_prev):
        j = block_of(t)
        m_next, alphas, pvs = [], [], []
        for c, (h, a) in enumerate(chains):
            chosen = chosen_row(t, c, j)
            m_new = jnp.where(chosen, jnp.maximum(m_prev[c], mt_ref[slot, c]), m_prev[c])
            alphas.append(jnp.exp2(m_prev[c] - m_new))
            p = jnp.exp2(s_ref[slot, c] - jnp.where(chosen, m_new, BIG)).astype(BF16)
            pvs.append(_dot(vt_ref[0, 0, j, a], p))
            m_next.append(m_new)
        for c in range(len(chains)):
            acc_ref[c] = alphas[c] * acc_ref[c] + pvs[c]
        return tuple(m_next)

    def exact_pair(tt, m):
        t = 2 * tt
        score_stage(t + 1, 1, True)
        m = softmax_stage(t, 0, m)
        score_stage(t + 2, 0, True)
        m = softmax_stage(t + 1, 1, m)
        return m

    def lazy_score_stage(t, slot, carry, near):
        r, excess = carry
        j = block_of(t)
        kj = key_block(j)
        r_next, excess_next = [], []
        for c in range(len(chains)):
            s = scores(t, c, kj, near)
            chosen = chosen_row(t, c, j)
            tile_max = jnp.max(s, axis=0, keepdims=True)
            p_ref[slot, c] = jnp.exp2(s - jnp.where(chosen, r[c], BIG)).astype(BF16)
            r_new = jnp.where(chosen, jnp.maximum(r[c], tile_max), r[c])
            alpha_ref[slot, c] = jnp.exp2(r[c] - r_new)
            excess_next.append(jnp.maximum(excess[c], jnp.where(chosen, tile_max - r[c], -BIG)))
            r_next.append(r_new)
        return tuple(r_next), tuple(excess_next)

    def lazy_pv_stage(t, slot):
        j = block_of(t)
        pvs = [_dot(vt_ref[0, 0, j, a], p_ref[slot, c]) for c, (h, a) in enumerate(chains)]
        for c in range(len(chains)):
            acc_ref[c] = (acc_ref[c] + pvs[c]) * alpha_ref[slot, c]

    def lazy_pairs(it, carry, *, near, first_pair, pairs_per_trip):
        t = 2 * (first_pair + it * pairs_per_trip)
        for _ in range(pairs_per_trip):
            carry = lazy_score_stage(t + 1, 1, carry, near)
            lazy_pv_stage(t, 0)
            carry = lazy_score_stage(t + 2, 0, carry, near)
            lazy_pv_stage(t + 1, 1)
            t = t + 2
        return carry

    def finalize():
        for h in range(Q_BLOCKS):
            outs = []
            for a in range(2):
                acc = acc_ref[2 * h + a]
                outs.append(acc[0:HEAD_DIM] * (1.0 / acc[HEAD_DIM:HEAD_DIM + 1]))
            o_ref[0, 0, h * MOBA_BLOCK:(h + 1) * MOBA_BLOCK, :] = jnp.concatenate(outs, axis=0).T.astype(BF16)

    near_pairs = (NEAR_TILES + Q_BLOCKS) // 2
    n_pairs = (j_last + 2) // 2
    far_pairs = jnp.maximum(n_pairs - near_pairs, 0)
    lowest = (jnp.full((1, MOBA_BLOCK), -BIG, F32),) * len(chains)

    ones_rows = jnp.ones((BF16_ROWS, PAIR_W), BF16)
    seeds = []
    for c, (h, a) in enumerate(chains):
        own_scores = _dot_nt(ones_rows, qa_ref[c] * key_block(step * Q_BLOCKS + h))
        seeds.append(own_scores[0:1] + bias_ref[a, 0, 0:1, 0:1])

    acc_ref[...] = jnp.zeros_like(acc_ref)
    carry = lazy_score_stage(0, 0, (tuple(seeds), lowest), True)
    carry = lax.fori_loop(0, jnp.where(n_pairs >= near_pairs, 1, 0),
                          functools.partial(lazy_pairs, near=True, first_pair=0, pairs_per_trip=near_pairs), carry)
    carry = lax.fori_loop(0, jnp.where(n_pairs >= near_pairs, 0, n_pairs),
                          functools.partial(lazy_pairs, near=True, first_pair=0, pairs_per_trip=1), carry)
    first, left = near_pairs, far_pairs
    for pairs_per_trip in FAR_TRIP_PAIRS:
        trips = left // pairs_per_trip
        carry = lax.fori_loop(0, trips, functools.partial(lazy_pairs, near=False, first_pair=first,
                                                          pairs_per_trip=pairs_per_trip), carry)
        first, left = first + trips * pairs_per_trip, left - trips * pairs_per_trip
    finalize()

    worst = functools.reduce(jnp.maximum, carry[1])
    @pl.when(jnp.max(worst) > LAZY_MAX_EXCESS)
    def _():
        acc_ref[...] = jnp.zeros_like(acc_ref)
        score_stage(0, 0, True)
        lax.fori_loop(0, n_pairs, exact_pair, lowest)
        finalize()


def _attention(q, k, vt, bias_t):
    batch, _, seq, _ = q.shape
    nb = seq // MOBA_BLOCK
    tq = Q_BLOCKS * MOBA_BLOCK
    n_chains = 2 * Q_BLOCKS
    return pl.pallas_call(
        functools.partial(_attn_kernel, nb=nb),
        grid=(batch, HEAD_PAIRS, seq // tq),
        in_specs=[
            pl.BlockSpec((1, 1, tq, PAIR_W), lambda b, p, i: (b, p, i, 0)),
            pl.BlockSpec((1, 1, seq, PAIR_W), lambda b, p, i: (b, p, 0, 0)),
            pl.BlockSpec((1, 1, nb, 2, V_ROWS, MOBA_BLOCK), lambda b, p, i: (b, p, 0, 0, 0, 0)),
            pl.BlockSpec((2, NEAR_TILES + 1, MOBA_BLOCK, MOBA_BLOCK), lambda b, p, i: (p, 0, 0, 0)),
        ],
        out_specs=pl.BlockSpec((1, 1, tq, PAIR_W), lambda b, p, i: (b, p, i, 0)),
        out_shape=jax.ShapeDtypeStruct((batch, HEAD_PAIRS, seq, PAIR_W), BF16),
        scratch_shapes=[
            pltpu.VMEM((nb, PAIR_W), F32),
            pltpu.VMEM((4 * nb, PAIR_W), BF16),
            pltpu.VMEM((n_chains, MOBA_BLOCK, PAIR_W), BF16),
            pltpu.VMEM((n_chains, nb, MOBA_BLOCK), F32),
            pltpu.VMEM((n_chains, V_ROWS, MOBA_BLOCK), F32),
            pltpu.VMEM((2, n_chains, MOBA_BLOCK, MOBA_BLOCK), F32),
            pltpu.VMEM((2, n_chains, 1, MOBA_BLOCK), F32),
            pltpu.VMEM((2, n_chains, MOBA_BLOCK, MOBA_BLOCK), BF16),
            pltpu.VMEM((2, n_chains, 1, MOBA_BLOCK), F32),
        ],
        compiler_params=pltpu.CompilerParams(dimension_semantics=("arbitrary", "arbitrary", "arbitrary"),
                                             vmem_limit_bytes=VMEM_LIMIT),
        name="moba_attention",
    )(q, k, vt, bias_t)


def _post_kernel(x_ref, u_ref, vn_ref, sga_ref, sgb_ref, yb_ref, p_ref,
                 ws_ref, bs_ref, wo_ref, gffn_ref, w1_ref, w2_ref, gple_ref, wpg_ref, wpp_ref, gfin_ref,
                 o_ref, merged_ref, *, tm, final_norm):
    row = lax.broadcasted_iota(jnp.int32, (SGU_CHUNK, SGU_CHUNK), 0)
    col = lax.broadcasted_iota(jnp.int32, (SGU_CHUNK, SGU_CHUNK), 1)
    for g in range(SGU_GROUPS):
        w = jnp.where(row >= col, ws_ref[g], 0.0).astype(BF16)
        b = bs_ref[g]
        cols = slice(g * LANES, (g + 1) * LANES)
        for t in range(tm // SGU_CHUNK):
            rows = slice(t * SGU_CHUNK, (t + 1) * SGU_CHUNK)
            mixed = _dot(w, vn_ref[rows, cols]) + b
            y_a = u_ref[rows, cols].astype(F32) * mixed
            y_b = yb_ref[0, g, rows, :].astype(F32)
            merged = sga_ref[rows, cols].astype(F32) * y_a + sgb_ref[rows, cols].astype(F32) * y_b
            merged_ref[rows, cols] = merged.astype(BF16)

    x1 = x_ref[...] + _dot(merged_ref[...], wo_ref[...])

    h = _rms(x1, gffn_ref[...]).astype(BF16)
    x2 = x1
    for c in range(D_FF // D_MODEL):
        cs = slice(c * D_MODEL, (c + 1) * D_MODEL)
        a = jnp.square(jnp.maximum(_dot(h, w1_ref[:, cs]), 0.0)).astype(BF16)
        x2 = x2 + _dot(a, w2_ref[cs, :])

    gate = jax.nn.sigmoid(_dot(_rms(x2, gple_ref[...]).astype(BF16), wpg_ref[...]))
    x3 = x2 + gate * _dot(p_ref[...].astype(BF16), wpp_ref[...])
    o_ref[...] = _rms(x3, gfin_ref[...]) if final_norm else x3


def _post(x2, u, vn, sga, sgb, yb, p2, ws, bs, wo, gffn, w1, w2, gple, wpg, wpp, gfin, *, seq, tm, final_norm):
    n = x2.shape[0]
    tiles_per_seq = seq // tm
    row = lambda r: (r, 0)
    resident = functools.partial(pl.BlockSpec, pipeline_mode=pl.Buffered(1))
    c2 = lambda r: (0, 0)
    c3 = lambda r: (0, 0, 0)
    tokb = pl.BlockSpec((tm, D_MODEL), row)
    return pl.pallas_call(
        functools.partial(_post_kernel, tm=tm, final_norm=final_norm),
        grid=(n // tm,),
        in_specs=[
            tokb, tokb, tokb, tokb, tokb,
            pl.BlockSpec((1, HEAD_PAIRS, tm, PAIR_W), lambda r: (r // tiles_per_seq, 0, r % tiles_per_seq, 0)),
            pl.BlockSpec((tm, PLE_DIM), row),
            resident((SGU_GROUPS, SGU_CHUNK, SGU_CHUNK), c3),
            resident((SGU_GROUPS, SGU_CHUNK, 1), c3),
            resident((D_MODEL, D_MODEL), c2),
            resident((1, D_MODEL), c2),
            resident((D_MODEL, D_FF), c2),
            resident((D_FF, D_MODEL), c2),
            resident((1, D_MODEL), c2),
            resident((D_MODEL, D_MODEL), c2),
            resident((PLE_DIM, D_MODEL), c2),
            resident((1, D_MODEL), c2),
        ],
        out_specs=pl.BlockSpec((tm, D_MODEL), row),
        out_shape=jax.ShapeDtypeStruct((n, D_MODEL), F32),
        scratch_shapes=[pltpu.VMEM((tm, D_MODEL), BF16)],
        compiler_params=pltpu.CompilerParams(dimension_semantics=("arbitrary",),
                                             vmem_limit_bytes=VMEM_LIMIT),
        name="post",
    )(x2, u, vn, sga, sgb, yb, p2, ws, bs, wo, gffn, w1, w2, gple, wpg, wpp, gfin)


def kernel(x, p, norm_mix_g, w_in, w_sgu_spatial, b_sgu_spatial, ln_v_g, ln_v_b, rel_bias, w_out, norm_ffn_g,
           w_ff1, w_ff2, norm_ple_g, w_ple_gate, w_ple_proj, norm_final_g):
    batch, seq, _ = x.shape
    depth = w_in.shape[0]
    n = batch * seq
    assert seq % (Q_BLOCKS * MOBA_BLOCK) == 0 and seq // MOBA_BLOCK >= MOBA_TOPK
    tm_in = 512
    tm_post = 256
    vec = lambda g: g.reshape(1, D_MODEL)

    bias_t = _bias_tiles(rel_bias)
    x2 = x.reshape(n, D_MODEL)
    for i in range(depth):
        wi = w_in[i].astype(BF16)
        wa = wi[:, :4 * D_MODEL]
        wvt = wi[:, 4 * D_MODEL:5 * D_MODEL].T
        wg = wi[:, 5 * D_MODEL:]
        u, vn, q, k, vt, sga, sgb = _in_proj(
            x2, vec(norm_mix_g[i]), wa, wvt, wg, vec(ln_v_g[i]), vec(ln_v_b[i]),
            batch=batch, seq=seq, tm=tm_in)
        yb = _attention(q, k, vt, bias_t)
        x2 = _post(
            x2, u, vn, sga, sgb, yb, p[i].reshape(n, PLE_DIM),
            w_sgu_spatial[i], b_sgu_spatial[i].reshape(SGU_GROUPS, SGU_CHUNK, 1),
            w_out[i].astype(BF16), vec(norm_ffn_g[i]), w_ff1[i].astype(BF16), w_ff2[i].astype(BF16),
            vec(norm_ple_g[i]), w_ple_gate[i].astype(BF16), w_ple_proj[i].astype(BF16),
            vec(norm_final_g), seq=seq, tm=tm_post, final_norm=(i == depth - 1))
    return x2.reshape(batch, seq, D_MODEL)
```

```python
import functools
import math

import jax
import jax.numpy as jnp
from jax import lax
from jax.experimental import pallas as pl
from jax.experimental.pallas import tpu as pltpu

D_MODEL = 1024
PLE_DIM = 256
SGU_CHUNK = 128
SGU_GROUPS = 8
ATT_HEADS = 16
HEAD_DIM = 64
MOBA_BLOCK = 256
MOBA_TOPK = 3
REL_BUCKETS = 32
REL_MAX_DIST = 1024
D_FF = 4 * D_MODEL
EPS = 1e-6

LANES = 128
BF16_ROWS = 16
HEAD_PAIRS = ATT_HEADS // 2
PAIR_W = 2 * HEAD_DIM
V_ROWS = HEAD_DIM + BF16_ROWS
Q_BLOCKS = 2
FAR_TRIP_PAIRS = (4, 2, 1)
SEL_CHUNK = 8
LAZY_MAX_EXCESS = 16.0
BIG = 1e30
VMEM_LIMIT = 60 * 1024 * 1024
LOG2E = math.log2(math.e)

F32 = jnp.float32
BF16 = jnp.bfloat16


def _bucket_thresholds():
    max_exact = REL_BUCKETS // 2
    n_log = REL_BUCKETS - max_exact
    ratio = REL_MAX_DIST // max_exact
    out = []
    for k in range(1, n_log):
        d = max_exact
        target = (max_exact ** n_log) * (ratio ** k)
        while d ** n_log < target:
            d += 1
        out.append(d)
    return tuple(out)


BUCKET_THRESHOLDS = _bucket_thresholds()
NEAR_TILES = -(-(BUCKET_THRESHOLDS[-1] + MOBA_BLOCK - 1) // MOBA_BLOCK)
assert NEAR_TILES == 5


def _dot(a, b):
    return jnp.dot(a, b, preferred_element_type=F32)


def _dot_nt(a, b):
    return lax.dot_general(a, b, (((1,), (1,)), ((), ())), preferred_element_type=F32)


def _rms(x, g):
    return x * lax.rsqrt(jnp.mean(x * x, axis=-1, keepdims=True) + EPS) * g


def _in_proj_kernel(x_ref, g_ref, wa_ref, wvt_ref, wg_ref, lng_ref, lnb_ref,
                    u_ref, vn_ref, q_ref, k_ref, vt_ref, sga_ref, sgb_ref, *, tm):
    h = _rms(x_ref[...], g_ref[...]).astype(BF16)

    u_ref[...] = jax.nn.gelu(_dot(h, wa_ref[:, 0:D_MODEL])).astype(BF16)

    gv = jax.nn.gelu(_dot(h, wa_ref[:, D_MODEL:2 * D_MODEL]))
    mu = jnp.mean(gv, axis=-1, keepdims=True)
    gc = gv - mu
    vn = gc * lax.rsqrt(jnp.mean(gc * gc, axis=-1, keepdims=True) + EPS)
    vn_ref[...] = (vn * lng_ref[...] + lnb_ref[...]).astype(BF16)

    q = _dot(h, wa_ref[:, 2 * D_MODEL:3 * D_MODEL]) * (HEAD_DIM ** -0.5 * LOG2E)
    k = _dot(h, wa_ref[:, 3 * D_MODEL:4 * D_MODEL])
    for p in range(HEAD_PAIRS):
        q_ref[0, p] = q[:, p * PAIR_W:(p + 1) * PAIR_W].astype(BF16)
        k_ref[0, p] = k[:, p * PAIR_W:(p + 1) * PAIR_W].astype(BF16)

    vt = _dot_nt(wvt_ref[...], h).astype(BF16)
    ones = jnp.ones((BF16_ROWS, MOBA_BLOCK), BF16)
    for p in range(HEAD_PAIRS):
        for jb in range(tm // MOBA_BLOCK):
            for a in range(2):
                r0 = p * PAIR_W + a * HEAD_DIM
                vt_ref[0, p, jb, a, 0:HEAD_DIM, :] = vt[r0:r0 + HEAD_DIM, jb * MOBA_BLOCK:(jb + 1) * MOBA_BLOCK]
                vt_ref[0, p, jb, a, HEAD_DIM:V_ROWS, :] = ones

    sga_ref[...] = jax.nn.sigmoid(_dot(h, wg_ref[:, 0:D_MODEL])).astype(BF16)
    sgb_ref[...] = jax.nn.sigmoid(_dot(h, wg_ref[:, D_MODEL:2 * D_MODEL])).astype(BF16)


def _in_proj(x2, g, wa, wvt, wg, lng, lnb, *, batch, seq, tm):
    n = x2.shape[0]
    tiles_per_seq = seq // tm
    nb = seq // MOBA_BLOCK
    row = lambda r: (r, 0)
    const = lambda r: (0, 0)
    resident = functools.partial(pl.BlockSpec, pipeline_mode=pl.Buffered(1))
    tok = jax.ShapeDtypeStruct((n, D_MODEL), BF16)
    pair_major = jax.ShapeDtypeStruct((batch, HEAD_PAIRS, seq, PAIR_W), BF16)
    return pl.pallas_call(
        functools.partial(_in_proj_kernel, tm=tm),
        grid=(n // tm,),
        in_specs=[
            pl.BlockSpec((tm, D_MODEL), row),
            resident((1, D_MODEL), const),
            resident((D_MODEL, 4 * D_MODEL), const),
            resident((D_MODEL, D_MODEL), const),
            resident((D_MODEL, 2 * D_MODEL), const),
            resident((1, D_MODEL), const),
            resident((1, D_MODEL), const),
        ],
        out_specs=[
            pl.BlockSpec((tm, D_MODEL), row),
            pl.BlockSpec((tm, D_MODEL), row),
            pl.BlockSpec((1, HEAD_PAIRS, tm, PAIR_W), lambda r: (r // tiles_per_seq, 0, r % tiles_per_seq, 0)),
            pl.BlockSpec((1, HEAD_PAIRS, tm, PAIR_W), lambda r: (r // tiles_per_seq, 0, r % tiles_per_seq, 0)),
            pl.BlockSpec((1, HEAD_PAIRS, tm // MOBA_BLOCK, 2, V_ROWS, MOBA_BLOCK),
                         lambda r: (r // tiles_per_seq, 0, r % tiles_per_seq, 0, 0, 0)),
            pl.BlockSpec((tm, D_MODEL), row),
            pl.BlockSpec((tm, D_MODEL), row),
        ],
        out_shape=[tok, tok, pair_major, pair_major,
                   jax.ShapeDtypeStruct((batch, HEAD_PAIRS, nb, 2, V_ROWS, MOBA_BLOCK), BF16),
                   tok, tok],
        compiler_params=pltpu.CompilerParams(dimension_semantics=("arbitrary",),
                                             vmem_limit_bytes=VMEM_LIMIT),
        name="in_proj",
    )(x2, g, wa, wvt, wg, lng, lnb)


def _bias_tiles_kernel(tab_ref, o_ref):
    h = pl.program_id(0)
    key = lax.broadcasted_iota(jnp.int32, (MOBA_BLOCK, MOBA_BLOCK), 0)
    qry = lax.broadcasted_iota(jnp.int32, (MOBA_BLOCK, MOBA_BLOCK), 1)
    max_exact = REL_BUCKETS // 2
    last = tab_ref[REL_BUCKETS - 1, h]
    for t in range(NEAR_TILES):
        dist = t * MOBA_BLOCK + qry - key
        n = jnp.maximum(dist, 0)
        val = jnp.full((MOBA_BLOCK, MOBA_BLOCK), tab_ref[0, h], F32)
        for b in range(1, max_exact + 1):
            val = jnp.where(n >= b, tab_ref[b, h], val)
        for kk, thr in enumerate(BUCKET_THRESHOLDS):
            val = jnp.where(n >= thr, tab_ref[max_exact + 1 + kk, h], val)
        val = (val - last) * LOG2E
        if t == 0:
            val = jnp.where(dist >= 0, val, -BIG)
        o_ref[0, t] = val
    o_ref[0, NEAR_TILES] = jnp.zeros((MOBA_BLOCK, MOBA_BLOCK), F32)


def _bias_tiles(rel_bias):
    return pl.pallas_call(
        _bias_tiles_kernel,
        grid=(ATT_HEADS,),
        in_specs=[pl.BlockSpec(memory_space=pltpu.SMEM)],
        out_specs=pl.BlockSpec((1, NEAR_TILES + 1, MOBA_BLOCK, MOBA_BLOCK), lambda h: (h, 0, 0, 0)),
        out_shape=jax.ShapeDtypeStruct((ATT_HEADS, NEAR_TILES + 1, MOBA_BLOCK, MOBA_BLOCK), F32),
        compiler_params=pltpu.CompilerParams(dimension_semantics=("arbitrary",)),
        name="bias_tiles",
    )(rel_bias)


def _attn_kernel(q_ref, k_ref, vt_ref, bias_ref, o_ref,
                 km_ref, kms_ref, qa_ref, sel_ref, acc_ref, s_ref, mt_ref, p_ref, alpha_ref, *, nb):
    step = pl.program_id(2)
    j_last = step * Q_BLOCKS + (Q_BLOCKS - 1)
    chains = [(h, a) for h in range(Q_BLOCKS) for a in range(2)]
    lane = lax.broadcasted_iota(jnp.int32, (1, PAIR_W), 1)

    @pl.when(step == 0)
    def _():
        def body(j, c):
            kb = k_ref[0, 0, pl.ds(pl.multiple_of(j * MOBA_BLOCK, MOBA_BLOCK), MOBA_BLOCK), :]
            km_ref[pl.ds(j, 1), :] = jnp.sum(kb.astype(F32), axis=0, keepdims=True) * (1.0 / MOBA_BLOCK)
            return c
        lax.fori_loop(0, nb, body, 0)
        km = km_ref[...]
        km_hi = km.astype(BF16)
        km_lo = (km - km_hi.astype(F32)).astype(BF16)
        for a in range(2):
            in_head = (lane >= a * HEAD_DIM) & (lane < (a + 1) * HEAD_DIM)
            kms_ref[(2 * a) * nb:(2 * a + 1) * nb, :] = jnp.where(in_head, km_hi, jnp.zeros_like(km_hi))
            kms_ref[(2 * a + 1) * nb:(2 * a + 2) * nb, :] = jnp.where(in_head, km_lo, jnp.zeros_like(km_lo))

        n_q = SEL_CHUNK * MOBA_BLOCK
        def select_chunk(ci, c):
            q = q_ref[0, 0, pl.ds(pl.multiple_of(ci * n_q, n_q), n_q), :]
            g = _dot_nt(kms_ref[...], q)
            gate = jnp.concatenate([g[(2 * a) * nb:(2 * a + 1) * nb] + g[(2 * a + 1) * nb:(2 * a + 2) * nb]
                                    for a in range(2)], axis=1)
            pos = lax.broadcasted_iota(jnp.int32, (1, n_q), 1)
            own = ci * SEL_CHUNK + jnp.right_shift(pos, MOBA_BLOCK.bit_length() - 1)
            own = jnp.concatenate([own, own], axis=1)
            blk = lax.broadcasted_iota(jnp.int32, gate.shape, 0)
            gate = jnp.where(blk < own, gate, -jnp.inf)
            sel = blk == own
            for _ in range(MOBA_TOPK):
                top = jnp.max(gate, axis=0, keepdims=True)
                first = jnp.min(jnp.where(gate == top, blk, nb), axis=0, keepdims=True)
                pick = (blk == first) & (top > -jnp.inf)
                sel = sel | pick
                gate = jnp.where(pick, -jnp.inf, gate)
            sel_f = jnp.where(sel, 1.0, 0.0)
            for a in range(2):
                for b in range(SEL_CHUNK):
                    lo = a * n_q + b * MOBA_BLOCK
                    sel_ref[ci * SEL_CHUNK + b, a] = sel_f[:, lo:lo + MOBA_BLOCK]
            return c
        lax.fori_loop(0, nb // SEL_CHUNK, select_chunk, 0)

    for c, (h, a) in enumerate(chains):
        q = q_ref[0, 0, pl.ds(pl.multiple_of((step * Q_BLOCKS + h) * MOBA_BLOCK, MOBA_BLOCK), MOBA_BLOCK), :]
        qa_ref[c] = jnp.where((lane >= a * HEAD_DIM) & (lane < (a + 1) * HEAD_DIM), q, jnp.zeros_like(q))

    def block_of(t):
        return jnp.maximum(j_last - t, 0)

    def key_block(j):
        return k_ref[0, 0, pl.ds(pl.multiple_of(j * MOBA_BLOCK, MOBA_BLOCK), MOBA_BLOCK), :]

    def scores(t, c, kj, near):
        h, a = chains[c]
        s = _dot_nt(kj, qa_ref[c])
        offset = t - (Q_BLOCKS - 1 - h)
        if isinstance(t, int):
            if offset < NEAR_TILES:
                s = s + bias_ref[a, max(offset, 0)]
        elif near:
            s = s + bias_ref[a, jnp.clip(offset, 0, NEAR_TILES)]
        return s

    def visits(t, c):
        return not isinstance(t, int) or t - (Q_BLOCKS - 1 - chains[c][0]) >= 0

    def chosen_row(t, c, j):
        h, a = chains[c]
        return sel_ref[step * Q_BLOCKS + h, a, pl.ds(j, 1), :] > jnp.where(t <= j_last, 0.5, 2.0)

    def score_stage(t, slot, near):
        kj = key_block(block_of(t))
        for c in range(len(chains)):
            s = scores(t, c, kj, near)
            s_ref[slot, c] = s
            mt_ref[slot, c] = jnp.max(s, axis=0, keepdims=True)

    def softmax_stage(t, slot, m_prev):
        j = block_of(t)
        m_next, alphas, pvs = [], [], []
        for c, (h, a) in enumerate(chains):
            chosen = chosen_row(t, c, j)
            m_new = jnp.where(chosen, jnp.maximum(m_prev[c], mt_ref[slot, c]), m_prev[c])
            alphas.append(jnp.exp2(m_prev[c] - m_new))
            p = jnp.exp2(s_ref[slot, c] - jnp.where(chosen, m_new, BIG)).astype(BF16)
            pvs.append(_dot(vt_ref[0, 0, j, a], p))
            m_next.append(m_new)
        for c in range(len(chains)):
            acc_ref[c] = alphas[c] * acc_ref[c] + pvs[c]
        return tuple(m_next)

    def exact_pair(tt, m):
        t = 2 * tt
        score_stage(t + 1, 1, True)
        m = softmax_stage(t, 0, m)
        score_stage(t + 2, 0, True)
        m = softmax_stage(t + 1, 1, m)
        return m

    def lazy_score_stage(t, slot, carry, near):
        r, excess = carry
        j = block_of(t)
        kj = key_block(j)
        r_next, excess_next = [], []
        for c in range(len(chains)):
            if not visits(t, c):
                r_next.append(r[c])
                excess_next.append(excess[c])
                continue
            s = scores(t, c, kj, near)
            chosen = chosen_row(t, c, j)
            tile_max = jnp.max(s, axis=0, keepdims=True)
            p_ref[slot, c] = jnp.exp2(s - jnp.where(chosen, r[c], BIG)).astype(BF16)
            r_new = jnp.where(chosen, jnp.maximum(r[c], tile_max), r[c])
            alpha_ref[slot, c] = jnp.exp2(r[c] - r_new)
            excess_next.append(jnp.maximum(excess[c], jnp.where(chosen, tile_max - r[c], -BIG)))
            r_next.append(r_new)
        return tuple(r_next), tuple(excess_next)

    def lazy_pv_stage(t, slot):
        j = block_of(t)
        live = [c for c in range(len(chains)) if visits(t, c)]
        pvs = [_dot(vt_ref[0, 0, j, chains[c][1]], p_ref[slot, c]) for c in live]
        for c, pv in zip(live, pvs):
            acc_ref[c] = (acc_ref[c] + pv) * alpha_ref[slot, c]

    def lazy_pairs(it, carry, *, near, first_pair, pairs_per_trip, single_trip=False):
        if single_trip:
            t = 2 * first_pair
            carry = lazy_score_stage(t, 0, carry, near)
        else:
            t = 2 * (first_pair + it * pairs_per_trip)
        for _ in range(pairs_per_trip):
            carry = lazy_score_stage(t + 1, 1, carry, near)
            lazy_pv_stage(t, 0)
            carry = lazy_score_stage(t + 2, 0, carry, near)
            lazy_pv_stage(t + 1, 1)
            t = t + 2
        return carry

    def finalize():
        for h in range(Q_BLOCKS):
            outs = []
            for a in range(2):
                acc = acc_ref[2 * h + a]
                outs.append(acc[0:HEAD_DIM] * (1.0 / acc[HEAD_DIM:HEAD_DIM + 1]))
            o_ref[0, 0, h * MOBA_BLOCK:(h + 1) * MOBA_BLOCK, :] = jnp.concatenate(outs, axis=0).T.astype(BF16)

    near_pairs = (NEAR_TILES + Q_BLOCKS) // 2
    n_pairs = (j_last + 2) // 2
    far_pairs = jnp.maximum(n_pairs - near_pairs, 0)
    lowest = (jnp.full((1, MOBA_BLOCK), -BIG, F32),) * len(chains)

    ones_rows = jnp.ones((BF16_ROWS, PAIR_W), BF16)
    seeds = []
    for c, (h, a) in enumerate(chains):
        own_scores = _dot_nt(ones_rows, qa_ref[c] * key_block(step * Q_BLOCKS + h))
        seeds.append(own_scores[0:1] + bias_ref[a, 0, 0:1, 0:1])

    acc_ref[...] = jnp.zeros_like(acc_ref)
    carry = (tuple(seeds), lowest)
    carry = lax.fori_loop(0, jnp.where(n_pairs >= near_pairs, 1, 0),
                          functools.partial(lazy_pairs, near=True, first_pair=0, pairs_per_trip=near_pairs,
                                            single_trip=True), carry)

    def few_pairs(it, carry):
        carry = lazy_score_stage(2 * it, 0, carry, True)
        return lax.fori_loop(0, n_pairs, functools.partial(lazy_pairs, near=True, first_pair=0, pairs_per_trip=1), carry)
    carry = lax.fori_loop(0, jnp.where(n_pairs >= near_pairs, 0, 1), few_pairs, carry)
    first, left = near_pairs, far_pairs
    for pairs_per_trip in FAR_TRIP_PAIRS:
        trips = left // pairs_per_trip
        carry = lax.fori_loop(0, trips, functools.partial(lazy_pairs, near=False, first_pair=first,
                                                          pairs_per_trip=pairs_per_trip), carry)
        first, left = first + trips * pairs_per_trip, left - trips * pairs_per_trip
    finalize()

    worst = functools.reduce(jnp.maximum, carry[1])
    @pl.when(jnp.max(worst) > LAZY_MAX_EXCESS)
    def _():
        acc_ref[...] = jnp.zeros_like(acc_ref)
        score_stage(0, 0, True)
        lax.fori_loop(0, n_pairs, exact_pair, lowest)
        finalize()


def _attention(q, k, vt, bias_t):
    batch, _, seq, _ = q.shape
    nb = seq // MOBA_BLOCK
    tq = Q_BLOCKS * MOBA_BLOCK
    n_chains = 2 * Q_BLOCKS
    return pl.pallas_call(
        functools.partial(_attn_kernel, nb=nb),
        grid=(batch, HEAD_PAIRS, seq // tq),
        in_specs=[
            pl.BlockSpec((1, 1, seq, PAIR_W), lambda b, p, i: (b, p, 0, 0)),
            pl.BlockSpec((1, 1, seq, PAIR_W), lambda b, p, i: (b, p, 0, 0)),
            pl.BlockSpec((1, 1, nb, 2, V_ROWS, MOBA_BLOCK), lambda b, p, i: (b, p, 0, 0, 0, 0)),
            pl.BlockSpec((2, NEAR_TILES + 1, MOBA_BLOCK, MOBA_BLOCK), lambda b, p, i: (p, 0, 0, 0)),
        ],
        out_specs=pl.BlockSpec((1, 1, tq, PAIR_W), lambda b, p, i: (b, p, i, 0)),
        out_shape=jax.ShapeDtypeStruct((batch, HEAD_PAIRS, seq, PAIR_W), BF16),
        scratch_shapes=[
            pltpu.VMEM((nb, PAIR_W), F32),
            pltpu.VMEM((4 * nb, PAIR_W), BF16),
            pltpu.VMEM((n_chains, MOBA_BLOCK, PAIR_W), BF16),
            pltpu.VMEM((nb, 2, nb, MOBA_BLOCK), F32),
            pltpu.VMEM((n_chains, V_ROWS, MOBA_BLOCK), F32),
            pltpu.VMEM((2, n_chains, MOBA_BLOCK, MOBA_BLOCK), F32),
            pltpu.VMEM((2, n_chains, 1, MOBA_BLOCK), F32),
            pltpu.VMEM((2, n_chains, MOBA_BLOCK, MOBA_BLOCK), BF16),
            pltpu.VMEM((2, n_chains, 1, MOBA_BLOCK), F32),
        ],
        compiler_params=pltpu.CompilerParams(dimension_semantics=("arbitrary", "arbitrary", "arbitrary"),
                                             vmem_limit_bytes=VMEM_LIMIT),
        name="moba_attention",
    )(q, k, vt, bias_t)


def _post_kernel(x_ref, u_ref, vn_ref, sga_ref, sgb_ref, yb_ref, p_ref,
                 ws_ref, bs_ref, wo_ref, gffn_ref, w1_ref, w2_ref, gple_ref, wpg_ref, wpp_ref, gfin_ref,
                 o_ref, merged_ref, *, tm, final_norm):
    row = lax.broadcasted_iota(jnp.int32, (SGU_CHUNK, SGU_CHUNK), 0)
    col = lax.broadcasted_iota(jnp.int32, (SGU_CHUNK, SGU_CHUNK), 1)
    for g in range(SGU_GROUPS):
        w = jnp.where(row >= col, ws_ref[g], 0.0).astype(BF16)
        b = bs_ref[g]
        cols = slice(g * LANES, (g + 1) * LANES)
        for t in range(tm // SGU_CHUNK):
            rows = slice(t * SGU_CHUNK, (t + 1) * SGU_CHUNK)
            mixed = _dot(w, vn_ref[rows, cols]) + b
            y_a = u_ref[rows, cols].astype(F32) * mixed
            y_b = yb_ref[0, g, rows, :].astype(F32)
            merged = sga_ref[rows, cols].astype(F32) * y_a + sgb_ref[rows, cols].astype(F32) * y_b
            merged_ref[rows, cols] = merged.astype(BF16)

    x1 = x_ref[...] + _dot(merged_ref[...], wo_ref[...])

    h = _rms(x1, gffn_ref[...]).astype(BF16)
    x2 = x1
    for c in range(D_FF // D_MODEL):
        cs = slice(c * D_MODEL, (c + 1) * D_MODEL)
        a = jnp.square(jnp.maximum(_dot(h, w1_ref[:, cs]), 0.0)).astype(BF16)
        x2 = x2 + _dot(a, w2_ref[cs, :])

    gate = jax.nn.sigmoid(_dot(_rms(x2, gple_ref[...]).astype(BF16), wpg_ref[...]))
    x3 = x2 + gate * _dot(p_ref[...].astype(BF16), wpp_ref[...])
    o_ref[...] = _rms(x3, gfin_ref[...]) if final_norm else x3


def _post(x2, u, vn, sga, sgb, yb, p2, ws, bs, wo, gffn, w1, w2, gple, wpg, wpp, gfin, *, seq, tm, final_norm):
    n = x2.shape[0]
    tiles_per_seq = seq // tm
    row = lambda r: (r, 0)
    resident = functools.partial(pl.BlockSpec, pipeline_mode=pl.Buffered(1))
    c2 = lambda r: (0, 0)
    c3 = lambda r: (0, 0, 0)
    tokb = pl.BlockSpec((tm, D_MODEL), row)
    return pl.pallas_call(
        functools.partial(_post_kernel, tm=tm, final_norm=final_norm),
        grid=(n // tm,),
        in_specs=[
            tokb, tokb, tokb, tokb, tokb,
            pl.BlockSpec((1, HEAD_PAIRS, tm, PAIR_W), lambda r: (r // tiles_per_seq, 0, r % tiles_per_seq, 0)),
            pl.BlockSpec((tm, PLE_DIM), row),
            resident((SGU_GROUPS, SGU_CHUNK, SGU_CHUNK), c3),
            resident((SGU_GROUPS, SGU_CHUNK, 1), c3),
            resident((D_MODEL, D_MODEL), c2),
            resident((1, D_MODEL), c2),
            resident((D_MODEL, D_FF), c2),
            resident((D_FF, D_MODEL), c2),
            resident((1, D_MODEL), c2),
            resident((D_MODEL, D_MODEL), c2),
            resident((PLE_DIM, D_MODEL), c2),
            resident((1, D_MODEL), c2),
        ],
        out_specs=pl.BlockSpec((tm, D_MODEL), row),
        out_shape=jax.ShapeDtypeStruct((n, D_MODEL), F32),
        scratch_shapes=[pltpu.VMEM((tm, D_MODEL), BF16)],
        compiler_params=pltpu.CompilerParams(dimension_semantics=("arbitrary",),
                                             vmem_limit_bytes=VMEM_LIMIT),
        name="post",
    )(x2, u, vn, sga, sgb, yb, p2, ws, bs, wo, gffn, w1, w2, gple, wpg, wpp, gfin)


def kernel(x, p, norm_mix_g, w_in, w_sgu_spatial, b_sgu_spatial, ln_v_g, ln_v_b, rel_bias, w_out, norm_ffn_g,
           w_ff1, w_ff2, norm_ple_g, w_ple_gate, w_ple_proj, norm_final_g):
    batch, seq, _ = x.shape
    depth = w_in.shape[0]
    n = batch * seq
    assert seq % (Q_BLOCKS * MOBA_BLOCK) == 0 and seq // MOBA_BLOCK >= MOBA_TOPK
    tm_in = 512
    tm_post = 256
    vec = lambda g: g.reshape(1, D_MODEL)

    bias_t = _bias_tiles(rel_bias)
    x2 = x.reshape(n, D_MODEL)
    for i in range(depth):
        wi = w_in[i].astype(BF16)
        wa = wi[:, :4 * D_MODEL]
        wvt = wi[:, 4 * D_MODEL:5 * D_MODEL].T
        wg = wi[:, 5 * D_MODEL:]
        u, vn, q, k, vt, sga, sgb = _in_proj(
            x2, vec(norm_mix_g[i]), wa, wvt, wg, vec(ln_v_g[i]), vec(ln_v_b[i]),
            batch=batch, seq=seq, tm=tm_in)
        yb = _attention(q, k, vt, bias_t)
        x2 = _post(
            x2, u, vn, sga, sgb, yb, p[i].reshape(n, PLE_DIM),
            w_sgu_spatial[i], b_sgu_spatial[i].reshape(SGU_GROUPS, SGU_CHUNK, 1),
            w_out[i].astype(BF16), vec(norm_ffn_g[i]), w_ff1[i].astype(BF16), w_ff2[i].astype(BF16),
            vec(norm_ple_g[i]), w_ple_gate[i].astype(BF16), w_ple_proj[i].astype(BF16),
            vec(norm_final_g), seq=seq, tm=tm_post, final_norm=(i == depth - 1))
    return x2.reshape(batch, seq, D_MODEL)
```

```python
import functools
import math

import jax
import jax.numpy as jnp
from jax import lax
from jax.experimental import pallas as pl
from jax.experimental.pallas import tpu as pltpu

D_MODEL = 1024
PLE_DIM = 256
SGU_CHUNK = 128
SGU_GROUPS = 8
ATT_HEADS = 16
HEAD_DIM = 64
MOBA_BLOCK = 256
MOBA_TOPK = 3
REL_BUCKETS = 32
REL_MAX_DIST = 1024
D_FF = 4 * D_MODEL
EPS = 1e-6

LANES = 128
BF16_ROWS = 16
HEAD_PAIRS = ATT_HEADS // 2
PAIR_W = 2 * HEAD_DIM
V_ROWS = HEAD_DIM + BF16_ROWS
Q_BLOCKS = 4
FAR_TRIP_PAIRS = (2, 1)
SEL_CHUNK = 8
LAZY_MAX_EXCESS = 16.0
BIG = 1e30
VMEM_LIMIT = 60 * 1024 * 1024
LOG2E = math.log2(math.e)

F32 = jnp.float32
BF16 = jnp.bfloat16


def _bucket_thresholds():
    max_exact = REL_BUCKETS // 2
    n_log = REL_BUCKETS - max_exact
    ratio = REL_MAX_DIST // max_exact
    out = []
    for k in range(1, n_log):
        d = max_exact
        target = (max_exact ** n_log) * (ratio ** k)
        while d ** n_log < target:
            d += 1
        out.append(d)
    return tuple(out)


BUCKET_THRESHOLDS = _bucket_thresholds()
NEAR_TILES = -(-(BUCKET_THRESHOLDS[-1] + MOBA_BLOCK - 1) // MOBA_BLOCK)
assert NEAR_TILES == 5


def _dot(a, b):
    return jnp.dot(a, b, preferred_element_type=F32)


def _dot_nt(a, b):
    return lax.dot_general(a, b, (((1,), (1,)), ((), ())), preferred_element_type=F32)


def _rms(x, g):
    return x * lax.rsqrt(jnp.mean(x * x, axis=-1, keepdims=True) + EPS) * g


def _in_proj_kernel(x_ref, g_ref, wa_ref, wvt_ref, wg_ref, lng_ref, lnb_ref,
                    u_ref, vn_ref, q_ref, k_ref, vt_ref, sga_ref, sgb_ref, *, tm):
    h = _rms(x_ref[...], g_ref[...]).astype(BF16)

    u_ref[...] = jax.nn.gelu(_dot(h, wa_ref[:, 0:D_MODEL])).astype(BF16)

    gv = jax.nn.gelu(_dot(h, wa_ref[:, D_MODEL:2 * D_MODEL]))
    mu = jnp.mean(gv, axis=-1, keepdims=True)
    gc = gv - mu
    vn = gc * lax.rsqrt(jnp.mean(gc * gc, axis=-1, keepdims=True) + EPS)
    vn_ref[...] = (vn * lng_ref[...] + lnb_ref[...]).astype(BF16)

    q = _dot(h, wa_ref[:, 2 * D_MODEL:3 * D_MODEL]) * (HEAD_DIM ** -0.5 * LOG2E)
    k = _dot(h, wa_ref[:, 3 * D_MODEL:4 * D_MODEL])
    for p in range(HEAD_PAIRS):
        q_ref[0, p] = q[:, p * PAIR_W:(p + 1) * PAIR_W].astype(BF16)
        k_ref[0, p] = k[:, p * PAIR_W:(p + 1) * PAIR_W].astype(BF16)

    vt = _dot_nt(wvt_ref[...], h).astype(BF16)
    ones = jnp.ones((BF16_ROWS, MOBA_BLOCK), BF16)
    for p in range(HEAD_PAIRS):
        for jb in range(tm // MOBA_BLOCK):
            for a in range(2):
                r0 = p * PAIR_W + a * HEAD_DIM
                vt_ref[0, p, jb, a, 0:HEAD_DIM, :] = vt[r0:r0 + HEAD_DIM, jb * MOBA_BLOCK:(jb + 1) * MOBA_BLOCK]
                vt_ref[0, p, jb, a, HEAD_DIM:V_ROWS, :] = ones

    sga_ref[...] = jax.nn.sigmoid(_dot(h, wg_ref[:, 0:D_MODEL])).astype(BF16)
    sgb_ref[...] = jax.nn.sigmoid(_dot(h, wg_ref[:, D_MODEL:2 * D_MODEL])).astype(BF16)


def _in_proj(x2, g, wa, wvt, wg, lng, lnb, *, batch, seq, tm):
    n = x2.shape[0]
    tiles_per_seq = seq // tm
    nb = seq // MOBA_BLOCK
    row = lambda r: (r, 0)
    const = lambda r: (0, 0)
    resident = functools.partial(pl.BlockSpec, pipeline_mode=pl.Buffered(1))
    tok = jax.ShapeDtypeStruct((n, D_MODEL), BF16)
    pair_major = jax.ShapeDtypeStruct((batch, HEAD_PAIRS, seq, PAIR_W), BF16)
    return pl.pallas_call(
        functools.partial(_in_proj_kernel, tm=tm),
        grid=(n // tm,),
        in_specs=[
            pl.BlockSpec((tm, D_MODEL), row),
            resident((1, D_MODEL), const),
            resident((D_MODEL, 4 * D_MODEL), const),
            resident((D_MODEL, D_MODEL), const),
            resident((D_MODEL, 2 * D_MODEL), const),
            resident((1, D_MODEL), const),
            resident((1, D_MODEL), const),
        ],
        out_specs=[
            pl.BlockSpec((tm, D_MODEL), row),
            pl.BlockSpec((tm, D_MODEL), row),
            pl.BlockSpec((1, HEAD_PAIRS, tm, PAIR_W), lambda r: (r // tiles_per_seq, 0, r % tiles_per_seq, 0)),
            pl.BlockSpec((1, HEAD_PAIRS, tm, PAIR_W), lambda r: (r // tiles_per_seq, 0, r % tiles_per_seq, 0)),
            pl.BlockSpec((1, HEAD_PAIRS, tm // MOBA_BLOCK, 2, V_ROWS, MOBA_BLOCK),
                         lambda r: (r // tiles_per_seq, 0, r % tiles_per_seq, 0, 0, 0)),
            pl.BlockSpec((tm, D_MODEL), row),
            pl.BlockSpec((tm, D_MODEL), row),
        ],
        out_shape=[tok, tok, pair_major, pair_major,
                   jax.ShapeDtypeStruct((batch, HEAD_PAIRS, nb, 2, V_ROWS, MOBA_BLOCK), BF16),
                   tok, tok],
        compiler_params=pltpu.CompilerParams(dimension_semantics=("arbitrary",),
                                             vmem_limit_bytes=VMEM_LIMIT),
        name="in_proj",
    )(x2, g, wa, wvt, wg, lng, lnb)


def _bias_tiles_kernel(tab_ref, o_ref):
    h = pl.program_id(0)
    key = lax.broadcasted_iota(jnp.int32, (MOBA_BLOCK, MOBA_BLOCK), 0)
    qry = lax.broadcasted_iota(jnp.int32, (MOBA_BLOCK, MOBA_BLOCK), 1)
    max_exact = REL_BUCKETS // 2
    last = tab_ref[REL_BUCKETS - 1, h]
    for t in range(NEAR_TILES):
        dist = t * MOBA_BLOCK + qry - key
        n = jnp.maximum(dist, 0)
        val = jnp.full((MOBA_BLOCK, MOBA_BLOCK), tab_ref[0, h], F32)
        for b in range(1, max_exact + 1):
            val = jnp.where(n >= b, tab_ref[b, h], val)
        for kk, thr in enumerate(BUCKET_THRESHOLDS):
            val = jnp.where(n >= thr, tab_ref[max_exact + 1 + kk, h], val)
        val = (val - last) * LOG2E
        if t == 0:
            val = jnp.where(dist >= 0, val, -BIG)
        o_ref[0, t] = val
    o_ref[0, NEAR_TILES] = jnp.zeros((MOBA_BLOCK, MOBA_BLOCK), F32)


def _bias_tiles(rel_bias):
    return pl.pallas_call(
        _bias_tiles_kernel,
        grid=(ATT_HEADS,),
        in_specs=[pl.BlockSpec(memory_space=pltpu.SMEM)],
        out_specs=pl.BlockSpec((1, NEAR_TILES + 1, MOBA_BLOCK, MOBA_BLOCK), lambda h: (h, 0, 0, 0)),
        out_shape=jax.ShapeDtypeStruct((ATT_HEADS, NEAR_TILES + 1, MOBA_BLOCK, MOBA_BLOCK), F32),
        compiler_params=pltpu.CompilerParams(dimension_semantics=("arbitrary",)),
        name="bias_tiles",
    )(rel_bias)


def _attn_kernel(q_ref, k_ref, vt_ref, bias_ref, o_ref,
                 km_ref, kms_ref, qa_ref, sel_ref, acc_ref, s_ref, mt_ref, p_ref, alpha_ref, *, nb):
    step = pl.program_id(2)
    j_last = step * Q_BLOCKS + (Q_BLOCKS - 1)
    chains = [(h, a) for h in range(Q_BLOCKS) for a in range(2)]
    lane = lax.broadcasted_iota(jnp.int32, (1, PAIR_W), 1)

    @pl.when(step == 0)
    def _():
        def body(j, c):
            kb = k_ref[0, 0, pl.ds(pl.multiple_of(j * MOBA_BLOCK, MOBA_BLOCK), MOBA_BLOCK), :]
            km_ref[pl.ds(j, 1), :] = jnp.sum(kb.astype(F32), axis=0, keepdims=True) * (1.0 / MOBA_BLOCK)
            return c
        lax.fori_loop(0, nb, body, 0)
        km = km_ref[...]
        km_hi = km.astype(BF16)
        km_lo = (km - km_hi.astype(F32)).astype(BF16)
        for a in range(2):
            in_head = (lane >= a * HEAD_DIM) & (lane < (a + 1) * HEAD_DIM)
            kms_ref[(2 * a) * nb:(2 * a + 1) * nb, :] = jnp.where(in_head, km_hi, jnp.zeros_like(km_hi))
            kms_ref[(2 * a + 1) * nb:(2 * a + 2) * nb, :] = jnp.where(in_head, km_lo, jnp.zeros_like(km_lo))

        n_q = SEL_CHUNK * MOBA_BLOCK
        def select_chunk(ci, c):
            q = q_ref[0, 0, pl.ds(pl.multiple_of(ci * n_q, n_q), n_q), :]
            g = _dot_nt(kms_ref[...], q)
            gate = jnp.concatenate([g[(2 * a) * nb:(2 * a + 1) * nb] + g[(2 * a + 1) * nb:(2 * a + 2) * nb]
                                    for a in range(2)], axis=1)
            pos = lax.broadcasted_iota(jnp.int32, (1, n_q), 1)
            own = ci * SEL_CHUNK + jnp.right_shift(pos, MOBA_BLOCK.bit_length() - 1)
            own = jnp.concatenate([own, own], axis=1)
            blk = lax.broadcasted_iota(jnp.int32, gate.shape, 0)
            gate = jnp.where(blk < own, gate, -jnp.inf)
            sel = blk == own
            for _ in range(MOBA_TOPK):
                top = jnp.max(gate, axis=0, keepdims=True)
                first = jnp.min(jnp.where(gate == top, blk, nb), axis=0, keepdims=True)
                pick = (blk == first) & (top > -jnp.inf)
                sel = sel | pick
                gate = jnp.where(pick, -jnp.inf, gate)
            sel_f = jnp.where(sel, 1.0, 0.0)
            for a in range(2):
                for b in range(SEL_CHUNK):
                    lo = a * n_q + b * MOBA_BLOCK
                    sel_ref[ci * SEL_CHUNK + b, a] = sel_f[:, lo:lo + MOBA_BLOCK]
            return c
        lax.fori_loop(0, nb // SEL_CHUNK, select_chunk, 0)

    for c, (h, a) in enumerate(chains):
        q = q_ref[0, 0, pl.ds(pl.multiple_of((step * Q_BLOCKS + h) * MOBA_BLOCK, MOBA_BLOCK), MOBA_BLOCK), :]
        qa_ref[c] = jnp.where((lane >= a * HEAD_DIM) & (lane < (a + 1) * HEAD_DIM), q, jnp.zeros_like(q))

    def block_of(t):
        return jnp.maximum(j_last - t, 0)

    def key_block(j):
        return k_ref[0, 0, pl.ds(pl.multiple_of(j * MOBA_BLOCK, MOBA_BLOCK), MOBA_BLOCK), :]

    def scores(t, c, kj, near):
        h, a = chains[c]
        s = _dot_nt(kj, qa_ref[c])
        offset = t - (Q_BLOCKS - 1 - h)
        if isinstance(t, int):
            if offset < NEAR_TILES:
                s = s + bias_ref[a, max(offset, 0)]
        elif near:
            s = s + bias_ref[a, jnp.clip(offset, 0, NEAR_TILES)]
        return s

    def visits(t, c):
        return not isinstance(t, int) or t - (Q_BLOCKS - 1 - chains[c][0]) >= 0

    def chosen_row(t, c, j):
        h, a = chains[c]
        return sel_ref[step * Q_BLOCKS + h, a, pl.ds(j, 1), :] > jnp.where(t <= j_last, 0.5, 2.0)

    def score_stage(t, slot, near):
        kj = key_block(block_of(t))
        for c in range(len(chains)):
            s = scores(t, c, kj, near)
            s_ref[slot, c] = s
            mt_ref[slot, c] = jnp.max(s, axis=0, keepdims=True)

    def softmax_stage(t, slot, m_prev):
        j = block_of(t)
        m_next, alphas, pvs = [], [], []
        for c, (h, a) in enumerate(chains):
            chosen = chosen_row(t, c, j)
            m_new = jnp.where(chosen, jnp.maximum(m_prev[c], mt_ref[slot, c]), m_prev[c])
            alphas.append(jnp.exp2(m_prev[c] - m_new))
            p = jnp.exp2(s_ref[slot, c] - jnp.where(chosen, m_new, BIG)).astype(BF16)
            pvs.append(_dot(vt_ref[0, 0, j, a], p))
            m_next.append(m_new)
        for c in range(len(chains)):
            acc_ref[c] = alphas[c] * acc_ref[c] + pvs[c]
        return tuple(m_next)

    def exact_pair(tt, m):
        t = 2 * tt
        score_stage(t + 1, 1, True)
        m = softmax_stage(t, 0, m)
        score_stage(t + 2, 0, True)
        m = softmax_stage(t + 1, 1, m)
        return m

    def lazy_score_stage(t, slot, carry, near):
        r, excess = carry
        j = block_of(t)
        kj = key_block(j)
        r_next, excess_next = [], []
        for c in range(len(chains)):
            if not visits(t, c):
                r_next.append(r[c])
                excess_next.append(excess[c])
                continue
            s = scores(t, c, kj, near)
            chosen = chosen_row(t, c, j)
            tile_max = jnp.max(s, axis=0, keepdims=True)
            p_ref[slot, c] = jnp.exp2(s - jnp.where(chosen, r[c], BIG)).astype(BF16)
            r_new = jnp.where(chosen, jnp.maximum(r[c], tile_max), r[c])
            alpha_ref[slot, c] = jnp.exp2(r[c] - r_new)
            excess_next.append(jnp.maximum(excess[c], jnp.where(chosen, tile_max - r[c], -BIG)))
            r_next.append(r_new)
        return tuple(r_next), tuple(excess_next)

    def lazy_pv_stage(t, slot):
        j = block_of(t)
        live = [c for c in range(len(chains)) if visits(t, c)]
        pvs = [_dot(vt_ref[0, 0, j, chains[c][1]], p_ref[slot, c]) for c in live]
        for c, pv in zip(live, pvs):
            acc_ref[c] = (acc_ref[c] + pv) * alpha_ref[slot, c]

    def lazy_pairs(it, carry, *, near, first_pair, pairs_per_trip, single_trip=False):
        if single_trip:
            t = 2 * first_pair
            carry = lazy_score_stage(t, 0, carry, near)
        else:
            t = 2 * (first_pair + it * pairs_per_trip)
        for _ in range(pairs_per_trip):
            carry = lazy_score_stage(t + 1, 1, carry, near)
            lazy_pv_stage(t, 0)
            carry = lazy_score_stage(t + 2, 0, carry, near)
            lazy_pv_stage(t + 1, 1)
            t = t + 2
        return carry

    def finalize():
        for h in range(Q_BLOCKS):
            outs = []
            for a in range(2):
                acc = acc_ref[2 * h + a]
                outs.append(acc[0:HEAD_DIM] * (1.0 / acc[HEAD_DIM:HEAD_DIM + 1]))
            o_ref[0, 0, h * MOBA_BLOCK:(h + 1) * MOBA_BLOCK, :] = jnp.concatenate(outs, axis=0).T.astype(BF16)

    near_pairs = (NEAR_TILES + Q_BLOCKS) // 2
    n_pairs = (j_last + 2) // 2
    far_pairs = jnp.maximum(n_pairs - near_pairs, 0)
    lowest = (jnp.full((1, MOBA_BLOCK), -BIG, F32),) * len(chains)

    ones_rows = jnp.ones((BF16_ROWS, PAIR_W), BF16)
    seeds = []
    for c, (h, a) in enumerate(chains):
        own_scores = _dot_nt(ones_rows, qa_ref[c] * key_block(step * Q_BLOCKS + h))
        seeds.append(own_scores[0:1] + bias_ref[a, 0, 0:1, 0:1])

    acc_ref[...] = jnp.zeros_like(acc_ref)
    carry = (tuple(seeds), lowest)
    carry = lax.fori_loop(0, jnp.where(n_pairs >= near_pairs, 1, 0),
                          functools.partial(lazy_pairs, near=True, first_pair=0, pairs_per_trip=near_pairs,
                                            single_trip=True), carry)

    def few_pairs(it, carry):
        carry = lazy_score_stage(2 * it, 0, carry, True)
        return lax.fori_loop(0, n_pairs, functools.partial(lazy_pairs, near=True, first_pair=0, pairs_per_trip=1), carry)
    carry = lax.fori_loop(0, jnp.where(n_pairs >= near_pairs, 0, 1), few_pairs, carry)
    first, left = near_pairs, far_pairs
    for pairs_per_trip in FAR_TRIP_PAIRS:
        trips = left // pairs_per_trip
        carry = lax.fori_loop(0, trips, functools.partial(lazy_pairs, near=False, first_pair=first,
                                                          pairs_per_trip=pairs_per_trip), carry)
        first, left = first + trips * pairs_per_trip, left - trips * pairs_per_trip
    finalize()

    worst = functools.reduce(jnp.maximum, carry[1])
    @pl.when(jnp.max(worst) > LAZY_MAX_EXCESS)
    def _():
        acc_ref[...] = jnp.zeros_like(acc_ref)
        score_stage(0, 0, True)
        lax.fori_loop(0, n_pairs, exact_pair, lowest)
        finalize()


def _attention(q, k, vt, bias_t):
    batch, _, seq, _ = q.shape
    nb = seq // MOBA_BLOCK
    tq = Q_BLOCKS * MOBA_BLOCK
    n_chains = 2 * Q_BLOCKS
    return pl.pallas_call(
        functools.partial(_attn_kernel, nb=nb),
        grid=(batch, HEAD_PAIRS, seq // tq),
        in_specs=[
            pl.BlockSpec((1, 1, seq, PAIR_W), lambda b, p, i: (b, p, 0, 0)),
            pl.BlockSpec((1, 1, seq, PAIR_W), lambda b, p, i: (b, p, 0, 0)),
            pl.BlockSpec((1, 1, nb, 2, V_ROWS, MOBA_BLOCK), lambda b, p, i: (b, p, 0, 0, 0, 0)),
            pl.BlockSpec((2, NEAR_TILES + 1, MOBA_BLOCK, MOBA_BLOCK), lambda b, p, i: (p, 0, 0, 0)),
        ],
        out_specs=pl.BlockSpec((1, 1, tq, PAIR_W), lambda b, p, i: (b, p, i, 0)),
        out_shape=jax.ShapeDtypeStruct((batch, HEAD_PAIRS, seq, PAIR_W), BF16),
        scratch_shapes=[
            pltpu.VMEM((nb, PAIR_W), F32),
            pltpu.VMEM((4 * nb, PAIR_W), BF16),
            pltpu.VMEM((n_chains, MOBA_BLOCK, PAIR_W), BF16),
            pltpu.VMEM((nb, 2, nb, MOBA_BLOCK), F32),
            pltpu.VMEM((n_chains, V_ROWS, MOBA_BLOCK), F32),
            pltpu.VMEM((2, n_chains, MOBA_BLOCK, MOBA_BLOCK), F32),
            pltpu.VMEM((2, n_chains, 1, MOBA_BLOCK), F32),
            pltpu.VMEM((2, n_chains, MOBA_BLOCK, MOBA_BLOCK), BF16),
            pltpu.VMEM((2, n_chains, 1, MOBA_BLOCK), F32),
        ],
        compiler_params=pltpu.CompilerParams(dimension_semantics=("arbitrary", "arbitrary", "arbitrary"),
                                             vmem_limit_bytes=VMEM_LIMIT),
        name="moba_attention",
    )(q, k, vt, bias_t)


def _post_kernel(x_ref, u_ref, vn_ref, sga_ref, sgb_ref, yb_ref, p_ref,
                 ws_ref, bs_ref, wo_ref, gffn_ref, w1_ref, w2_ref, gple_ref, wpg_ref, wpp_ref, gfin_ref,
                 o_ref, merged_ref, *, tm, final_norm):
    row = lax.broadcasted_iota(jnp.int32, (SGU_CHUNK, SGU_CHUNK), 0)
    col = lax.broadcasted_iota(jnp.int32, (SGU_CHUNK, SGU_CHUNK), 1)
    for g in range(SGU_GROUPS):
        w = jnp.where(row >= col, ws_ref[g], 0.0).astype(BF16)
        b = bs_ref[g]
        cols = slice(g * LANES, (g + 1) * LANES)
        for t in range(tm // SGU_CHUNK):
            rows = slice(t * SGU_CHUNK, (t + 1) * SGU_CHUNK)
            mixed = _dot(w, vn_ref[rows, cols]) + b
            y_a = u_ref[rows, cols].astype(F32) * mixed
            y_b = yb_ref[0, g, rows, :].astype(F32)
            merged = sga_ref[rows, cols].astype(F32) * y_a + sgb_ref[rows, cols].astype(F32) * y_b
            merged_ref[rows, cols] = merged.astype(BF16)

    x1 = x_ref[...] + _dot(merged_ref[...], wo_ref[...])

    h = _rms(x1, gffn_ref[...]).astype(BF16)
    x2 = x1
    for c in range(D_FF // D_MODEL):
        cs = slice(c * D_MODEL, (c + 1) * D_MODEL)
        a = jnp.square(jnp.maximum(_dot(h, w1_ref[:, cs]), 0.0)).astype(BF16)
        x2 = x2 + _dot(a, w2_ref[cs, :])

    gate = jax.nn.sigmoid(_dot(_rms(x2, gple_ref[...]).astype(BF16), wpg_ref[...]))
    x3 = x2 + gate * _dot(p_ref[...].astype(BF16), wpp_ref[...])
    o_ref[...] = _rms(x3, gfin_ref[...]) if final_norm else x3


def _post(x2, u, vn, sga, sgb, yb, p2, ws, bs, wo, gffn, w1, w2, gple, wpg, wpp, gfin, *, seq, tm, final_norm):
    n = x2.shape[0]
    tiles_per_seq = seq // tm
    row = lambda r: (r, 0)
    resident = functools.partial(pl.BlockSpec, pipeline_mode=pl.Buffered(1))
    c2 = lambda r: (0, 0)
    c3 = lambda r: (0, 0, 0)
    tokb = pl.BlockSpec((tm, D_MODEL), row)
    return pl.pallas_call(
        functools.partial(_post_kernel, tm=tm, final_norm=final_norm),
        grid=(n // tm,),
        in_specs=[
            tokb, tokb, tokb, tokb, tokb,
            pl.BlockSpec((1, HEAD_PAIRS, tm, PAIR_W), lambda r: (r // tiles_per_seq, 0, r % tiles_per_seq, 0)),
            pl.BlockSpec((tm, PLE_DIM), row),
            resident((SGU_GROUPS, SGU_CHUNK, SGU_CHUNK), c3),
            resident((SGU_GROUPS, SGU_CHUNK, 1), c3),
            resident((D_MODEL, D_MODEL), c2),
            resident((1, D_MODEL), c2),
            resident((D_MODEL, D_FF), c2),
            resident((D_FF, D_MODEL), c2),
            resident((1, D_MODEL), c2),
            resident((D_MODEL, D_MODEL), c2),
            resident((PLE_DIM, D_MODEL), c2),
            resident((1, D_MODEL), c2),
        ],
        out_specs=pl.BlockSpec((tm, D_MODEL), row),
        out_shape=jax.ShapeDtypeStruct((n, D_MODEL), F32),
        scratch_shapes=[pltpu.VMEM((tm, D_MODEL), BF16)],
        compiler_params=pltpu.CompilerParams(dimension_semantics=("arbitrary",),
                                             vmem_limit_bytes=VMEM_LIMIT),
        name="post",
    )(x2, u, vn, sga, sgb, yb, p2, ws, bs, wo, gffn, w1, w2, gple, wpg, wpp, gfin)


def kernel(x, p, norm_mix_g, w_in, w_sgu_spatial, b_sgu_spatial, ln_v_g, ln_v_b, rel_bias, w_out, norm_ffn_g,
           w_ff1, w_ff2, norm_ple_g, w_ple_gate, w_ple_proj, norm_final_g):
    batch, seq, _ = x.shape
    depth = w_in.shape[0]
    n = batch * seq
    assert seq % (Q_BLOCKS * MOBA_BLOCK) == 0 and seq // MOBA_BLOCK >= MOBA_TOPK
    tm_in = 512
    tm_post = 256
    vec = lambda g: g.reshape(1, D_MODEL)

    bias_t = _bias_tiles(rel_bias)
    x2 = x.reshape(n, D_MODEL)
    for i in range(depth):
        wi = w_in[i].astype(BF16)
        wa = wi[:, :4 * D_MODEL]
        wvt = wi[:, 4 * D_MODEL:5 * D_MODEL].T
        wg = wi[:, 5 * D_MODEL:]
        u, vn, q, k, vt, sga, sgb = _in_proj(
            x2, vec(norm_mix_g[i]), wa, wvt, wg, vec(ln_v_g[i]), vec(ln_v_b[i]),
            batch=batch, seq=seq, tm=tm_in)
        yb = _attention(q, k, vt, bias_t)
        x2 = _post(
            x2, u, vn, sga, sgb, yb, p[i].reshape(n, PLE_DIM),
            w_sgu_spatial[i], b_sgu_spatial[i].reshape(SGU_GROUPS, SGU_CHUNK, 1),
            w_out[i].astype(BF16), vec(norm_ffn_g[i]), w_ff1[i].astype(BF16), w_ff2[i].astype(BF16),
            vec(norm_ple_g[i]), w_ple_gate[i].astype(BF16), w_ple_proj[i].astype(BF16),
            vec(norm_final_g), seq=seq, tm=tm_post, final_norm=(i == depth - 1))
    return x2.reshape(batch, seq, D_MODEL)
```

```python
import functools
import math

import jax
import jax.numpy as jnp
from jax import lax
from jax.experimental import pallas as pl
from jax.experimental.pallas import tpu as pltpu

D_MODEL = 1024
PLE_DIM = 256
SGU_CHUNK = 128
SGU_GROUPS = 8
ATT_HEADS = 16
HEAD_DIM = 64
MOBA_BLOCK = 256
MOBA_TOPK = 3
REL_BUCKETS = 32
REL_MAX_DIST = 1024
D_FF = 4 * D_MODEL
EPS = 1e-6

LANES = 128
BF16_ROWS = 16
HEAD_PAIRS = ATT_HEADS // 2
PAIR_W = 2 * HEAD_DIM
V_ROWS = HEAD_DIM + BF16_ROWS
Q_BLOCKS = 4
FAR_TRIP_PAIRS = (2, 1)
SEL_CHUNK = 8
LAZY_MAX_EXCESS = 16.0
BIG = 1e30
VMEM_LIMIT = 60 * 1024 * 1024
LOG2E = math.log2(math.e)

F32 = jnp.float32
BF16 = jnp.bfloat16


def _bucket_thresholds():
    max_exact = REL_BUCKETS // 2
    n_log = REL_BUCKETS - max_exact
    ratio = REL_MAX_DIST // max_exact
    out = []
    for k in range(1, n_log):
        d = max_exact
        target = (max_exact ** n_log) * (ratio ** k)
        while d ** n_log < target:
            d += 1
        out.append(d)
    return tuple(out)


BUCKET_THRESHOLDS = _bucket_thresholds()
NEAR_TILES = -(-(BUCKET_THRESHOLDS[-1] + MOBA_BLOCK - 1) // MOBA_BLOCK)
assert NEAR_TILES == 5


def _dot(a, b):
    return jnp.dot(a, b, preferred_element_type=F32)


def _dot_nt(a, b):
    return lax.dot_general(a, b, (((1,), (1,)), ((), ())), preferred_element_type=F32)


def _rms(x, g):
    return x * lax.rsqrt(jnp.mean(x * x, axis=-1, keepdims=True) + EPS) * g


def _in_proj_kernel(x_ref, g_ref, wa_ref, wvt_ref, wg_ref, lng_ref, lnb_ref,
                    u_ref, vn_ref, q_ref, k_ref, vt_ref, sga_ref, sgb_ref, *, tm):
    h = _rms(x_ref[...], g_ref[...]).astype(BF16)

    u_ref[...] = jax.nn.gelu(_dot(h, wa_ref[:, 0:D_MODEL])).astype(BF16)

    gv = jax.nn.gelu(_dot(h, wa_ref[:, D_MODEL:2 * D_MODEL]))
    mu = jnp.mean(gv, axis=-1, keepdims=True)
    gc = gv - mu
    vn = gc * lax.rsqrt(jnp.mean(gc * gc, axis=-1, keepdims=True) + EPS)
    vn_ref[...] = (vn * lng_ref[...] + lnb_ref[...]).astype(BF16)

    q = _dot(h, wa_ref[:, 2 * D_MODEL:3 * D_MODEL]) * (HEAD_DIM ** -0.5 * LOG2E)
    k = _dot(h, wa_ref[:, 3 * D_MODEL:4 * D_MODEL])
    for p in range(HEAD_PAIRS):
        q_ref[0, p] = q[:, p * PAIR_W:(p + 1) * PAIR_W].astype(BF16)
        k_ref[0, p] = k[:, p * PAIR_W:(p + 1) * PAIR_W].astype(BF16)

    vt = _dot_nt(wvt_ref[...], h).astype(BF16)
    ones = jnp.ones((BF16_ROWS, MOBA_BLOCK), BF16)
    for p in range(HEAD_PAIRS):
        for jb in range(tm // MOBA_BLOCK):
            for a in range(2):
                r0 = p * PAIR_W + a * HEAD_DIM
                vt_ref[0, p, jb, a, 0:HEAD_DIM, :] = vt[r0:r0 + HEAD_DIM, jb * MOBA_BLOCK:(jb + 1) * MOBA_BLOCK]
                vt_ref[0, p, jb, a, HEAD_DIM:V_ROWS, :] = ones

    sga_ref[...] = jax.nn.sigmoid(_dot(h, wg_ref[:, 0:D_MODEL])).astype(BF16)
    sgb_ref[...] = jax.nn.sigmoid(_dot(h, wg_ref[:, D_MODEL:2 * D_MODEL])).astype(BF16)


def _in_proj(x2, g, wa, wvt, wg, lng, lnb, *, batch, seq, tm):
    n = x2.shape[0]
    tiles_per_seq = seq // tm
    nb = seq // MOBA_BLOCK
    row = lambda r: (r, 0)
    const = lambda r: (0, 0)
    resident = functools.partial(pl.BlockSpec, pipeline_mode=pl.Buffered(1))
    tok = jax.ShapeDtypeStruct((n, D_MODEL), BF16)
    pair_major = jax.ShapeDtypeStruct((batch, HEAD_PAIRS, seq, PAIR_W), BF16)
    return pl.pallas_call(
        functools.partial(_in_proj_kernel, tm=tm),
        grid=(n // tm,),
        in_specs=[
            pl.BlockSpec((tm, D_MODEL), row),
            resident((1, D_MODEL), const),
            resident((D_MODEL, 4 * D_MODEL), const),
            resident((D_MODEL, D_MODEL), const),
            resident((D_MODEL, 2 * D_MODEL), const),
            resident((1, D_MODEL), const),
            resident((1, D_MODEL), const),
        ],
        out_specs=[
            pl.BlockSpec((tm, D_MODEL), row),
            pl.BlockSpec((tm, D_MODEL), row),
            pl.BlockSpec((1, HEAD_PAIRS, tm, PAIR_W), lambda r: (r // tiles_per_seq, 0, r % tiles_per_seq, 0)),
            pl.BlockSpec((1, HEAD_PAIRS, tm, PAIR_W), lambda r: (r // tiles_per_seq, 0, r % tiles_per_seq, 0)),
            pl.BlockSpec((1, HEAD_PAIRS, tm // MOBA_BLOCK, 2, V_ROWS, MOBA_BLOCK),
                         lambda r: (r // tiles_per_seq, 0, r % tiles_per_seq, 0, 0, 0)),
            pl.BlockSpec((tm, D_MODEL), row),
            pl.BlockSpec((tm, D_MODEL), row),
        ],
        out_shape=[tok, tok, pair_major, pair_major,
                   jax.ShapeDtypeStruct((batch, HEAD_PAIRS, nb, 2, V_ROWS, MOBA_BLOCK), BF16),
                   tok, tok],
        compiler_params=pltpu.CompilerParams(dimension_semantics=("arbitrary",),
                                             vmem_limit_bytes=VMEM_LIMIT),
        name="in_proj",
    )(x2, g, wa, wvt, wg, lng, lnb)


def _bias_tiles_kernel(tab_ref, o_ref):
    h = pl.program_id(0)
    key = lax.broadcasted_iota(jnp.int32, (MOBA_BLOCK, MOBA_BLOCK), 0)
    qry = lax.broadcasted_iota(jnp.int32, (MOBA_BLOCK, MOBA_BLOCK), 1)
    max_exact = REL_BUCKETS // 2
    last = tab_ref[REL_BUCKETS - 1, h]
    for t in range(NEAR_TILES):
        dist = t * MOBA_BLOCK + qry - key
        n = jnp.maximum(dist, 0)
        val = jnp.full((MOBA_BLOCK, MOBA_BLOCK), tab_ref[0, h], F32)
        for b in range(1, max_exact + 1):
            val = jnp.where(n >= b, tab_ref[b, h], val)
        for kk, thr in enumerate(BUCKET_THRESHOLDS):
            val = jnp.where(n >= thr, tab_ref[max_exact + 1 + kk, h], val)
        val = (val - last) * LOG2E
        if t == 0:
            val = jnp.where(dist >= 0, val, -BIG)
        o_ref[0, t] = val
    o_ref[0, NEAR_TILES] = jnp.zeros((MOBA_BLOCK, MOBA_BLOCK), F32)


def _bias_tiles(rel_bias):
    return pl.pallas_call(
        _bias_tiles_kernel,
        grid=(ATT_HEADS,),
        in_specs=[pl.BlockSpec(memory_space=pltpu.SMEM)],
        out_specs=pl.BlockSpec((1, NEAR_TILES + 1, MOBA_BLOCK, MOBA_BLOCK), lambda h: (h, 0, 0, 0)),
        out_shape=jax.ShapeDtypeStruct((ATT_HEADS, NEAR_TILES + 1, MOBA_BLOCK, MOBA_BLOCK), F32),
        compiler_params=pltpu.CompilerParams(dimension_semantics=("arbitrary",)),
        name="bias_tiles",
    )(rel_bias)


def _attn_kernel(q_ref, k_ref, vt_ref, bias_ref, o_ref,
                 km_ref, kms_ref, qa_ref, sel_ref, acc_ref, s_ref, mt_ref, p_ref, alpha_ref, *, nb):
    step = pl.program_id(2)
    j_last = step * Q_BLOCKS + (Q_BLOCKS - 1)
    chains = [(h, a) for h in range(Q_BLOCKS) for a in range(2)]
    lane = lax.broadcasted_iota(jnp.int32, (1, PAIR_W), 1)

    @pl.when(step == 0)
    def _():
        def body(j, c):
            kb = k_ref[0, 0, pl.ds(pl.multiple_of(j * MOBA_BLOCK, MOBA_BLOCK), MOBA_BLOCK), :]
            km_ref[pl.ds(j, 1), :] = jnp.sum(kb.astype(F32), axis=0, keepdims=True) * (1.0 / MOBA_BLOCK)
            return c
        lax.fori_loop(0, nb, body, 0)
        km = km_ref[...]
        km_hi = km.astype(BF16)
        km_lo = (km - km_hi.astype(F32)).astype(BF16)
        for a in range(2):
            in_head = (lane >= a * HEAD_DIM) & (lane < (a + 1) * HEAD_DIM)
            kms_ref[(2 * a) * nb:(2 * a + 1) * nb, :] = jnp.where(in_head, km_hi, jnp.zeros_like(km_hi))
            kms_ref[(2 * a + 1) * nb:(2 * a + 2) * nb, :] = jnp.where(in_head, km_lo, jnp.zeros_like(km_lo))

        n_q = SEL_CHUNK * MOBA_BLOCK
        def select_chunk(ci, c):
            q = q_ref[0, 0, pl.ds(pl.multiple_of(ci * n_q, n_q), n_q), :]
            g = _dot_nt(kms_ref[...], q)
            gate = jnp.concatenate([g[(2 * a) * nb:(2 * a + 1) * nb] + g[(2 * a + 1) * nb:(2 * a + 2) * nb]
                                    for a in range(2)], axis=1)
            pos = lax.broadcasted_iota(jnp.int32, (1, n_q), 1)
            own = ci * SEL_CHUNK + jnp.right_shift(pos, MOBA_BLOCK.bit_length() - 1)
            own = jnp.concatenate([own, own], axis=1)
            blk = lax.broadcasted_iota(jnp.int32, gate.shape, 0)
            gate = jnp.where(blk < own, gate, -jnp.inf)
            sel = blk == own
            for _ in range(MOBA_TOPK):
                top = jnp.max(gate, axis=0, keepdims=True)
                first = jnp.min(jnp.where(gate == top, blk, nb), axis=0, keepdims=True)
                pick = (blk == first) & (top > -jnp.inf)
                sel = sel | pick
                gate = jnp.where(pick, -jnp.inf, gate)
            sel_f = jnp.where(sel, 1.0, 0.0)
            for a in range(2):
                for b in range(SEL_CHUNK):
                    lo = a * n_q + b * MOBA_BLOCK
                    sel_ref[ci * SEL_CHUNK + b, a] = sel_f[:, lo:lo + MOBA_BLOCK]
            return c
        lax.fori_loop(0, nb // SEL_CHUNK, select_chunk, 0)

    for c, (h, a) in enumerate(chains):
        q = q_ref[0, 0, pl.ds(pl.multiple_of((step * Q_BLOCKS + h) * MOBA_BLOCK, MOBA_BLOCK), MOBA_BLOCK), :]
        qa_ref[c] = jnp.where((lane >= a * HEAD_DIM) & (lane < (a + 1) * HEAD_DIM), q, jnp.zeros_like(q))

    def block_of(t):
        return jnp.maximum(j_last - t, 0)

    def key_block(j):
        return k_ref[0, 0, pl.ds(pl.multiple_of(j * MOBA_BLOCK, MOBA_BLOCK), MOBA_BLOCK), :]

    def scores(t, c, kj, near):
        h, a = chains[c]
        s = _dot_nt(kj, qa_ref[c])
        offset = t - (Q_BLOCKS - 1 - h)
        if isinstance(t, int):
            if offset < NEAR_TILES:
                s = s + bias_ref[a, max(offset, 0)]
        elif near:
            s = s + bias_ref[a, jnp.clip(offset, 0, NEAR_TILES)]
        return s

    def visits(t, c):
        return not isinstance(t, int) or t - (Q_BLOCKS - 1 - chains[c][0]) >= 0

    def chosen_row(t, c, j):
        h, a = chains[c]
        return sel_ref[step * Q_BLOCKS + h, a, pl.ds(j, 1), :] > jnp.where(t <= j_last, 0.5, 2.0)

    def score_stage(t, slot, near):
        kj = key_block(block_of(t))
        for c in range(len(chains)):
            s = scores(t, c, kj, near)
            s_ref[slot, c] = s
            mt_ref[slot, c] = jnp.max(s, axis=0, keepdims=True)

    def softmax_stage(t, slot, m_prev):
        j = block_of(t)
        m_next, alphas, pvs = [], [], []
        for c, (h, a) in enumerate(chains):
            chosen = chosen_row(t, c, j)
            m_new = jnp.where(chosen, jnp.maximum(m_prev[c], mt_ref[slot, c]), m_prev[c])
            alphas.append(jnp.exp2(m_prev[c] - m_new))
            p = jnp.exp2(s_ref[slot, c] - jnp.where(chosen, m_new, BIG)).astype(BF16)
            pvs.append(_dot(vt_ref[0, 0, j, a], p))
            m_next.append(m_new)
        for c in range(len(chains)):
            acc_ref[c] = alphas[c] * acc_ref[c] + pvs[c]
        return tuple(m_next)

    def exact_pair(tt, m):
        t = 2 * tt
        score_stage(t + 1, 1, True)
        m = softmax_stage(t, 0, m)
        score_stage(t + 2, 0, True)
        m = softmax_stage(t + 1, 1, m)
        return m

    def lazy_stage(carry, near, score=None, pv=None):
        r, excess = list(carry[0]), list(carry[1])
        if score is not None:
            t_s, slot_s = score
            j_s = block_of(t_s)
            kj = key_block(j_s)
        if pv is not None:
            t_p, slot_p = pv
            j_p = block_of(t_p)
        for c0 in range(0, len(chains), 2):
            if score is not None:
                for c in (c0, c0 + 1):
                    if not visits(t_s, c):
                        continue
                    s = scores(t_s, c, kj, near)
                    chosen = chosen_row(t_s, c, j_s)
                    tile_max = jnp.max(s, axis=0, keepdims=True)
                    p_ref[slot_s, c] = jnp.exp2(s - jnp.where(chosen, r[c], BIG)).astype(BF16)
                    r_new = jnp.where(chosen, jnp.maximum(r[c], tile_max), r[c])
                    alpha_ref[slot_s, c] = jnp.exp2(r[c] - r_new)
                    excess[c] = jnp.maximum(excess[c], jnp.where(chosen, tile_max - r[c], -BIG))
                    r[c] = r_new
            if pv is not None:
                live = [c for c in (c0, c0 + 1) if visits(t_p, c)]
                pvs = [_dot(vt_ref[0, 0, j_p, chains[c][1]], p_ref[slot_p, c]) for c in live]
                for c, prod in zip(live, pvs):
                    acc_ref[c] = (acc_ref[c] + prod) * alpha_ref[slot_p, c]
        return tuple(r), tuple(excess)

    def lazy_pairs(it, carry, *, near, first_pair, pairs_per_trip, single_trip=False):
        if single_trip:
            t = 2 * first_pair
            carry = lazy_stage(carry, near, score=(t, 0))
        else:
            t = 2 * (first_pair + it * pairs_per_trip)
        for _ in range(pairs_per_trip):
            carry = lazy_stage(carry, near, score=(t + 1, 1), pv=(t, 0))
            carry = lazy_stage(carry, near, score=(t + 2, 0), pv=(t + 1, 1))
            t = t + 2
        return carry

    def finalize():
        for h in range(Q_BLOCKS):
            outs = []
            for a in range(2):
                acc = acc_ref[2 * h + a]
                outs.append(acc[0:HEAD_DIM] * (1.0 / acc[HEAD_DIM:HEAD_DIM + 1]))
            o_ref[0, 0, h * MOBA_BLOCK:(h + 1) * MOBA_BLOCK, :] = jnp.concatenate(outs, axis=0).T.astype(BF16)

    near_pairs = (NEAR_TILES + Q_BLOCKS) // 2
    n_pairs = (j_last + 2) // 2
    far_pairs = jnp.maximum(n_pairs - near_pairs, 0)
    lowest = (jnp.full((1, MOBA_BLOCK), -BIG, F32),) * len(chains)

    ones_rows = jnp.ones((BF16_ROWS, PAIR_W), BF16)
    seeds = []
    for c, (h, a) in enumerate(chains):
        own_scores = _dot_nt(ones_rows, qa_ref[c] * key_block(step * Q_BLOCKS + h))
        seeds.append(own_scores[0:1] + bias_ref[a, 0, 0:1, 0:1])

    acc_ref[...] = jnp.zeros_like(acc_ref)
    carry = (tuple(seeds), lowest)
    carry = lax.fori_loop(0, jnp.where(n_pairs >= near_pairs, 1, 0),
                          functools.partial(lazy_pairs, near=True, first_pair=0, pairs_per_trip=near_pairs,
                                            single_trip=True), carry)

    def few_pairs(it, carry):
        carry = lazy_stage(carry, True, score=(2 * it, 0))
        return lax.fori_loop(0, n_pairs, functools.partial(lazy_pairs, near=True, first_pair=0, pairs_per_trip=1), carry)
    carry = lax.fori_loop(0, jnp.where(n_pairs >= near_pairs, 0, 1), few_pairs, carry)
    first, left = near_pairs, far_pairs
    for pairs_per_trip in FAR_TRIP_PAIRS:
        trips = left // pairs_per_trip
        carry = lax.fori_loop(0, trips, functools.partial(lazy_pairs, near=False, first_pair=first,
                                                          pairs_per_trip=pairs_per_trip), carry)
        first, left = first + trips * pairs_per_trip, left - trips * pairs_per_trip
    finalize()

    worst = functools.reduce(jnp.maximum, carry[1])
    @pl.when(jnp.max(worst) > LAZY_MAX_EXCESS)
    def _():
        acc_ref[...] = jnp.zeros_like(acc_ref)
        score_stage(0, 0, True)
        lax.fori_loop(0, n_pairs, exact_pair, lowest)
        finalize()


def _attention(q, k, vt, bias_t):
    batch, _, seq, _ = q.shape
    nb = seq // MOBA_BLOCK
    tq = Q_BLOCKS * MOBA_BLOCK
    n_chains = 2 * Q_BLOCKS
    return pl.pallas_call(
        functools.partial(_attn_kernel, nb=nb),
        grid=(batch, HEAD_PAIRS, seq // tq),
        in_specs=[
            pl.BlockSpec((1, 1, seq, PAIR_W), lambda b, p, i: (b, p, 0, 0)),
            pl.BlockSpec((1, 1, seq, PAIR_W), lambda b, p, i: (b, p, 0, 0)),
            pl.BlockSpec((1, 1, nb, 2, V_ROWS, MOBA_BLOCK), lambda b, p, i: (b, p, 0, 0, 0, 0)),
            pl.BlockSpec((2, NEAR_TILES + 1, MOBA_BLOCK, MOBA_BLOCK), lambda b, p, i: (p, 0, 0, 0)),
        ],
        out_specs=pl.BlockSpec((1, 1, tq, PAIR_W), lambda b, p, i: (b, p, i, 0)),
        out_shape=jax.ShapeDtypeStruct((batch, HEAD_PAIRS, seq, PAIR_W), BF16),
        scratch_shapes=[
            pltpu.VMEM((nb, PAIR_W), F32),
            pltpu.VMEM((4 * nb, PAIR_W), BF16),
            pltpu.VMEM((n_chains, MOBA_BLOCK, PAIR_W), BF16),
            pltpu.VMEM((nb, 2, nb, MOBA_BLOCK), F32),
            pltpu.VMEM((n_chains, V_ROWS, MOBA_BLOCK), F32),
            pltpu.VMEM((2, n_chains, MOBA_BLOCK, MOBA_BLOCK), F32),
            pltpu.VMEM((2, n_chains, 1, MOBA_BLOCK), F32),
            pltpu.VMEM((2, n_chains, MOBA_BLOCK, MOBA_BLOCK), BF16),
            pltpu.VMEM((2, n_chains, 1, MOBA_BLOCK), F32),
        ],
        compiler_params=pltpu.CompilerParams(dimension_semantics=("arbitrary", "arbitrary", "arbitrary"),
                                             vmem_limit_bytes=VMEM_LIMIT),
        name="moba_attention",
    )(q, k, vt, bias_t)


def _post_kernel(x_ref, u_ref, vn_ref, sga_ref, sgb_ref, yb_ref, p_ref,
                 ws_ref, bs_ref, wo_ref, gffn_ref, w1_ref, w2_ref, gple_ref, wpg_ref, wpp_ref, gfin_ref,
                 o_ref, merged_ref, *, tm, final_norm):
    row = lax.broadcasted_iota(jnp.int32, (SGU_CHUNK, SGU_CHUNK), 0)
    col = lax.broadcasted_iota(jnp.int32, (SGU_CHUNK, SGU_CHUNK), 1)
    for g in range(SGU_GROUPS):
        w = jnp.where(row >= col, ws_ref[g], 0.0).astype(BF16)
        b = bs_ref[g]
        cols = slice(g * LANES, (g + 1) * LANES)
        for t in range(tm // SGU_CHUNK):
            rows = slice(t * SGU_CHUNK, (t + 1) * SGU_CHUNK)
            mixed = _dot(w, vn_ref[rows, cols]) + b
            y_a = u_ref[rows, cols].astype(F32) * mixed
            y_b = yb_ref[0, g, rows, :].astype(F32)
            merged = sga_ref[rows, cols].astype(F32) * y_a + sgb_ref[rows, cols].astype(F32) * y_b
            merged_ref[rows, cols] = merged.astype(BF16)

    x1 = x_ref[...] + _dot(merged_ref[...], wo_ref[...])

    h = _rms(x1, gffn_ref[...]).astype(BF16)
    x2 = x1
    for c in range(D_FF // D_MODEL):
        cs = slice(c * D_MODEL, (c + 1) * D_MODEL)
        a = jnp.square(jnp.maximum(_dot(h, w1_ref[:, cs]), 0.0)).astype(BF16)
        x2 = x2 + _dot(a, w2_ref[cs, :])

    gate = jax.nn.sigmoid(_dot(_rms(x2, gple_ref[...]).astype(BF16), wpg_ref[...]))
    x3 = x2 + gate * _dot(p_ref[...].astype(BF16), wpp_ref[...])
    o_ref[...] = _rms(x3, gfin_ref[...]) if final_norm else x3


def _post(x2, u, vn, sga, sgb, yb, p2, ws, bs, wo, gffn, w1, w2, gple, wpg, wpp, gfin, *, seq, tm, final_norm):
    n = x2.shape[0]
    tiles_per_seq = seq // tm
    row = lambda r: (r, 0)
    resident = functools.partial(pl.BlockSpec, pipeline_mode=pl.Buffered(1))
    c2 = lambda r: (0, 0)
    c3 = lambda r: (0, 0, 0)
    tokb = pl.BlockSpec((tm, D_MODEL), row)
    return pl.pallas_call(
        functools.partial(_post_kernel, tm=tm, final_norm=final_norm),
        grid=(n // tm,),
        in_specs=[
            tokb, tokb, tokb, tokb, tokb,
            pl.BlockSpec((1, HEAD_PAIRS, tm, PAIR_W), lambda r: (r // tiles_per_seq, 0, r % tiles_per_seq, 0)),
            pl.BlockSpec((tm, PLE_DIM), row),
            resident((SGU_GROUPS, SGU_CHUNK, SGU_CHUNK), c3),
            resident((SGU_GROUPS, SGU_CHUNK, 1), c3),
            resident((D_MODEL, D_MODEL), c2),
            resident((1, D_MODEL), c2),
            resident((D_MODEL, D_FF), c2),
            resident((D_FF, D_MODEL), c2),
            resident((1, D_MODEL), c2),
            resident((D_MODEL, D_MODEL), c2),
            resident((PLE_DIM, D_MODEL), c2),
            resident((1, D_MODEL), c2),
        ],
        out_specs=pl.BlockSpec((tm, D_MODEL), row),
        out_shape=jax.ShapeDtypeStruct((n, D_MODEL), F32),
        scratch_shapes=[pltpu.VMEM((tm, D_MODEL), BF16)],
        compiler_params=pltpu.CompilerParams(dimension_semantics=("arbitrary",),
                                             vmem_limit_bytes=VMEM_LIMIT),
        name="post",
    )(x2, u, vn, sga, sgb, yb, p2, ws, bs, wo, gffn, w1, w2, gple, wpg, wpp, gfin)


def kernel(x, p, norm_mix_g, w_in, w_sgu_spatial, b_sgu_spatial, ln_v_g, ln_v_b, rel_bias, w_out, norm_ffn_g,
           w_ff1, w_ff2, norm_ple_g, w_ple_gate, w_ple_proj, norm_final_g):
    batch, seq, _ = x.shape
    depth = w_in.shape[0]
    n = batch * seq
    assert seq % (Q_BLOCKS * MOBA_BLOCK) == 0 and seq // MOBA_BLOCK >= MOBA_TOPK
    tm_in = 512
    tm_post = 512
    vec = lambda g: g.reshape(1, D_MODEL)

    bias_t = _bias_tiles(rel_bias)
    x2 = x.reshape(n, D_MODEL)
    for i in range(depth):
        wi = w_in[i].astype(BF16)
        wa = wi[:, :4 * D_MODEL]
        wvt = wi[:, 4 * D_MODEL:5 * D_MODEL].T
        wg = wi[:, 5 * D_MODEL:]
        u, vn, q, k, vt, sga, sgb = _in_proj(
            x2, vec(norm_mix_g[i]), wa, wvt, wg, vec(ln_v_g[i]), vec(ln_v_b[i]),
            batch=batch, seq=seq, tm=tm_in)
        yb = _attention(q, k, vt, bias_t)
        x2 = _post(
            x2, u, vn, sga, sgb, yb, p[i].reshape(n, PLE_DIM),
            w_sgu_spatial[i], b_sgu_spatial[i].reshape(SGU_GROUPS, SGU_CHUNK, 1),
            w_out[i].astype(BF16), vec(norm_ffn_g[i]), w_ff1[i].astype(BF16), w_ff2[i].astype(BF16),
            vec(norm_ple_g[i]), w_ple_gate[i].astype(BF16), w_ple_proj[i].astype(BF16),
            vec(norm_final_g), seq=seq, tm=tm_post, final_norm=(i == depth - 1))
    return x2.reshape(batch, seq, D_MODEL)
```

```python
import functools
import math

import jax
import jax.numpy as jnp
from jax import lax
from jax.experimental import pallas as pl
from jax.experimental.pallas import tpu as pltpu

D_MODEL = 1024
PLE_DIM = 256
SGU_CHUNK = 128
SGU_GROUPS = 8
ATT_HEADS = 16
HEAD_DIM = 64
MOBA_BLOCK = 256
MOBA_TOPK = 3
REL_BUCKETS = 32
REL_MAX_DIST = 1024
D_FF = 4 * D_MODEL
EPS = 1e-6

LANES = 128
BF16_ROWS = 16
HEAD_PAIRS = ATT_HEADS // 2
PAIR_W = 2 * HEAD_DIM
V_ROWS = HEAD_DIM + BF16_ROWS
Q_BLOCKS = 4
FAR_TRIP_PAIRS = (2, 1)
SEL_CHUNK = 8
LAZY_MAX_EXCESS = 16.0
BIG = 1e30
VMEM_LIMIT = 60 * 1024 * 1024
LOG2E = math.log2(math.e)

F32 = jnp.float32
BF16 = jnp.bfloat16


def _bucket_thresholds():
    max_exact = REL_BUCKETS // 2
    n_log = REL_BUCKETS - max_exact
    ratio = REL_MAX_DIST // max_exact
    out = []
    for k in range(1, n_log):
        d = max_exact
        target = (max_exact ** n_log) * (ratio ** k)
        while d ** n_log < target:
            d += 1
        out.append(d)
    return tuple(out)


BUCKET_THRESHOLDS = _bucket_thresholds()
NEAR_TILES = -(-(BUCKET_THRESHOLDS[-1] + MOBA_BLOCK - 1) // MOBA_BLOCK)
assert NEAR_TILES == 5


def _dot(a, b):
    return jnp.dot(a, b, preferred_element_type=F32)


def _dot_nt(a, b):
    return lax.dot_general(a, b, (((1,), (1,)), ((), ())), preferred_element_type=F32)


def _rms(x, g):
    return x * lax.rsqrt(jnp.mean(x * x, axis=-1, keepdims=True) + EPS) * g


def _in_proj_kernel(x_ref, g_ref, wa_ref, wvt_ref, wg_ref, lng_ref, lnb_ref,
                    u_ref, vn_ref, q_ref, k_ref, vt_ref, sga_ref, sgb_ref, *, tm):
    h = _rms(x_ref[...], g_ref[...]).astype(BF16)

    u_ref[...] = jax.nn.gelu(_dot(h, wa_ref[:, 0:D_MODEL])).astype(BF16)

    gv = jax.nn.gelu(_dot(h, wa_ref[:, D_MODEL:2 * D_MODEL]))
    mu = jnp.mean(gv, axis=-1, keepdims=True)
    gc = gv - mu
    vn = gc * lax.rsqrt(jnp.mean(gc * gc, axis=-1, keepdims=True) + EPS)
    vn_ref[...] = (vn * lng_ref[...] + lnb_ref[...]).astype(BF16)

    q = _dot(h, wa_ref[:, 2 * D_MODEL:3 * D_MODEL]) * (HEAD_DIM ** -0.5 * LOG2E)
    k = _dot(h, wa_ref[:, 3 * D_MODEL:4 * D_MODEL])
    for p in range(HEAD_PAIRS):
        q_ref[0, p] = q[:, p * PAIR_W:(p + 1) * PAIR_W].astype(BF16)
        k_ref[0, p] = k[:, p * PAIR_W:(p + 1) * PAIR_W].astype(BF16)

    vt = _dot_nt(wvt_ref[...], h).astype(BF16)
    ones = jnp.ones((BF16_ROWS, MOBA_BLOCK), BF16)
    for p in range(HEAD_PAIRS):
        for jb in range(tm // MOBA_BLOCK):
            for a in range(2):
                r0 = p * PAIR_W + a * HEAD_DIM
                vt_ref[0, p, jb, a, 0:HEAD_DIM, :] = vt[r0:r0 + HEAD_DIM, jb * MOBA_BLOCK:(jb + 1) * MOBA_BLOCK]
                vt_ref[0, p, jb, a, HEAD_DIM:V_ROWS, :] = ones

    sga_ref[...] = jax.nn.sigmoid(_dot(h, wg_ref[:, 0:D_MODEL])).astype(BF16)
    sgb_ref[...] = jax.nn.sigmoid(_dot(h, wg_ref[:, D_MODEL:2 * D_MODEL])).astype(BF16)


def _in_proj(x2, g, wa, wvt, wg, lng, lnb, *, batch, seq, tm):
    n = x2.shape[0]
    tiles_per_seq = seq // tm
    nb = seq // MOBA_BLOCK
    row = lambda r: (r, 0)
    const = lambda r: (0, 0)
    resident = functools.partial(pl.BlockSpec, pipeline_mode=pl.Buffered(1))
    tok = jax.ShapeDtypeStruct((n, D_MODEL), BF16)
    pair_major = jax.ShapeDtypeStruct((batch, HEAD_PAIRS, seq, PAIR_W), BF16)
    return pl.pallas_call(
        functools.partial(_in_proj_kernel, tm=tm),
        grid=(n // tm,),
        in_specs=[
            pl.BlockSpec((tm, D_MODEL), row),
            resident((1, D_MODEL), const),
            resident((D_MODEL, 4 * D_MODEL), const),
            resident((D_MODEL, D_MODEL), const),
            resident((D_MODEL, 2 * D_MODEL), const),
            resident((1, D_MODEL), const),
            resident((1, D_MODEL), const),
        ],
        out_specs=[
            pl.BlockSpec((tm, D_MODEL), row),
            pl.BlockSpec((tm, D_MODEL), row),
            pl.BlockSpec((1, HEAD_PAIRS, tm, PAIR_W), lambda r: (r // tiles_per_seq, 0, r % tiles_per_seq, 0)),
            pl.BlockSpec((1, HEAD_PAIRS, tm, PAIR_W), lambda r: (r // tiles_per_seq, 0, r % tiles_per_seq, 0)),
            pl.BlockSpec((1, HEAD_PAIRS, tm // MOBA_BLOCK, 2, V_ROWS, MOBA_BLOCK),
                         lambda r: (r // tiles_per_seq, 0, r % tiles_per_seq, 0, 0, 0)),
            pl.BlockSpec((tm, D_MODEL), row),
            pl.BlockSpec((tm, D_MODEL), row),
        ],
        out_shape=[tok, tok, pair_major, pair_major,
                   jax.ShapeDtypeStruct((batch, HEAD_PAIRS, nb, 2, V_ROWS, MOBA_BLOCK), BF16),
                   tok, tok],
        compiler_params=pltpu.CompilerParams(dimension_semantics=("arbitrary",),
                                             vmem_limit_bytes=VMEM_LIMIT),
        name="in_proj",
    )(x2, g, wa, wvt, wg, lng, lnb)


def _bias_tiles_kernel(tab_ref, o_ref):
    h = pl.program_id(0)
    key = lax.broadcasted_iota(jnp.int32, (MOBA_BLOCK, MOBA_BLOCK), 0)
    qry = lax.broadcasted_iota(jnp.int32, (MOBA_BLOCK, MOBA_BLOCK), 1)
    max_exact = REL_BUCKETS // 2
    last = tab_ref[REL_BUCKETS - 1, h]
    for t in range(NEAR_TILES):
        dist = t * MOBA_BLOCK + qry - key
        n = jnp.maximum(dist, 0)
        val = jnp.full((MOBA_BLOCK, MOBA_BLOCK), tab_ref[0, h], F32)
        for b in range(1, max_exact + 1):
            val = jnp.where(n >= b, tab_ref[b, h], val)
        for kk, thr in enumerate(BUCKET_THRESHOLDS):
            val = jnp.where(n >= thr, tab_ref[max_exact + 1 + kk, h], val)
        val = (val - last) * LOG2E
        if t == 0:
            val = jnp.where(dist >= 0, val, -BIG)
        o_ref[0, t] = val
    o_ref[0, NEAR_TILES] = jnp.zeros((MOBA_BLOCK, MOBA_BLOCK), F32)


def _bias_tiles(rel_bias):
    return pl.pallas_call(
        _bias_tiles_kernel,
        grid=(ATT_HEADS,),
        in_specs=[pl.BlockSpec(memory_space=pltpu.SMEM)],
        out_specs=pl.BlockSpec((1, NEAR_TILES + 1, MOBA_BLOCK, MOBA_BLOCK), lambda h: (h, 0, 0, 0)),
        out_shape=jax.ShapeDtypeStruct((ATT_HEADS, NEAR_TILES + 1, MOBA_BLOCK, MOBA_BLOCK), F32),
        compiler_params=pltpu.CompilerParams(dimension_semantics=("arbitrary",)),
        name="bias_tiles",
    )(rel_bias)


def _attn_kernel(q_ref, k_ref, vt_ref, bias_ref, o_ref,
                 km_ref, kms_ref, qa_ref, sel_ref, acc_ref, s_ref, mt_ref, p_ref, alpha_ref, *, nb):
    step = pl.program_id(2)
    j_last = step * Q_BLOCKS + (Q_BLOCKS - 1)
    chains = [(h, a) for h in range(Q_BLOCKS) for a in range(2)]
    lane = lax.broadcasted_iota(jnp.int32, (1, PAIR_W), 1)

    @pl.when(step == 0)
    def _():
        def body(j, c):
            kb = k_ref[0, 0, pl.ds(pl.multiple_of(j * MOBA_BLOCK, MOBA_BLOCK), MOBA_BLOCK), :]
            km_ref[pl.ds(j, 1), :] = jnp.sum(kb.astype(F32), axis=0, keepdims=True) * (1.0 / MOBA_BLOCK)
            return c
        lax.fori_loop(0, nb, body, 0)
        km = km_ref[...]
        km_hi = km.astype(BF16)
        km_lo = (km - km_hi.astype(F32)).astype(BF16)
        for a in range(2):
            in_head = (lane >= a * HEAD_DIM) & (lane < (a + 1) * HEAD_DIM)
            kms_ref[(2 * a) * nb:(2 * a + 1) * nb, :] = jnp.where(in_head, km_hi, jnp.zeros_like(km_hi))
            kms_ref[(2 * a + 1) * nb:(2 * a + 2) * nb, :] = jnp.where(in_head, km_lo, jnp.zeros_like(km_lo))

        n_q = SEL_CHUNK * MOBA_BLOCK
        def select_chunk(ci, c):
            q = q_ref[0, 0, pl.ds(pl.multiple_of(ci * n_q, n_q), n_q), :]
            g = _dot_nt(kms_ref[...], q)
            gate = jnp.concatenate([g[(2 * a) * nb:(2 * a + 1) * nb] + g[(2 * a + 1) * nb:(2 * a + 2) * nb]
                                    for a in range(2)], axis=1)
            pos = lax.broadcasted_iota(jnp.int32, (1, n_q), 1)
            own = ci * SEL_CHUNK + jnp.right_shift(pos, MOBA_BLOCK.bit_length() - 1)
            own = jnp.concatenate([own, own], axis=1)
            blk = lax.broadcasted_iota(jnp.int32, gate.shape, 0)
            gate = jnp.where(blk < own, gate, -jnp.inf)
            sel = blk == own
            for _ in range(MOBA_TOPK):
                top = jnp.max(gate, axis=0, keepdims=True)
                first = jnp.min(jnp.where(gate == top, blk, nb), axis=0, keepdims=True)
                pick = (blk == first) & (top > -jnp.inf)
                sel = sel | pick
                gate = jnp.where(pick, -jnp.inf, gate)
            sel_f = jnp.where(sel, 1.0, 0.0)
            for a in range(2):
                for b in range(SEL_CHUNK):
                    lo = a * n_q + b * MOBA_BLOCK
                    sel_ref[ci * SEL_CHUNK + b, a] = sel_f[:, lo:lo + MOBA_BLOCK]
            return c
        lax.fori_loop(0, nb // SEL_CHUNK, select_chunk, 0)

    for c, (h, a) in enumerate(chains):
        q = q_ref[0, 0, pl.ds(pl.multiple_of((step * Q_BLOCKS + h) * MOBA_BLOCK, MOBA_BLOCK), MOBA_BLOCK), :]
        qa_ref[c] = jnp.where((lane >= a * HEAD_DIM) & (lane < (a + 1) * HEAD_DIM), q, jnp.zeros_like(q))

    def block_of(t):
        return jnp.maximum(j_last - t, 0)

    def key_block(j):
        return k_ref[0, 0, pl.ds(pl.multiple_of(j * MOBA_BLOCK, MOBA_BLOCK), MOBA_BLOCK), :]

    def scores(t, c, kj, near):
        h, a = chains[c]
        s = _dot_nt(kj, qa_ref[c])
        offset = t - (Q_BLOCKS - 1 - h)
        if isinstance(t, int):
            if offset < NEAR_TILES:
                s = s + bias_ref[a, max(offset, 0)]
        elif near:
            s = s + bias_ref[a, jnp.clip(offset, 0, NEAR_TILES)]
        return s

    def visits(t, c):
        return not isinstance(t, int) or t - (Q_BLOCKS - 1 - chains[c][0]) >= 0

    def chosen_row(t, c, j):
        h, a = chains[c]
        return sel_ref[step * Q_BLOCKS + h, a, pl.ds(j, 1), :] > jnp.where(t <= j_last, 0.5, 2.0)

    def score_stage(t, slot, near):
        kj = key_block(block_of(t))
        for c in range(len(chains)):
            s = scores(t, c, kj, near)
            s_ref[slot, c] = s
            mt_ref[slot, c] = jnp.max(s, axis=0, keepdims=True)

    def softmax_stage(t, slot, m_prev):
        j = block_of(t)
        m_next, alphas, pvs = [], [], []
        for c, (h, a) in enumerate(chains):
            chosen = chosen_row(t, c, j)
            m_new = jnp.where(chosen, jnp.maximum(m_prev[c], mt_ref[slot, c]), m_prev[c])
            alphas.append(jnp.exp2(m_prev[c] - m_new))
            p = jnp.exp2(s_ref[slot, c] - jnp.where(chosen, m_new, BIG)).astype(BF16)
            pvs.append(_dot(vt_ref[0, 0, j, a], p))
            m_next.append(m_new)
        for c in range(len(chains)):
            acc_ref[c] = alphas[c] * acc_ref[c] + pvs[c]
        return tuple(m_next)

    def exact_pair(tt, m):
        t = 2 * tt
        score_stage(t + 1, 1, True)
        m = softmax_stage(t, 0, m)
        score_stage(t + 2, 0, True)
        m = softmax_stage(t + 1, 1, m)
        return m

    def lazy_stage(carry, near, score=None, pv=None):
        r, excess = list(carry[0]), list(carry[1])
        if score is not None:
            t, slot = score
            j = block_of(t)
            kj = key_block(j)
            for c in range(len(chains)):
                if not visits(t, c):
                    continue
                s = scores(t, c, kj, near)
                chosen = chosen_row(t, c, j)
                tile_max = jnp.max(s, axis=0, keepdims=True)
                p_ref[slot, c] = jnp.exp2(s - jnp.where(chosen, r[c], BIG)).astype(BF16)
                r_new = jnp.where(chosen, jnp.maximum(r[c], tile_max), r[c])
                alpha_ref[slot, c] = jnp.exp2(r[c] - r_new)
                excess[c] = jnp.maximum(excess[c], jnp.where(chosen, tile_max - r[c], -BIG))
                r[c] = r_new
        if pv is not None:
            t, slot = pv
            j = block_of(t)
            live = [c for c in range(len(chains)) if visits(t, c)]
            pvs = [_dot(vt_ref[0, 0, j, chains[c][1]], p_ref[slot, c]) for c in live]
            for c, prod in zip(live, pvs):
                acc_ref[c] = (acc_ref[c] + prod) * alpha_ref[slot, c]
        return tuple(r), tuple(excess)

    def lazy_pairs(it, carry, *, near, first_pair, pairs_per_trip, single_trip=False):
        if single_trip:
            t = 2 * first_pair
            carry = lazy_stage(carry, near, score=(t, 0))
        else:
            t = 2 * (first_pair + it * pairs_per_trip)
        for _ in range(pairs_per_trip):
            carry = lazy_stage(carry, near, score=(t + 1, 1), pv=(t, 0))
            carry = lazy_stage(carry, near, score=(t + 2, 0), pv=(t + 1, 1))
            t = t + 2
        return carry

    def finalize():
        for h in range(Q_BLOCKS):
            outs = []
            for a in range(2):
                acc = acc_ref[2 * h + a]
                outs.append(acc[0:HEAD_DIM] * (1.0 / acc[HEAD_DIM:HEAD_DIM + 1]))
            o_ref[0, 0, h * MOBA_BLOCK:(h + 1) * MOBA_BLOCK, :] = jnp.concatenate(outs, axis=0).T.astype(BF16)

    near_pairs = (NEAR_TILES + Q_BLOCKS) // 2
    n_pairs = (j_last + 2) // 2
    far_pairs = jnp.maximum(n_pairs - near_pairs, 0)
    lowest = (jnp.full((1, MOBA_BLOCK), -BIG, F32),) * len(chains)

    ones_rows = jnp.ones((BF16_ROWS, PAIR_W), BF16)
    seeds = []
    for c, (h, a) in enumerate(chains):
        own_scores = _dot_nt(ones_rows, qa_ref[c] * key_block(step * Q_BLOCKS + h))
        seeds.append(own_scores[0:1] + bias_ref[a, 0, 0:1, 0:1])

    acc_ref[...] = jnp.zeros_like(acc_ref)
    carry = (tuple(seeds), lowest)
    carry = lax.fori_loop(0, jnp.where(n_pairs >= near_pairs, 1, 0),
                          functools.partial(lazy_pairs, near=True, first_pair=0, pairs_per_trip=near_pairs,
                                            single_trip=True), carry)

    def few_pairs(it, carry):
        carry = lazy_stage(carry, True, score=(2 * it, 0))
        return lax.fori_loop(0, n_pairs, functools.partial(lazy_pairs, near=True, first_pair=0, pairs_per_trip=1), carry)
    carry = lax.fori_loop(0, jnp.where(n_pairs >= near_pairs, 0, 1), few_pairs, carry)
    first, left = near_pairs, far_pairs
    for pairs_per_trip in FAR_TRIP_PAIRS:
        trips = left // pairs_per_trip
        carry = lax.fori_loop(0, trips, functools.partial(lazy_pairs, near=False, first_pair=first,
                                                          pairs_per_trip=pairs_per_trip), carry)
        first, left = first + trips * pairs_per_trip, left - trips * pairs_per_trip
    finalize()

    worst = functools.reduce(jnp.maximum, carry[1])
    @pl.when(jnp.max(worst) > LAZY_MAX_EXCESS)
    def _():
        acc_ref[...] = jnp.zeros_like(acc_ref)
        score_stage(0, 0, True)
        lax.fori_loop(0, n_pairs, exact_pair, lowest)
        finalize()


def _attention(q, k, vt, bias_t):
    batch, _, seq, _ = q.shape
    nb = seq // MOBA_BLOCK
    tq = Q_BLOCKS * MOBA_BLOCK
    n_chains = 2 * Q_BLOCKS
    return pl.pallas_call(
        functools.partial(_attn_kernel, nb=nb),
        grid=(batch, HEAD_PAIRS, seq // tq),
        in_specs=[
            pl.BlockSpec((1, 1, seq, PAIR_W), lambda b, p, i: (b, p, 0, 0)),
            pl.BlockSpec((1, 1, seq, PAIR_W), lambda b, p, i: (b, p, 0, 0)),
            pl.BlockSpec((1, 1, nb, 2, V_ROWS, MOBA_BLOCK), lambda b, p, i: (b, p, 0, 0, 0, 0)),
            pl.BlockSpec((2, NEAR_TILES + 1, MOBA_BLOCK, MOBA_BLOCK), lambda b, p, i: (p, 0, 0, 0)),
        ],
        out_specs=pl.BlockSpec((1, 1, tq, PAIR_W), lambda b, p, i: (b, p, i, 0)),
        out_shape=jax.ShapeDtypeStruct((batch, HEAD_PAIRS, seq, PAIR_W), BF16),
        scratch_shapes=[
            pltpu.VMEM((nb, PAIR_W), F32),
            pltpu.VMEM((4 * nb, PAIR_W), BF16),
            pltpu.VMEM((n_chains, MOBA_BLOCK, PAIR_W), BF16),
            pltpu.VMEM((nb, 2, nb, MOBA_BLOCK), F32),
            pltpu.VMEM((n_chains, V_ROWS, MOBA_BLOCK), F32),
            pltpu.VMEM((2, n_chains, MOBA_BLOCK, MOBA_BLOCK), F32),
            pltpu.VMEM((2, n_chains, 1, MOBA_BLOCK), F32),
            pltpu.VMEM((2, n_chains, MOBA_BLOCK, MOBA_BLOCK), BF16),
            pltpu.VMEM((2, n_chains, 1, MOBA_BLOCK), F32),
        ],
        compiler_params=pltpu.CompilerParams(dimension_semantics=("arbitrary", "arbitrary", "arbitrary"),
                                             vmem_limit_bytes=VMEM_LIMIT),
        name="moba_attention",
    )(q, k, vt, bias_t)


def _post_kernel(x_ref, u_ref, vn_ref, sga_ref, sgb_ref, yb_ref, p_ref,
                 ws_ref, bs_ref, wo_ref, gffn_ref, w1_ref, w2_ref, gple_ref, wpg_ref, wpp_ref, gfin_ref,
                 o_ref, merged_ref, *, tm, final_norm):
    row = lax.broadcasted_iota(jnp.int32, (SGU_CHUNK, SGU_CHUNK), 0)
    col = lax.broadcasted_iota(jnp.int32, (SGU_CHUNK, SGU_CHUNK), 1)
    for g in range(SGU_GROUPS):
        w = jnp.where(row >= col, ws_ref[g], 0.0).astype(BF16)
        b = bs_ref[g]
        cols = slice(g * LANES, (g + 1) * LANES)
        for t in range(tm // SGU_CHUNK):
            rows = slice(t * SGU_CHUNK, (t + 1) * SGU_CHUNK)
            mixed = _dot(w, vn_ref[rows, cols]) + b
            y_a = u_ref[rows, cols].astype(F32) * mixed
            y_b = yb_ref[0, g, rows, :].astype(F32)
            merged = sga_ref[rows, cols].astype(F32) * y_a + sgb_ref[rows, cols].astype(F32) * y_b
            merged_ref[rows, cols] = merged.astype(BF16)

    x1 = x_ref[...] + _dot(merged_ref[...], wo_ref[...])

    h = _rms(x1, gffn_ref[...]).astype(BF16)
    x2 = x1
    for c in range(D_FF // D_MODEL):
        cs = slice(c * D_MODEL, (c + 1) * D_MODEL)
        a = jnp.square(jnp.maximum(_dot(h, w1_ref[:, cs]), 0.0)).astype(BF16)
        x2 = x2 + _dot(a, w2_ref[cs, :])

    gate = jax.nn.sigmoid(_dot(_rms(x2, gple_ref[...]).astype(BF16), wpg_ref[...]))
    x3 = x2 + gate * _dot(p_ref[...].astype(BF16), wpp_ref[...])
    o_ref[...] = _rms(x3, gfin_ref[...]) if final_norm else x3


def _post(x2, u, vn, sga, sgb, yb, p2, ws, bs, wo, gffn, w1, w2, gple, wpg, wpp, gfin, *, seq, tm, final_norm):
    n = x2.shape[0]
    tiles_per_seq = seq // tm
    row = lambda r: (r, 0)
    resident = functools.partial(pl.BlockSpec, pipeline_mode=pl.Buffered(1))
    c2 = lambda r: (0, 0)
    c3 = lambda r: (0, 0, 0)
    tokb = pl.BlockSpec((tm, D_MODEL), row)
    return pl.pallas_call(
        functools.partial(_post_kernel, tm=tm, final_norm=final_norm),
        grid=(n // tm,),
        in_specs=[
            tokb, tokb, tokb, tokb, tokb,
            pl.BlockSpec((1, HEAD_PAIRS, tm, PAIR_W), lambda r: (r // tiles_per_seq, 0, r % tiles_per_seq, 0)),
            pl.BlockSpec((tm, PLE_DIM), row),
            resident((SGU_GROUPS, SGU_CHUNK, SGU_CHUNK), c3),
            resident((SGU_GROUPS, SGU_CHUNK, 1), c3),
            resident((D_MODEL, D_MODEL), c2),
            resident((1, D_MODEL), c2),
            resident((D_MODEL, D_FF), c2),
            resident((D_FF, D_MODEL), c2),
            resident((1, D_MODEL), c2),
            resident((D_MODEL, D_MODEL), c2),
            resident((PLE_DIM, D_MODEL), c2),
            resident((1, D_MODEL), c2),
        ],
        out_specs=pl.BlockSpec((tm, D_MODEL), row),
        out_shape=jax.ShapeDtypeStruct((n, D_MODEL), F32),
        scratch_shapes=[pltpu.VMEM((tm, D_MODEL), BF16)],
        compiler_params=pltpu.CompilerParams(dimension_semantics=("arbitrary",),
                                             vmem_limit_bytes=VMEM_LIMIT),
        name="post",
    )(x2, u, vn, sga, sgb, yb, p2, ws, bs, wo, gffn, w1, w2, gple, wpg, wpp, gfin)


def kernel(x, p, norm_mix_g, w_in, w_sgu_spatial, b_sgu_spatial, ln_v_g, ln_v_b, rel_bias, w_out, norm_ffn_g,
           w_ff1, w_ff2, norm_ple_g, w_ple_gate, w_ple_proj, norm_final_g):
    batch, seq, _ = x.shape
    depth = w_in.shape[0]
    n = batch * seq
    assert seq % (Q_BLOCKS * MOBA_BLOCK) == 0 and seq // MOBA_BLOCK >= MOBA_TOPK
    tm_in = 512
    tm_post = 512
    vec = lambda g: g.reshape(1, D_MODEL)

    bias_t = _bias_tiles(rel_bias)
    x2 = x.reshape(n, D_MODEL)
    for i in range(depth):
        wi = w_in[i].astype(BF16)
        wa = wi[:, :4 * D_MODEL]
        wvt = wi[:, 4 * D_MODEL:5 * D_MODEL].T
        wg = wi[:, 5 * D_MODEL:]
        u, vn, q, k, vt, sga, sgb = _in_proj(
            x2, vec(norm_mix_g[i]), wa, wvt, wg, vec(ln_v_g[i]), vec(ln_v_b[i]),
            batch=batch, seq=seq, tm=tm_in)
        yb = _attention(q, k, vt, bias_t)
        x2 = _post(
            x2, u, vn, sga, sgb, yb, p[i].reshape(n, PLE_DIM),
            w_sgu_spatial[i], b_sgu_spatial[i].reshape(SGU_GROUPS, SGU_CHUNK, 1),
            w_out[i].astype(BF16), vec(norm_ffn_g[i]), w_ff1[i].astype(BF16), w_ff2[i].astype(BF16),
            vec(norm_ple_g[i]), w_ple_gate[i].astype(BF16), w_ple_proj[i].astype(BF16),
            vec(norm_final_g), seq=seq, tm=tm_post, final_norm=(i == depth - 1))
    return x2.reshape(batch, seq, D_MODEL)
```

```python
import functools
import math

import jax
import jax.numpy as jnp
from jax import lax
from jax.experimental import pallas as pl
from jax.experimental.pallas import tpu as pltpu

D_MODEL = 1024
PLE_DIM = 256
SGU_CHUNK = 128
SGU_GROUPS = 8
ATT_HEADS = 16
HEAD_DIM = 64
MOBA_BLOCK = 256
MOBA_TOPK = 3
REL_BUCKETS = 32
REL_MAX_DIST = 1024
D_FF = 4 * D_MODEL
EPS = 1e-6

LANES = 128
BF16_ROWS = 16
HEAD_PAIRS = ATT_HEADS // 2
PAIR_W = 2 * HEAD_DIM
V_ROWS = HEAD_DIM + BF16_ROWS
Q_BLOCKS = 4
FAR_TRIP_PAIRS = (2, 1)
SEL_CHUNK = 8
LAZY_MAX_EXCESS = 16.0
BIG = 1e30
VMEM_LIMIT = 60 * 1024 * 1024
LOG2E = math.log2(math.e)

F32 = jnp.float32
BF16 = jnp.bfloat16


def _bucket_thresholds():
    max_exact = REL_BUCKETS // 2
    n_log = REL_BUCKETS - max_exact
    ratio = REL_MAX_DIST // max_exact
    out = []
    for k in range(1, n_log):
        d = max_exact
        target = (max_exact ** n_log) * (ratio ** k)
        while d ** n_log < target:
            d += 1
        out.append(d)
    return tuple(out)


BUCKET_THRESHOLDS = _bucket_thresholds()
NEAR_TILES = -(-(BUCKET_THRESHOLDS[-1] + MOBA_BLOCK - 1) // MOBA_BLOCK)
assert NEAR_TILES == 5


def _dot(a, b):
    return jnp.dot(a, b, preferred_element_type=F32)


def _dot_nt(a, b):
    return lax.dot_general(a, b, (((1,), (1,)), ((), ())), preferred_element_type=F32)


def _rms(x, g):
    return x * lax.rsqrt(jnp.mean(x * x, axis=-1, keepdims=True) + EPS) * g


def _in_proj_kernel(x_ref, g_ref, wa_ref, wvt_ref, wga_ref, wgb_ref, lng_ref, lnb_ref,
                    u_ref, vn_ref, q_ref, k_ref, vt_ref, sga_ref, sgb_ref, *, tm):
    h = _rms(x_ref[...], g_ref[...]).astype(BF16)

    u_ref[...] = jax.nn.gelu(_dot(h, wa_ref[:, 0:D_MODEL])).astype(BF16)

    gv = jax.nn.gelu(_dot(h, wa_ref[:, D_MODEL:2 * D_MODEL]))
    mu = jnp.mean(gv, axis=-1, keepdims=True)
    gc = gv - mu
    vn = gc * lax.rsqrt(jnp.mean(gc * gc, axis=-1, keepdims=True) + EPS)
    vn_ref[...] = (vn * lng_ref[...] + lnb_ref[...]).astype(BF16)

    q = _dot(h, wa_ref[:, 2 * D_MODEL:3 * D_MODEL]) * (HEAD_DIM ** -0.5 * LOG2E)
    k = _dot(h, wa_ref[:, 3 * D_MODEL:4 * D_MODEL])
    for p in range(HEAD_PAIRS):
        q_ref[0, p] = q[:, p * PAIR_W:(p + 1) * PAIR_W].astype(BF16)
        k_ref[0, p] = k[:, p * PAIR_W:(p + 1) * PAIR_W].astype(BF16)

    vt = _dot_nt(wvt_ref[...], h).astype(BF16)
    ones = jnp.ones((BF16_ROWS, MOBA_BLOCK), BF16)
    for p in range(HEAD_PAIRS):
        for jb in range(tm // MOBA_BLOCK):
            for a in range(2):
                r0 = p * PAIR_W + a * HEAD_DIM
                vt_ref[0, p, jb, a, 0:HEAD_DIM, :] = vt[r0:r0 + HEAD_DIM, jb * MOBA_BLOCK:(jb + 1) * MOBA_BLOCK]
                vt_ref[0, p, jb, a, HEAD_DIM:V_ROWS, :] = ones

    sga_ref[...] = jax.nn.sigmoid(_dot(h, wga_ref[...])).astype(BF16)
    sgb_ref[...] = jax.nn.sigmoid(_dot(h, wgb_ref[...])).astype(BF16)


def _in_proj(x2, g, w, wvt, lng, lnb, *, batch, seq, tm):
    n = x2.shape[0]
    tiles_per_seq = seq // tm
    nb = seq // MOBA_BLOCK
    row = lambda r: (r, 0)
    const = lambda r: (0, 0)
    resident = functools.partial(pl.BlockSpec, pipeline_mode=pl.Buffered(1))
    tok = jax.ShapeDtypeStruct((n, D_MODEL), BF16)
    pair_major = jax.ShapeDtypeStruct((batch, HEAD_PAIRS, seq, PAIR_W), BF16)
    return pl.pallas_call(
        functools.partial(_in_proj_kernel, tm=tm),
        grid=(n // tm,),
        in_specs=[
            pl.BlockSpec((tm, D_MODEL), row),
            resident((1, D_MODEL), const),
            resident((D_MODEL, 4 * D_MODEL), const),
            resident((D_MODEL, D_MODEL), const),
            resident((D_MODEL, D_MODEL), lambda r: (0, 5)),
            resident((D_MODEL, D_MODEL), lambda r: (0, 6)),
            resident((1, D_MODEL), const),
            resident((1, D_MODEL), const),
        ],
        out_specs=[
            pl.BlockSpec((tm, D_MODEL), row),
            pl.BlockSpec((tm, D_MODEL), row),
            pl.BlockSpec((1, HEAD_PAIRS, tm, PAIR_W), lambda r: (r // tiles_per_seq, 0, r % tiles_per_seq, 0)),
            pl.BlockSpec((1, HEAD_PAIRS, tm, PAIR_W), lambda r: (r // tiles_per_seq, 0, r % tiles_per_seq, 0)),
            pl.BlockSpec((1, HEAD_PAIRS, tm // MOBA_BLOCK, 2, V_ROWS, MOBA_BLOCK),
                         lambda r: (r // tiles_per_seq, 0, r % tiles_per_seq, 0, 0, 0)),
            pl.BlockSpec((tm, D_MODEL), row),
            pl.BlockSpec((tm, D_MODEL), row),
        ],
        out_shape=[tok, tok, pair_major, pair_major,
                   jax.ShapeDtypeStruct((batch, HEAD_PAIRS, nb, 2, V_ROWS, MOBA_BLOCK), BF16),
                   tok, tok],
        compiler_params=pltpu.CompilerParams(dimension_semantics=("arbitrary",),
                                             vmem_limit_bytes=VMEM_LIMIT),
        name="in_proj",
    )(x2, g, w, wvt, w, w, lng, lnb)


def _bias_tiles_kernel(tab_ref, o_ref):
    h = pl.program_id(0)
    key = lax.broadcasted_iota(jnp.int32, (MOBA_BLOCK, MOBA_BLOCK), 0)
    qry = lax.broadcasted_iota(jnp.int32, (MOBA_BLOCK, MOBA_BLOCK), 1)
    max_exact = REL_BUCKETS // 2
    last = tab_ref[REL_BUCKETS - 1, h]
    for t in range(NEAR_TILES):
        dist = t * MOBA_BLOCK + qry - key
        n = jnp.maximum(dist, 0)
        val = jnp.full((MOBA_BLOCK, MOBA_BLOCK), tab_ref[0, h], F32)
        for b in range(1, max_exact + 1):
            val = jnp.where(n >= b, tab_ref[b, h], val)
        for kk, thr in enumerate(BUCKET_THRESHOLDS):
            val = jnp.where(n >= thr, tab_ref[max_exact + 1 + kk, h], val)
        val = (val - last) * LOG2E
        if t == 0:
            val = jnp.where(dist >= 0, val, -BIG)
        o_ref[0, t] = val
    o_ref[0, NEAR_TILES] = jnp.zeros((MOBA_BLOCK, MOBA_BLOCK), F32)


def _bias_tiles(rel_bias):
    return pl.pallas_call(
        _bias_tiles_kernel,
        grid=(ATT_HEADS,),
        in_specs=[pl.BlockSpec(memory_space=pltpu.SMEM)],
        out_specs=pl.BlockSpec((1, NEAR_TILES + 1, MOBA_BLOCK, MOBA_BLOCK), lambda h: (h, 0, 0, 0)),
        out_shape=jax.ShapeDtypeStruct((ATT_HEADS, NEAR_TILES + 1, MOBA_BLOCK, MOBA_BLOCK), F32),
        compiler_params=pltpu.CompilerParams(dimension_semantics=("arbitrary",)),
        name="bias_tiles",
    )(rel_bias)


def _attn_kernel(q_ref, k_ref, vt_ref, bias_ref, o_ref,
                 km_ref, kms_ref, qa_ref, sel_ref, acc_ref, s_ref, mt_ref, p_ref, alpha_ref, *, nb):
    step = pl.program_id(2)
    j_last = step * Q_BLOCKS + (Q_BLOCKS - 1)
    chains = [(h, a) for h in range(Q_BLOCKS) for a in range(2)]
    lane = lax.broadcasted_iota(jnp.int32, (1, PAIR_W), 1)

    @pl.when(step == 0)
    def _():
        def body(j, c):
            kb = k_ref[0, 0, pl.ds(pl.multiple_of(j * MOBA_BLOCK, MOBA_BLOCK), MOBA_BLOCK), :]
            km_ref[pl.ds(j, 1), :] = jnp.sum(kb.astype(F32), axis=0, keepdims=True) * (1.0 / MOBA_BLOCK)
            return c
        lax.fori_loop(0, nb, body, 0)
        km = km_ref[...]
        km_hi = km.astype(BF16)
        km_lo = (km - km_hi.astype(F32)).astype(BF16)
        for a in range(2):
            in_head = (lane >= a * HEAD_DIM) & (lane < (a + 1) * HEAD_DIM)
            kms_ref[(2 * a) * nb:(2 * a + 1) * nb, :] = jnp.where(in_head, km_hi, jnp.zeros_like(km_hi))
            kms_ref[(2 * a + 1) * nb:(2 * a + 2) * nb, :] = jnp.where(in_head, km_lo, jnp.zeros_like(km_lo))

        n_q = SEL_CHUNK * MOBA_BLOCK
        def select_chunk(ci, c):
            q = q_ref[0, 0, pl.ds(pl.multiple_of(ci * n_q, n_q), n_q), :]
            g = _dot_nt(kms_ref[...], q)
            gate = jnp.concatenate([g[(2 * a) * nb:(2 * a + 1) * nb] + g[(2 * a + 1) * nb:(2 * a + 2) * nb]
                                    for a in range(2)], axis=1)
            pos = lax.broadcasted_iota(jnp.int32, (1, n_q), 1)
            own = ci * SEL_CHUNK + jnp.right_shift(pos, MOBA_BLOCK.bit_length() - 1)
            own = jnp.concatenate([own, own], axis=1)
            blk = lax.broadcasted_iota(jnp.int32, gate.shape, 0)
            gate = jnp.where(blk < own, gate, -jnp.inf)
            sel = blk == own
            for _ in range(MOBA_TOPK):
                top = jnp.max(gate, axis=0, keepdims=True)
                first = jnp.min(jnp.where(gate == top, blk, nb), axis=0, keepdims=True)
                pick = (blk == first) & (top > -jnp.inf)
                sel = sel | pick
                gate = jnp.where(pick, -jnp.inf, gate)
            sel_f = jnp.where(sel, 1.0, 0.0)
            for a in range(2):
                for b in range(SEL_CHUNK):
                    lo = a * n_q + b * MOBA_BLOCK
                    sel_ref[ci * SEL_CHUNK + b, a] = sel_f[:, lo:lo + MOBA_BLOCK]
            return c
        lax.fori_loop(0, nb // SEL_CHUNK, select_chunk, 0)

    for c, (h, a) in enumerate(chains):
        q = q_ref[0, 0, pl.ds(pl.multiple_of((step * Q_BLOCKS + h) * MOBA_BLOCK, MOBA_BLOCK), MOBA_BLOCK), :]
        qa_ref[c] = jnp.where((lane >= a * HEAD_DIM) & (lane < (a + 1) * HEAD_DIM), q, jnp.zeros_like(q))

    def block_of(t):
        return jnp.maximum(j_last - t, 0)

    def key_block(j):
        return k_ref[0, 0, pl.ds(pl.multiple_of(j * MOBA_BLOCK, MOBA_BLOCK), MOBA_BLOCK), :]

    def scores(t, c, kj, near):
        h, a = chains[c]
        s = _dot_nt(kj, qa_ref[c])
        offset = t - (Q_BLOCKS - 1 - h)
        if isinstance(t, int):
            if offset < NEAR_TILES:
                s = s + bias_ref[a, max(offset, 0)]
        elif near:
            s = s + bias_ref[a, jnp.clip(offset, 0, NEAR_TILES)]
        return s

    def visits(t, c):
        return not isinstance(t, int) or t - (Q_BLOCKS - 1 - chains[c][0]) >= 0

    def chosen_row(t, c, j):
        h, a = chains[c]
        return sel_ref[step * Q_BLOCKS + h, a, pl.ds(j, 1), :] > jnp.where(t <= j_last, 0.5, 2.0)

    def score_stage(t, slot, near):
        kj = key_block(block_of(t))
        for c in range(len(chains)):
            s = scores(t, c, kj, near)
            s_ref[slot, c] = s
            mt_ref[slot, c] = jnp.max(s, axis=0, keepdims=True)

    def softmax_stage(t, slot, m_prev):
        j = block_of(t)
        m_next, alphas, pvs = [], [], []
        for c, (h, a) in enumerate(chains):
            chosen = chosen_row(t, c, j)
            m_new = jnp.where(chosen, jnp.maximum(m_prev[c], mt_ref[slot, c]), m_prev[c])
            alphas.append(jnp.exp2(m_prev[c] - m_new))
            p = jnp.exp2(s_ref[slot, c] - jnp.where(chosen, m_new, BIG)).astype(BF16)
            pvs.append(_dot(vt_ref[0, 0, j, a], p))
            m_next.append(m_new)
        for c in range(len(chains)):
            acc_ref[c] = alphas[c] * acc_ref[c] + pvs[c]
        return tuple(m_next)

    def exact_pair(tt, m):
        t = 2 * tt
        score_stage(t + 1, 1, True)
        m = softmax_stage(t, 0, m)
        score_stage(t + 2, 0, True)
        m = softmax_stage(t + 1, 1, m)
        return m

    def lazy_stage(carry, near, score=None, pv=None):
        r, excess = list(carry[0]), list(carry[1])
        if score is not None:
            t, slot = score
            j = block_of(t)
            kj = key_block(j)
            for c in range(len(chains)):
                if not visits(t, c):
                    continue
                s = scores(t, c, kj, near)
                chosen = chosen_row(t, c, j)
                p = jnp.exp2(s - jnp.where(chosen, r[c], BIG)).astype(BF16)
                p_ref[slot, c] = p
                p_max = jnp.max(p, axis=0, keepdims=True).astype(F32)
                over = jnp.maximum(p_max, 1.0)
                alpha_ref[slot, c] = 1.0 / over
                excess[c] = jnp.maximum(excess[c], p_max)
                r[c] = r[c] + jnp.log2(over)
        if pv is not None:
            t, slot = pv
            j = block_of(t)
            live = [c for c in range(len(chains)) if visits(t, c)]
            pvs = [_dot(vt_ref[0, 0, j, chains[c][1]], p_ref[slot, c]) for c in live]
            for c, prod in zip(live, pvs):
                acc_ref[c] = (acc_ref[c] + prod) * alpha_ref[slot, c]
        return tuple(r), tuple(excess)

    def lazy_pairs(it, carry, *, near, first_pair, pairs_per_trip, single_trip=False):
        if single_trip:
            t = 2 * first_pair
            carry = lazy_stage(carry, near, score=(t, 0))
        else:
            t = 2 * (first_pair + it * pairs_per_trip)
        for _ in range(pairs_per_trip):
            carry = lazy_stage(carry, near, score=(t + 1, 1), pv=(t, 0))
            carry = lazy_stage(carry, near, score=(t + 2, 0), pv=(t + 1, 1))
            t = t + 2
        return carry

    def finalize():
        for h in range(Q_BLOCKS):
            outs = []
            for a in range(2):
                acc = acc_ref[2 * h + a]
                outs.append(acc[0:HEAD_DIM] * (1.0 / acc[HEAD_DIM:HEAD_DIM + 1]))
            o_ref[0, 0, h * MOBA_BLOCK:(h + 1) * MOBA_BLOCK, :] = jnp.concatenate(outs, axis=0).T.astype(BF16)

    near_pairs = (NEAR_TILES + Q_BLOCKS) // 2
    n_pairs = (j_last + 2) // 2
    far_pairs = jnp.maximum(n_pairs - near_pairs, 0)
    lowest = (jnp.full((1, MOBA_BLOCK), -BIG, F32),) * len(chains)

    ones_rows = jnp.ones((BF16_ROWS, PAIR_W), BF16)
    seeds = []
    for c, (h, a) in enumerate(chains):
        own_scores = _dot_nt(ones_rows, qa_ref[c] * key_block(step * Q_BLOCKS + h))
        seeds.append(own_scores[0:1] + bias_ref[a, 0, 0:1, 0:1])

    acc_ref[...] = jnp.zeros_like(acc_ref)
    carry = (tuple(seeds), lowest)
    carry = lax.fori_loop(0, jnp.where(n_pairs >= near_pairs, 1, 0),
                          functools.partial(lazy_pairs, near=True, first_pair=0, pairs_per_trip=near_pairs,
                                            single_trip=True), carry)

    def few_pairs(it, carry):
        carry = lazy_stage(carry, True, score=(2 * it, 0))
        return lax.fori_loop(0, n_pairs, functools.partial(lazy_pairs, near=True, first_pair=0, pairs_per_trip=1), carry)
    carry = lax.fori_loop(0, jnp.where(n_pairs >= near_pairs, 0, 1), few_pairs, carry)
    first, left = near_pairs, far_pairs
    for pairs_per_trip in FAR_TRIP_PAIRS:
        trips = left // pairs_per_trip
        carry = lax.fori_loop(0, trips, functools.partial(lazy_pairs, near=False, first_pair=first,
                                                          pairs_per_trip=pairs_per_trip), carry)
        first, left = first + trips * pairs_per_trip, left - trips * pairs_per_trip
    finalize()

    worst = functools.reduce(jnp.maximum, carry[1])
    @pl.when(jnp.max(worst) > 2.0 ** LAZY_MAX_EXCESS)
    def _():
        acc_ref[...] = jnp.zeros_like(acc_ref)
        score_stage(0, 0, True)
        lax.fori_loop(0, n_pairs, exact_pair, lowest)
        finalize()


def _attention(q, k, vt, bias_t):
    batch, _, seq, _ = q.shape
    nb = seq // MOBA_BLOCK
    tq = Q_BLOCKS * MOBA_BLOCK
    n_chains = 2 * Q_BLOCKS
    return pl.pallas_call(
        functools.partial(_attn_kernel, nb=nb),
        grid=(batch, HEAD_PAIRS, seq // tq),
        in_specs=[
            pl.BlockSpec((1, 1, seq, PAIR_W), lambda b, p, i: (b, p, 0, 0)),
            pl.BlockSpec((1, 1, seq, PAIR_W), lambda b, p, i: (b, p, 0, 0)),
            pl.BlockSpec((1, 1, nb, 2, V_ROWS, MOBA_BLOCK), lambda b, p, i: (b, p, 0, 0, 0, 0)),
            pl.BlockSpec((2, NEAR_TILES + 1, MOBA_BLOCK, MOBA_BLOCK), lambda b, p, i: (p, 0, 0, 0)),
        ],
        out_specs=pl.BlockSpec((1, 1, tq, PAIR_W), lambda b, p, i: (b, p, i, 0)),
        out_shape=jax.ShapeDtypeStruct((batch, HEAD_PAIRS, seq, PAIR_W), BF16),
        scratch_shapes=[
            pltpu.VMEM((nb, PAIR_W), F32),
            pltpu.VMEM((4 * nb, PAIR_W), BF16),
            pltpu.VMEM((n_chains, MOBA_BLOCK, PAIR_W), BF16),
            pltpu.VMEM((nb, 2, nb, MOBA_BLOCK), F32),
            pltpu.VMEM((n_chains, V_ROWS, MOBA_BLOCK), F32),
            pltpu.VMEM((2, n_chains, MOBA_BLOCK, MOBA_BLOCK), F32),
            pltpu.VMEM((2, n_chains, 1, MOBA_BLOCK), F32),
            pltpu.VMEM((2, n_chains, MOBA_BLOCK, MOBA_BLOCK), BF16),
            pltpu.VMEM((2, n_chains, 1, MOBA_BLOCK), F32),
        ],
        compiler_params=pltpu.CompilerParams(dimension_semantics=("arbitrary", "arbitrary", "arbitrary"),
                                             vmem_limit_bytes=VMEM_LIMIT),
        name="moba_attention",
    )(q, k, vt, bias_t)


def _post_kernel(x_ref, u_ref, vn_ref, sga_ref, sgb_ref, yb_ref, p_ref,
                 ws_ref, bs_ref, wo_ref, gffn_ref, w1_ref, w2_ref, gple_ref, wpg_ref, wpp_ref, gfin_ref,
                 o_ref, merged_ref, *, tm, final_norm):
    row = lax.broadcasted_iota(jnp.int32, (SGU_CHUNK, SGU_CHUNK), 0)
    col = lax.broadcasted_iota(jnp.int32, (SGU_CHUNK, SGU_CHUNK), 1)
    for g in range(SGU_GROUPS):
        w = jnp.where(row >= col, ws_ref[g], 0.0).astype(BF16)
        b = bs_ref[g]
        cols = slice(g * LANES, (g + 1) * LANES)
        for t in range(tm // SGU_CHUNK):
            rows = slice(t * SGU_CHUNK, (t + 1) * SGU_CHUNK)
            mixed = _dot(w, vn_ref[rows, cols]) + b
            y_a = u_ref[rows, cols].astype(F32) * mixed
            y_b = yb_ref[0, g, rows, :].astype(F32)
            merged = sga_ref[rows, cols].astype(F32) * y_a + sgb_ref[rows, cols].astype(F32) * y_b
            merged_ref[rows, cols] = merged.astype(BF16)

    x1 = x_ref[...] + _dot(merged_ref[...], wo_ref[...])

    h = _rms(x1, gffn_ref[...]).astype(BF16)
    x2 = x1
    for c in range(D_FF // D_MODEL):
        cs = slice(c * D_MODEL, (c + 1) * D_MODEL)
        a = jnp.square(jnp.maximum(_dot(h, w1_ref[:, cs]), 0.0)).astype(BF16)
        x2 = x2 + _dot(a, w2_ref[cs, :])

    gate = jax.nn.sigmoid(_dot(_rms(x2, gple_ref[...]).astype(BF16), wpg_ref[...]))
    x3 = x2 + gate * _dot(p_ref[...].astype(BF16), wpp_ref[...])
    o_ref[...] = _rms(x3, gfin_ref[...]) if final_norm else x3


def _post(x2, u, vn, sga, sgb, yb, p2, ws, bs, wo, gffn, w1, w2, gple, wpg, wpp, gfin, *, seq, tm, final_norm):
    n = x2.shape[0]
    tiles_per_seq = seq // tm
    row = lambda r: (r, 0)
    resident = functools.partial(pl.BlockSpec, pipeline_mode=pl.Buffered(1))
    c2 = lambda r: (0, 0)
    c3 = lambda r: (0, 0, 0)
    tokb = pl.BlockSpec((tm, D_MODEL), row)
    return pl.pallas_call(
        functools.partial(_post_kernel, tm=tm, final_norm=final_norm),
        grid=(n // tm,),
        in_specs=[
            tokb, tokb, tokb, tokb, tokb,
            pl.BlockSpec((1, HEAD_PAIRS, tm, PAIR_W), lambda r: (r // tiles_per_seq, 0, r % tiles_per_seq, 0)),
            pl.BlockSpec((tm, PLE_DIM), row),
            resident((SGU_GROUPS, SGU_CHUNK, SGU_CHUNK), c3),
            resident((SGU_GROUPS, SGU_CHUNK, 1), c3),
            resident((D_MODEL, D_MODEL), c2),
            resident((1, D_MODEL), c2),
            resident((D_MODEL, D_FF), c2),
            resident((D_FF, D_MODEL), c2),
            resident((1, D_MODEL), c2),
            resident((D_MODEL, D_MODEL), c2),
            resident((PLE_DIM, D_MODEL), c2),
            resident((1, D_MODEL), c2),
        ],
        out_specs=pl.BlockSpec((tm, D_MODEL), row),
        out_shape=jax.ShapeDtypeStruct((n, D_MODEL), F32),
        scratch_shapes=[pltpu.VMEM((tm, D_MODEL), BF16)],
        compiler_params=pltpu.CompilerParams(dimension_semantics=("arbitrary",),
                                             vmem_limit_bytes=VMEM_LIMIT),
        name="post",
    )(x2, u, vn, sga, sgb, yb, p2, ws, bs, wo, gffn, w1, w2, gple, wpg, wpp, gfin)


def kernel(x, p, norm_mix_g, w_in, w_sgu_spatial, b_sgu_spatial, ln_v_g, ln_v_b, rel_bias, w_out, norm_ffn_g,
           w_ff1, w_ff2, norm_ple_g, w_ple_gate, w_ple_proj, norm_final_g):
    batch, seq, _ = x.shape
    depth = w_in.shape[0]
    n = batch * seq
    assert seq % (Q_BLOCKS * MOBA_BLOCK) == 0 and seq // MOBA_BLOCK >= MOBA_TOPK
    tm_in = 512
    tm_post = 512
    vec = lambda g: g.reshape(1, D_MODEL)

    bias_t = _bias_tiles(rel_bias)
    x2 = x.reshape(n, D_MODEL)
    for i in range(depth):
        wi = w_in[i].astype(BF16)
        wvt = wi[:, 4 * D_MODEL:5 * D_MODEL].T
        u, vn, q, k, vt, sga, sgb = _in_proj(
            x2, vec(norm_mix_g[i]), wi, wvt, vec(ln_v_g[i]), vec(ln_v_b[i]),
            batch=batch, seq=seq, tm=tm_in)
        yb = _attention(q, k, vt, bias_t)
        x2 = _post(
            x2, u, vn, sga, sgb, yb, p[i].reshape(n, PLE_DIM),
            w_sgu_spatial[i], b_sgu_spatial[i].reshape(SGU_GROUPS, SGU_CHUNK, 1),
            w_out[i].astype(BF16), vec(norm_ffn_g[i]), w_ff1[i].astype(BF16), w_ff2[i].astype(BF16),
            vec(norm_ple_g[i]), w_ple_gate[i].astype(BF16), w_ple_proj[i].astype(BF16),
            vec(norm_final_g), seq=seq, tm=tm_post, final_norm=(i == depth - 1))
    return x2.reshape(batch, seq, D_MODEL)
```

```python
import functools
import math

import jax
import jax.numpy as jnp
from jax import lax
from jax.experimental import pallas as pl
from jax.experimental.pallas import tpu as pltpu

D_MODEL = 1024
PLE_DIM = 256
SGU_CHUNK = 128
SGU_GROUPS = 8
ATT_HEADS = 16
HEAD_DIM = 64
MOBA_BLOCK = 256
MOBA_TOPK = 3
REL_BUCKETS = 32
REL_MAX_DIST = 1024
D_FF = 4 * D_MODEL
EPS = 1e-6

LANES = 128
BF16_ROWS = 16
HEAD_PAIRS = ATT_HEADS // 2
PAIR_W = 2 * HEAD_DIM
V_ROWS = HEAD_DIM + BF16_ROWS
Q_BLOCKS = 4
FAR_TRIP_PAIRS = (2, 1)
SEL_CHUNK = 8
LAZY_MAX_EXCESS = 16.0
BIG = 1e30
VMEM_LIMIT = 60 * 1024 * 1024
LOG2E = math.log2(math.e)

F32 = jnp.float32
BF16 = jnp.bfloat16


def _bucket_thresholds():
    max_exact = REL_BUCKETS // 2
    n_log = REL_BUCKETS - max_exact
    ratio = REL_MAX_DIST // max_exact
    out = []
    for k in range(1, n_log):
        d = max_exact
        target = (max_exact ** n_log) * (ratio ** k)
        while d ** n_log < target:
            d += 1
        out.append(d)
    return tuple(out)


BUCKET_THRESHOLDS = _bucket_thresholds()
NEAR_TILES = -(-(BUCKET_THRESHOLDS[-1] + MOBA_BLOCK - 1) // MOBA_BLOCK)
assert NEAR_TILES == 5


def _dot(a, b):
    return jnp.dot(a, b, preferred_element_type=F32)


def _dot_nt(a, b):
    return lax.dot_general(a, b, (((1,), (1,)), ((), ())), preferred_element_type=F32)


def _rms(x, g):
    return x * lax.rsqrt(jnp.mean(x * x, axis=-1, keepdims=True) + EPS) * g


def _in_proj_kernel(x_ref, g_ref, wa_ref, wvt_ref, wga_ref, wgb_ref, lng_ref, lnb_ref,
                    u_ref, vn_ref, q_ref, k_ref, vt_ref, sga_ref, sgb_ref, *, tm):
    h = _rms(x_ref[...], g_ref[...]).astype(BF16)

    u_ref[...] = jax.nn.gelu(_dot(h, wa_ref[:, 0:D_MODEL])).astype(BF16)

    gv = jax.nn.gelu(_dot(h, wa_ref[:, D_MODEL:2 * D_MODEL]))
    mu = jnp.mean(gv, axis=-1, keepdims=True)
    gc = gv - mu
    vn = gc * lax.rsqrt(jnp.mean(gc * gc, axis=-1, keepdims=True) + EPS)
    vn_ref[...] = (vn * lng_ref[...] + lnb_ref[...]).astype(BF16)

    q = _dot(h, wa_ref[:, 2 * D_MODEL:3 * D_MODEL]) * (HEAD_DIM ** -0.5 * LOG2E)
    k = _dot(h, wa_ref[:, 3 * D_MODEL:4 * D_MODEL])
    for p in range(HEAD_PAIRS):
        q_ref[0, p] = q[:, p * PAIR_W:(p + 1) * PAIR_W].astype(BF16)
        k_ref[0, p] = k[:, p * PAIR_W:(p + 1) * PAIR_W].astype(BF16)

    vt = _dot_nt(wvt_ref[...], h).astype(BF16)
    ones = jnp.ones((BF16_ROWS, MOBA_BLOCK), BF16)
    for p in range(HEAD_PAIRS):
        for jb in range(tm // MOBA_BLOCK):
            for a in range(2):
                r0 = p * PAIR_W + a * HEAD_DIM
                vt_ref[0, p, jb, a, 0:HEAD_DIM, :] = vt[r0:r0 + HEAD_DIM, jb * MOBA_BLOCK:(jb + 1) * MOBA_BLOCK]
                vt_ref[0, p, jb, a, HEAD_DIM:V_ROWS, :] = ones

    sga_ref[...] = jax.nn.sigmoid(_dot(h, wga_ref[...])).astype(BF16)
    sgb_ref[...] = jax.nn.sigmoid(_dot(h, wgb_ref[...])).astype(BF16)


def _in_proj(x2, g, w, wvt, lng, lnb, *, batch, seq, tm):
    n = x2.shape[0]
    tiles_per_seq = seq // tm
    nb = seq // MOBA_BLOCK
    row = lambda r: (r, 0)
    const = lambda r: (0, 0)
    resident = functools.partial(pl.BlockSpec, pipeline_mode=pl.Buffered(1))
    tok = jax.ShapeDtypeStruct((n, D_MODEL), BF16)
    pair_major = jax.ShapeDtypeStruct((batch, HEAD_PAIRS, seq, PAIR_W), BF16)
    return pl.pallas_call(
        functools.partial(_in_proj_kernel, tm=tm),
        grid=(n // tm,),
        in_specs=[
            pl.BlockSpec((tm, D_MODEL), row),
            resident((1, D_MODEL), const),
            resident((D_MODEL, 4 * D_MODEL), const),
            resident((D_MODEL, D_MODEL), const),
            resident((D_MODEL, D_MODEL), lambda r: (0, 5)),
            resident((D_MODEL, D_MODEL), lambda r: (0, 6)),
            resident((1, D_MODEL), const),
            resident((1, D_MODEL), const),
        ],
        out_specs=[
            pl.BlockSpec((tm, D_MODEL), row),
            pl.BlockSpec((tm, D_MODEL), row),
            pl.BlockSpec((1, HEAD_PAIRS, tm, PAIR_W), lambda r: (r // tiles_per_seq, 0, r % tiles_per_seq, 0)),
            pl.BlockSpec((1, HEAD_PAIRS, tm, PAIR_W), lambda r: (r // tiles_per_seq, 0, r % tiles_per_seq, 0)),
            pl.BlockSpec((1, HEAD_PAIRS, tm // MOBA_BLOCK, 2, V_ROWS, MOBA_BLOCK),
                         lambda r: (r // tiles_per_seq, 0, r % tiles_per_seq, 0, 0, 0)),
            pl.BlockSpec((tm, D_MODEL), row),
            pl.BlockSpec((tm, D_MODEL), row),
        ],
        out_shape=[tok, tok, pair_major, pair_major,
                   jax.ShapeDtypeStruct((batch, HEAD_PAIRS, nb, 2, V_ROWS, MOBA_BLOCK), BF16),
                   tok, tok],
        compiler_params=pltpu.CompilerParams(dimension_semantics=("arbitrary",),
                                             vmem_limit_bytes=VMEM_LIMIT),
        name="in_proj",
    )(x2, g, w, wvt, w, w, lng, lnb)


def _bias_tiles_kernel(tab_ref, o_ref):
    h = pl.program_id(0)
    key = lax.broadcasted_iota(jnp.int32, (MOBA_BLOCK, MOBA_BLOCK), 0)
    qry = lax.broadcasted_iota(jnp.int32, (MOBA_BLOCK, MOBA_BLOCK), 1)
    max_exact = REL_BUCKETS // 2
    last = tab_ref[REL_BUCKETS - 1, h]
    for t in range(NEAR_TILES):
        dist = t * MOBA_BLOCK + qry - key
        n = jnp.maximum(dist, 0)
        val = jnp.full((MOBA_BLOCK, MOBA_BLOCK), tab_ref[0, h], F32)
        for b in range(1, max_exact + 1):
            val = jnp.where(n >= b, tab_ref[b, h], val)
        for kk, thr in enumerate(BUCKET_THRESHOLDS):
            val = jnp.where(n >= thr, tab_ref[max_exact + 1 + kk, h], val)
        val = (val - last) * LOG2E
        if t == 0:
            val = jnp.where(dist >= 0, val, -BIG)
        o_ref[0, t] = val
    o_ref[0, NEAR_TILES] = jnp.zeros((MOBA_BLOCK, MOBA_BLOCK), F32)


def _bias_tiles(rel_bias):
    return pl.pallas_call(
        _bias_tiles_kernel,
        grid=(ATT_HEADS,),
        in_specs=[pl.BlockSpec(memory_space=pltpu.SMEM)],
        out_specs=pl.BlockSpec((1, NEAR_TILES + 1, MOBA_BLOCK, MOBA_BLOCK), lambda h: (h, 0, 0, 0)),
        out_shape=jax.ShapeDtypeStruct((ATT_HEADS, NEAR_TILES + 1, MOBA_BLOCK, MOBA_BLOCK), F32),
        compiler_params=pltpu.CompilerParams(dimension_semantics=("arbitrary",)),
        name="bias_tiles",
    )(rel_bias)


def _attn_kernel(q_ref, k_ref, vt_ref, bias_ref, o_ref,
                 km_ref, kms_ref, qa_ref, sel_ref, acc_ref, s_ref, mt_ref, p_ref, alpha_ref, *, nb):
    step = pl.program_id(2)
    j_last = step * Q_BLOCKS + (Q_BLOCKS - 1)
    chains = [(h, a) for h in range(Q_BLOCKS) for a in range(2)]
    lane = lax.broadcasted_iota(jnp.int32, (1, PAIR_W), 1)

    @pl.when(step == 0)
    def _():
        def body(j, c):
            kb = k_ref[0, 0, pl.ds(pl.multiple_of(j * MOBA_BLOCK, MOBA_BLOCK), MOBA_BLOCK), :]
            km_ref[pl.ds(j, 1), :] = jnp.sum(kb.astype(F32), axis=0, keepdims=True) * (1.0 / MOBA_BLOCK)
            return c
        lax.fori_loop(0, nb, body, 0)
        km = km_ref[...]
        km_hi = km.astype(BF16)
        km_lo = (km - km_hi.astype(F32)).astype(BF16)
        for a in range(2):
            in_head = (lane >= a * HEAD_DIM) & (lane < (a + 1) * HEAD_DIM)
            kms_ref[(2 * a) * nb:(2 * a + 1) * nb, :] = jnp.where(in_head, km_hi, jnp.zeros_like(km_hi))
            kms_ref[(2 * a + 1) * nb:(2 * a + 2) * nb, :] = jnp.where(in_head, km_lo, jnp.zeros_like(km_lo))

        n_q = SEL_CHUNK * MOBA_BLOCK
        def select_chunk(ci, c):
            q = q_ref[0, 0, pl.ds(pl.multiple_of(ci * n_q, n_q), n_q), :]
            g = _dot_nt(kms_ref[...], q)
            gate = jnp.concatenate([g[(2 * a) * nb:(2 * a + 1) * nb] + g[(2 * a + 1) * nb:(2 * a + 2) * nb]
                                    for a in range(2)], axis=1)
            pos = lax.broadcasted_iota(jnp.int32, (1, n_q), 1)
            own = ci * SEL_CHUNK + jnp.right_shift(pos, MOBA_BLOCK.bit_length() - 1)
            own = jnp.concatenate([own, own], axis=1)
            blk = lax.broadcasted_iota(jnp.int32, gate.shape, 0)
            gate = jnp.where(blk < own, gate, -jnp.inf)
            sel = blk == own
            for _ in range(MOBA_TOPK):
                top = jnp.max(gate, axis=0, keepdims=True)
                first = jnp.min(jnp.where(gate == top, blk, nb), axis=0, keepdims=True)
                pick = (blk == first) & (top > -jnp.inf)
                sel = sel | pick
                gate = jnp.where(pick, -jnp.inf, gate)
            sel_f = jnp.where(sel, 1.0, 0.0)
            for a in range(2):
                for b in range(SEL_CHUNK):
                    lo = a * n_q + b * MOBA_BLOCK
                    sel_ref[ci * SEL_CHUNK + b, a] = sel_f[:, lo:lo + MOBA_BLOCK]
            return c
        lax.fori_loop(0, nb // SEL_CHUNK, select_chunk, 0)

    for c, (h, a) in enumerate(chains):
        q = q_ref[0, 0, pl.ds(pl.multiple_of((step * Q_BLOCKS + h) * MOBA_BLOCK, MOBA_BLOCK), MOBA_BLOCK), :]
        qa_ref[c] = jnp.where((lane >= a * HEAD_DIM) & (lane < (a + 1) * HEAD_DIM), q, jnp.zeros_like(q))

    def block_of(t):
        return jnp.maximum(j_last - t, 0)

    def key_block(j):
        return k_ref[0, 0, pl.ds(pl.multiple_of(j * MOBA_BLOCK, MOBA_BLOCK), MOBA_BLOCK), :]

    def scores(t, c, kj, near):
        h, a = chains[c]
        s = _dot_nt(kj, qa_ref[c])
        offset = t - (Q_BLOCKS - 1 - h)
        if isinstance(t, int):
            if offset < NEAR_TILES:
                s = s + bias_ref[a, max(offset, 0)]
        elif near:
            s = s + bias_ref[a, jnp.clip(offset, 0, NEAR_TILES)]
        return s

    def visits(t, c):
        return not isinstance(t, int) or t - (Q_BLOCKS - 1 - chains[c][0]) >= 0

    def chosen_row(t, c, j):
        h, a = chains[c]
        return sel_ref[step * Q_BLOCKS + h, a, pl.ds(j, 1), :] > jnp.where(t <= j_last, 0.5, 2.0)

    def score_stage(t, slot, near):
        kj = key_block(block_of(t))
        for c in range(len(chains)):
            s = scores(t, c, kj, near)
            s_ref[slot, c] = s
            mt_ref[slot, c] = jnp.max(s, axis=0, keepdims=True)

    def softmax_stage(t, slot, m_prev):
        j = block_of(t)
        m_next, alphas, pvs = [], [], []
        for c, (h, a) in enumerate(chains):
            chosen = chosen_row(t, c, j)
            m_new = jnp.where(chosen, jnp.maximum(m_prev[c], mt_ref[slot, c]), m_prev[c])
            alphas.append(jnp.exp2(m_prev[c] - m_new))
            p = jnp.exp2(s_ref[slot, c] - jnp.where(chosen, m_new, BIG)).astype(BF16)
            pvs.append(_dot(vt_ref[0, 0, j, a], p))
            m_next.append(m_new)
        for c in range(len(chains)):
            acc_ref[c] = alphas[c] * acc_ref[c] + pvs[c]
        return tuple(m_next)

    def exact_pair(tt, m):
        t = 2 * tt
        score_stage(t + 1, 1, True)
        m = softmax_stage(t, 0, m)
        score_stage(t + 2, 0, True)
        m = softmax_stage(t + 1, 1, m)
        return m

    def lazy_stage(carry, near, score=None, pv=None):
        r, excess = list(carry[0]), list(carry[1])
        if score is not None:
            t, slot = score
            j = block_of(t)
            kj = key_block(j)
            for c in range(len(chains)):
                if not visits(t, c):
                    continue
                s = scores(t, c, kj, near)
                chosen = chosen_row(t, c, j)
                tile_max = jnp.max(s, axis=0, keepdims=True)
                p_ref[slot, c] = jnp.exp2(s - jnp.where(chosen, r[c], BIG)).astype(BF16)
                r_new = jnp.where(chosen, jnp.maximum(r[c], tile_max), r[c])
                alpha_ref[slot, c] = jnp.exp2(r[c] - r_new)
                excess[c] = jnp.maximum(excess[c], jnp.where(chosen, tile_max - r[c], -BIG))
                r[c] = r_new
        if pv is not None:
            t, slot = pv
            j = block_of(t)
            live = [c for c in range(len(chains)) if visits(t, c)]
            pvs = [_dot(vt_ref[0, 0, j, chains[c][1]], p_ref[slot, c]) for c in live]
            for c, prod in zip(live, pvs):
                acc_ref[c] = (acc_ref[c] + prod) * alpha_ref[slot, c]
        return tuple(r), tuple(excess)

    def lazy_pairs(it, carry, *, near, first_pair, pairs_per_trip, single_trip=False):
        if single_trip:
            t = 2 * first_pair
            carry = lazy_stage(carry, near, score=(t, 0))
        else:
            t = 2 * (first_pair + it * pairs_per_trip)
        for _ in range(pairs_per_trip):
            carry = lazy_stage(carry, near, score=(t + 1, 1), pv=(t, 0))
            carry = lazy_stage(carry, near, score=(t + 2, 0), pv=(t + 1, 1))
            t = t + 2
        return carry

    def finalize():
        for h in range(Q_BLOCKS):
            outs = []
            for a in range(2):
                acc = acc_ref[2 * h + a]
                outs.append(acc[0:HEAD_DIM] * (1.0 / acc[HEAD_DIM:HEAD_DIM + 1]))
            o_ref[0, 0, h * MOBA_BLOCK:(h + 1) * MOBA_BLOCK, :] = jnp.concatenate(outs, axis=0).T.astype(BF16)

    near_pairs = (NEAR_TILES + Q_BLOCKS) // 2
    n_pairs = (j_last + 2) // 2
    far_pairs = jnp.maximum(n_pairs - near_pairs, 0)
    lowest = (jnp.full((1, MOBA_BLOCK), -BIG, F32),) * len(chains)

    ones_rows = jnp.ones((BF16_ROWS, PAIR_W), BF16)
    seeds = []
    for c, (h, a) in enumerate(chains):
        own_scores = _dot_nt(ones_rows, qa_ref[c] * key_block(step * Q_BLOCKS + h))
        seeds.append(own_scores[0:1] + bias_ref[a, 0, 0:1, 0:1])

    acc_ref[...] = jnp.zeros_like(acc_ref)
    carry = (tuple(seeds), lowest)
    counts = {(Q_BLOCKS * (i + 1) + 1) // 2 for i in range(nb // Q_BLOCKS)}
    for n in sorted(c for c in counts if c < near_pairs) + [near_pairs]:
        runs = (n_pairs >= n) if n == near_pairs else (n_pairs == n)
        carry = lax.fori_loop(0, jnp.where(runs, 1, 0),
                              functools.partial(lazy_pairs, near=True, first_pair=0, pairs_per_trip=n,
                                                single_trip=True), carry)
    first, left = near_pairs, far_pairs
    for pairs_per_trip in FAR_TRIP_PAIRS:
        trips = left // pairs_per_trip
        carry = lax.fori_loop(0, trips, functools.partial(lazy_pairs, near=False, first_pair=first,
                                                          pairs_per_trip=pairs_per_trip), carry)
        first, left = first + trips * pairs_per_trip, left - trips * pairs_per_trip
    finalize()

    worst = functools.reduce(jnp.maximum, carry[1])
    @pl.when(jnp.max(worst) > LAZY_MAX_EXCESS)
    def _():
        acc_ref[...] = jnp.zeros_like(acc_ref)
        score_stage(0, 0, True)
        lax.fori_loop(0, n_pairs, exact_pair, lowest)
        finalize()


def _attention(q, k, vt, bias_t):
    batch, _, seq, _ = q.shape
    nb = seq // MOBA_BLOCK
    tq = Q_BLOCKS * MOBA_BLOCK
    n_chains = 2 * Q_BLOCKS
    return pl.pallas_call(
        functools.partial(_attn_kernel, nb=nb),
        grid=(batch, HEAD_PAIRS, seq // tq),
        in_specs=[
            pl.BlockSpec((1, 1, seq, PAIR_W), lambda b, p, i: (b, p, 0, 0)),
            pl.BlockSpec((1, 1, seq, PAIR_W), lambda b, p, i: (b, p, 0, 0)),
            pl.BlockSpec((1, 1, nb, 2, V_ROWS, MOBA_BLOCK), lambda b, p, i: (b, p, 0, 0, 0, 0)),
            pl.BlockSpec((2, NEAR_TILES + 1, MOBA_BLOCK, MOBA_BLOCK), lambda b, p, i: (p, 0, 0, 0)),
        ],
        out_specs=pl.BlockSpec((1, 1, tq, PAIR_W), lambda b, p, i: (b, p, i, 0)),
        out_shape=jax.ShapeDtypeStruct((batch, HEAD_PAIRS, seq, PAIR_W), BF16),
        scratch_shapes=[
            pltpu.VMEM((nb, PAIR_W), F32),
            pltpu.VMEM((4 * nb, PAIR_W), BF16),
            pltpu.VMEM((n_chains, MOBA_BLOCK, PAIR_W), BF16),
            pltpu.VMEM((nb, 2, nb, MOBA_BLOCK), F32),
            pltpu.VMEM((n_chains, V_ROWS, MOBA_BLOCK), F32),
            pltpu.VMEM((2, n_chains, MOBA_BLOCK, MOBA_BLOCK), F32),
            pltpu.VMEM((2, n_chains, 1, MOBA_BLOCK), F32),
            pltpu.VMEM((2, n_chains, MOBA_BLOCK, MOBA_BLOCK), BF16),
            pltpu.VMEM((2, n_chains, 1, MOBA_BLOCK), F32),
        ],
        compiler_params=pltpu.CompilerParams(dimension_semantics=("arbitrary", "arbitrary", "arbitrary"),
                                             vmem_limit_bytes=VMEM_LIMIT),
        name="moba_attention",
    )(q, k, vt, bias_t)


def _post_kernel(x_ref, u_ref, vn_ref, sga_ref, sgb_ref, yb_ref, p_ref,
                 ws_ref, bs_ref, wo_ref, gffn_ref, w1_ref, w2_ref, gple_ref, wpg_ref, wpp_ref, gfin_ref,
                 o_ref, merged_ref, *, tm, final_norm):
    row = lax.broadcasted_iota(jnp.int32, (SGU_CHUNK, SGU_CHUNK), 0)
    col = lax.broadcasted_iota(jnp.int32, (SGU_CHUNK, SGU_CHUNK), 1)
    for g in range(SGU_GROUPS):
        w = jnp.where(row >= col, ws_ref[g], 0.0).astype(BF16)
        b = bs_ref[g]
        cols = slice(g * LANES, (g + 1) * LANES)
        for t in range(tm // SGU_CHUNK):
            rows = slice(t * SGU_CHUNK, (t + 1) * SGU_CHUNK)
            mixed = _dot(w, vn_ref[rows, cols]) + b
            y_a = u_ref[rows, cols].astype(F32) * mixed
            y_b = yb_ref[0, g, rows, :].astype(F32)
            merged = sga_ref[rows, cols].astype(F32) * y_a + sgb_ref[rows, cols].astype(F32) * y_b
            merged_ref[rows, cols] = merged.astype(BF16)

    x1 = x_ref[...] + _dot(merged_ref[...], wo_ref[...])

    h = _rms(x1, gffn_ref[...]).astype(BF16)
    x2 = x1
    for c in range(D_FF // D_MODEL):
        cs = slice(c * D_MODEL, (c + 1) * D_MODEL)
        a = jnp.square(jnp.maximum(_dot(h, w1_ref[:, cs]), 0.0)).astype(BF16)
        x2 = x2 + _dot(a, w2_ref[cs, :])

    gate = jax.nn.sigmoid(_dot(_rms(x2, gple_ref[...]).astype(BF16), wpg_ref[...]))
    x3 = x2 + gate * _dot(p_ref[...].astype(BF16), wpp_ref[...])
    o_ref[...] = _rms(x3, gfin_ref[...]) if final_norm else x3


def _post(x2, u, vn, sga, sgb, yb, p2, ws, bs, wo, gffn, w1, w2, gple, wpg, wpp, gfin, *, seq, tm, final_norm):
    n = x2.shape[0]
    tiles_per_seq = seq // tm
    row = lambda r: (r, 0)
    resident = functools.partial(pl.BlockSpec, pipeline_mode=pl.Buffered(1))
    c2 = lambda r: (0, 0)
    c3 = lambda r: (0, 0, 0)
    tokb = pl.BlockSpec((tm, D_MODEL), row)
    return pl.pallas_call(
        functools.partial(_post_kernel, tm=tm, final_norm=final_norm),
        grid=(n // tm,),
        in_specs=[
            tokb, tokb, tokb, tokb, tokb,
            pl.BlockSpec((1, HEAD_PAIRS, tm, PAIR_W), lambda r: (r // tiles_per_seq, 0, r % tiles_per_seq, 0)),
            pl.BlockSpec((tm, PLE_DIM), row),
            resident((SGU_GROUPS, SGU_CHUNK, SGU_CHUNK), c3),
            resident((SGU_GROUPS, SGU_CHUNK, 1), c3),
            resident((D_MODEL, D_MODEL), c2),
            resident((1, D_MODEL), c2),
            resident((D_MODEL, D_FF), c2),
            resident((D_FF, D_MODEL), c2),
            resident((1, D_MODEL), c2),
            resident((D_MODEL, D_MODEL), c2),
            resident((PLE_DIM, D_MODEL), c2),
            resident((1, D_MODEL), c2),
        ],
        out_specs=pl.BlockSpec((tm, D_MODEL), row),
        out_shape=jax.ShapeDtypeStruct((n, D_MODEL), F32),
        scratch_shapes=[pltpu.VMEM((tm, D_MODEL), BF16)],
        compiler_params=pltpu.CompilerParams(dimension_semantics=("arbitrary",),
                                             vmem_limit_bytes=VMEM_LIMIT),
        name="post",
    )(x2, u, vn, sga, sgb, yb, p2, ws, bs, wo, gffn, w1, w2, gple, wpg, wpp, gfin)


def kernel(x, p, norm_mix_g, w_in, w_sgu_spatial, b_sgu_spatial, ln_v_g, ln_v_b, rel_bias, w_out, norm_ffn_g,
           w_ff1, w_ff2, norm_ple_g, w_ple_gate, w_ple_proj, norm_final_g):
    batch, seq, _ = x.shape
    depth = w_in.shape[0]
    n = batch * seq
    assert seq % (Q_BLOCKS * MOBA_BLOCK) == 0 and seq // MOBA_BLOCK >= MOBA_TOPK
    tm_in = 512
    tm_post = 512
    vec = lambda g: g.reshape(1, D_MODEL)

    bias_t = _bias_tiles(rel_bias)
    x2 = x.reshape(n, D_MODEL)
    for i in range(depth):
        wi = w_in[i].astype(BF16)
        wvt = wi[:, 4 * D_MODEL:5 * D_MODEL].T
        u, vn, q, k, vt, sga, sgb = _in_proj(
            x2, vec(norm_mix_g[i]), wi, wvt, vec(ln_v_g[i]), vec(ln_v_b[i]),
            batch=batch, seq=seq, tm=tm_in)
        yb = _attention(q, k, vt, bias_t)
        x2 = _post(
            x2, u, vn, sga, sgb, yb, p[i].reshape(n, PLE_DIM),
            w_sgu_spatial[i], b_sgu_spatial[i].reshape(SGU_GROUPS, SGU_CHUNK, 1),
            w_out[i].astype(BF16), vec(norm_ffn_g[i]), w_ff1[i].astype(BF16), w_ff2[i].astype(BF16),
            vec(norm_ple_g[i]), w_ple_gate[i].astype(BF16), w_ple_proj[i].astype(BF16),
            vec(norm_final_g), seq=seq, tm=tm_post, final_norm=(i == depth - 1))
    return x2.reshape(batch, seq, D_MODEL)
```

```python
import functools
import math

import jax
import jax.numpy as jnp
from jax import lax
from jax.experimental import pallas as pl
from jax.experimental.pallas import tpu as pltpu

D_MODEL = 1024
PLE_DIM = 256
SGU_CHUNK = 128
SGU_GROUPS = 8
ATT_HEADS = 16
HEAD_DIM = 64
MOBA_BLOCK = 256
MOBA_TOPK = 3
REL_BUCKETS = 32
REL_MAX_DIST = 1024
D_FF = 4 * D_MODEL
EPS = 1e-6

LANES = 128
BF16_ROWS = 16
HEAD_PAIRS = ATT_HEADS // 2
PAIR_W = 2 * HEAD_DIM
V_ROWS = HEAD_DIM + BF16_ROWS
Q_BLOCKS = 4
FAR_TRIP_PAIRS = (4, 2, 1)
SEL_CHUNK = 8
LAZY_MAX_EXCESS = 16.0
BIG = 1e30
VMEM_LIMIT = 60 * 1024 * 1024
LOG2E = math.log2(math.e)

F32 = jnp.float32
BF16 = jnp.bfloat16


def _bucket_thresholds():
    max_exact = REL_BUCKETS // 2
    n_log = REL_BUCKETS - max_exact
    ratio = REL_MAX_DIST // max_exact
    out = []
    for k in range(1, n_log):
        d = max_exact
        target = (max_exact ** n_log) * (ratio ** k)
        while d ** n_log < target:
            d += 1
        out.append(d)
    return tuple(out)


BUCKET_THRESHOLDS = _bucket_thresholds()
NEAR_TILES = -(-(BUCKET_THRESHOLDS[-1] + MOBA_BLOCK - 1) // MOBA_BLOCK)
assert NEAR_TILES == 5


def _dot(a, b):
    return jnp.dot(a, b, preferred_element_type=F32)


def _dot_nt(a, b):
    return lax.dot_general(a, b, (((1,), (1,)), ((), ())), preferred_element_type=F32)


def _rms(x, g):
    return x * lax.rsqrt(jnp.mean(x * x, axis=-1, keepdims=True) + EPS) * g


def _in_proj_kernel(x_ref, g_ref, wa_ref, wvt_ref, wga_ref, wgb_ref, lng_ref, lnb_ref,
                    u_ref, vn_ref, q_ref, k_ref, vt_ref, sga_ref, sgb_ref, *, tm):
    h = _rms(x_ref[...], g_ref[...]).astype(BF16)

    u_ref[...] = jax.nn.gelu(_dot(h, wa_ref[:, 0:D_MODEL])).astype(BF16)

    gv = jax.nn.gelu(_dot(h, wa_ref[:, D_MODEL:2 * D_MODEL]))
    mu = jnp.mean(gv, axis=-1, keepdims=True)
    gc = gv - mu
    vn = gc * lax.rsqrt(jnp.mean(gc * gc, axis=-1, keepdims=True) + EPS)
    vn_ref[...] = (vn * lng_ref[...] + lnb_ref[...]).astype(BF16)

    q = _dot(h, wa_ref[:, 2 * D_MODEL:3 * D_MODEL]) * (HEAD_DIM ** -0.5 * LOG2E)
    k = _dot(h, wa_ref[:, 3 * D_MODEL:4 * D_MODEL])
    for p in range(HEAD_PAIRS):
        q_ref[0, p] = q[:, p * PAIR_W:(p + 1) * PAIR_W].astype(BF16)
        k_ref[0, p] = k[:, p * PAIR_W:(p + 1) * PAIR_W].astype(BF16)

    vt = _dot_nt(wvt_ref[...], h).astype(BF16)
    ones = jnp.ones((BF16_ROWS, MOBA_BLOCK), BF16)
    for p in range(HEAD_PAIRS):
        for jb in range(tm // MOBA_BLOCK):
            for a in range(2):
                r0 = p * PAIR_W + a * HEAD_DIM
                vt_ref[0, p, jb, a, 0:HEAD_DIM, :] = vt[r0:r0 + HEAD_DIM, jb * MOBA_BLOCK:(jb + 1) * MOBA_BLOCK]
                vt_ref[0, p, jb, a, HEAD_DIM:V_ROWS, :] = ones

    sga_ref[...] = jax.nn.sigmoid(_dot(h, wga_ref[...])).astype(BF16)
    sgb_ref[...] = jax.nn.sigmoid(_dot(h, wgb_ref[...])).astype(BF16)


def _in_proj(x2, g, w, wvt, lng, lnb, *, batch, seq, tm):
    n = x2.shape[0]
    tiles_per_seq = seq // tm
    nb = seq // MOBA_BLOCK
    row = lambda r: (r, 0)
    const = lambda r: (0, 0)
    resident = functools.partial(pl.BlockSpec, pipeline_mode=pl.Buffered(1))
    tok = jax.ShapeDtypeStruct((n, D_MODEL), BF16)
    pair_major = jax.ShapeDtypeStruct((batch, HEAD_PAIRS, seq, PAIR_W), BF16)
    return pl.pallas_call(
        functools.partial(_in_proj_kernel, tm=tm),
        grid=(n // tm,),
        in_specs=[
            pl.BlockSpec((tm, D_MODEL), row),
            resident((1, D_MODEL), const),
            resident((D_MODEL, 4 * D_MODEL), const),
            resident((D_MODEL, D_MODEL), const),
            resident((D_MODEL, D_MODEL), lambda r: (0, 5)),
            resident((D_MODEL, D_MODEL), lambda r: (0, 6)),
            resident((1, D_MODEL), const),
            resident((1, D_MODEL), const),
        ],
        out_specs=[
            pl.BlockSpec((tm, D_MODEL), row),
            pl.BlockSpec((tm, D_MODEL), row),
            pl.BlockSpec((1, HEAD_PAIRS, tm, PAIR_W), lambda r: (r // tiles_per_seq, 0, r % tiles_per_seq, 0)),
            pl.BlockSpec((1, HEAD_PAIRS, tm, PAIR_W), lambda r: (r // tiles_per_seq, 0, r % tiles_per_seq, 0)),
            pl.BlockSpec((1, HEAD_PAIRS, tm // MOBA_BLOCK, 2, V_ROWS, MOBA_BLOCK),
                         lambda r: (r // tiles_per_seq, 0, r % tiles_per_seq, 0, 0, 0)),
            pl.BlockSpec((tm, D_MODEL), row),
            pl.BlockSpec((tm, D_MODEL), row),
        ],
        out_shape=[tok, tok, pair_major, pair_major,
                   jax.ShapeDtypeStruct((batch, HEAD_PAIRS, nb, 2, V_ROWS, MOBA_BLOCK), BF16),
                   tok, tok],
        compiler_params=pltpu.CompilerParams(dimension_semantics=("arbitrary",),
                                             vmem_limit_bytes=VMEM_LIMIT),
        name="in_proj",
    )(x2, g, w, wvt, w, w, lng, lnb)


def _bias_tiles_kernel(tab_ref, o_ref):
    h = pl.program_id(0)
    key = lax.broadcasted_iota(jnp.int32, (MOBA_BLOCK, MOBA_BLOCK), 0)
    qry = lax.broadcasted_iota(jnp.int32, (MOBA_BLOCK, MOBA_BLOCK), 1)
    max_exact = REL_BUCKETS // 2
    last = tab_ref[REL_BUCKETS - 1, h]
    for t in range(NEAR_TILES):
        dist = t * MOBA_BLOCK + qry - key
        n = jnp.maximum(dist, 0)
        val = jnp.full((MOBA_BLOCK, MOBA_BLOCK), tab_ref[0, h], F32)
        for b in range(1, max_exact + 1):
            val = jnp.where(n >= b, tab_ref[b, h], val)
        for kk, thr in enumerate(BUCKET_THRESHOLDS):
            val = jnp.where(n >= thr, tab_ref[max_exact + 1 + kk, h], val)
        val = (val - last) * LOG2E
        if t == 0:
            val = jnp.where(dist >= 0, val, -BIG)
        o_ref[0, t] = val
    o_ref[0, NEAR_TILES] = jnp.zeros((MOBA_BLOCK, MOBA_BLOCK), F32)


def _bias_tiles(rel_bias):
    return pl.pallas_call(
        _bias_tiles_kernel,
        grid=(ATT_HEADS,),
        in_specs=[pl.BlockSpec(memory_space=pltpu.SMEM)],
        out_specs=pl.BlockSpec((1, NEAR_TILES + 1, MOBA_BLOCK, MOBA_BLOCK), lambda h: (h, 0, 0, 0)),
        out_shape=jax.ShapeDtypeStruct((ATT_HEADS, NEAR_TILES + 1, MOBA_BLOCK, MOBA_BLOCK), F32),
        compiler_params=pltpu.CompilerParams(dimension_semantics=("arbitrary",)),
        name="bias_tiles",
    )(rel_bias)


def _attn_kernel(q_ref, k_ref, vt_ref, bias_ref, o_ref,
                 km_ref, kms_ref, qa_ref, sel_ref, acc_ref, s_ref, mt_ref, p_ref, alpha_ref, *, nb):
    step = pl.program_id(2)
    j_last = step * Q_BLOCKS + (Q_BLOCKS - 1)
    chains = [(h, a) for h in range(Q_BLOCKS) for a in range(2)]
    lane = lax.broadcasted_iota(jnp.int32, (1, PAIR_W), 1)

    @pl.when(step == 0)
    def _():
        def body(j, c):
            kb = k_ref[0, 0, pl.ds(pl.multiple_of(j * MOBA_BLOCK, MOBA_BLOCK), MOBA_BLOCK), :]
            km_ref[pl.ds(j, 1), :] = jnp.sum(kb.astype(F32), axis=0, keepdims=True) * (1.0 / MOBA_BLOCK)
            return c
        lax.fori_loop(0, nb, body, 0)
        km = km_ref[...]
        km_hi = km.astype(BF16)
        km_lo = (km - km_hi.astype(F32)).astype(BF16)
        for a in range(2):
            in_head = (lane >= a * HEAD_DIM) & (lane < (a + 1) * HEAD_DIM)
            kms_ref[(2 * a) * nb:(2 * a + 1) * nb, :] = jnp.where(in_head, km_hi, jnp.zeros_like(km_hi))
            kms_ref[(2 * a + 1) * nb:(2 * a + 2) * nb, :] = jnp.where(in_head, km_lo, jnp.zeros_like(km_lo))

        n_q = SEL_CHUNK * MOBA_BLOCK
        def select_chunk(ci, c):
            q = q_ref[0, 0, pl.ds(pl.multiple_of(ci * n_q, n_q), n_q), :]
            g = _dot_nt(kms_ref[...], q)
            gate = jnp.concatenate([g[(2 * a) * nb:(2 * a + 1) * nb] + g[(2 * a + 1) * nb:(2 * a + 2) * nb]
                                    for a in range(2)], axis=1)
            pos = lax.broadcasted_iota(jnp.int32, (1, n_q), 1)
            own = ci * SEL_CHUNK + jnp.right_shift(pos, MOBA_BLOCK.bit_length() - 1)
            own = jnp.concatenate([own, own], axis=1)
            blk = lax.broadcasted_iota(jnp.int32, gate.shape, 0)
            gate = jnp.where(blk < own, gate, -jnp.inf)
            sel = blk == own
            for _ in range(MOBA_TOPK):
                top = jnp.max(gate, axis=0, keepdims=True)
                first = jnp.min(jnp.where(gate == top, blk, nb), axis=0, keepdims=True)
                pick = (blk == first) & (top > -jnp.inf)
                sel = sel | pick
                gate = jnp.where(pick, -jnp.inf, gate)
            sel_f = jnp.where(sel, 1.0, 0.0)
            for a in range(2):
                for b in range(SEL_CHUNK):
                    lo = a * n_q + b * MOBA_BLOCK
                    sel_ref[ci * SEL_CHUNK + b, a] = sel_f[:, lo:lo + MOBA_BLOCK]
            return c
        lax.fori_loop(0, nb // SEL_CHUNK, select_chunk, 0)

    for c, (h, a) in enumerate(chains):
        q = q_ref[0, 0, pl.ds(pl.multiple_of((step * Q_BLOCKS + h) * MOBA_BLOCK, MOBA_BLOCK), MOBA_BLOCK), :]
        qa_ref[c] = jnp.where((lane >= a * HEAD_DIM) & (lane < (a + 1) * HEAD_DIM), q, jnp.zeros_like(q))

    def block_of(t):
        return jnp.maximum(j_last - t, 0)

    def key_block(j):
        return k_ref[0, 0, pl.ds(pl.multiple_of(j * MOBA_BLOCK, MOBA_BLOCK), MOBA_BLOCK), :]

    def scores(t, c, kj, near):
        h, a = chains[c]
        s = _dot_nt(kj, qa_ref[c])
        offset = t - (Q_BLOCKS - 1 - h)
        if isinstance(t, int):
            if offset < NEAR_TILES:
                s = s + bias_ref[a, max(offset, 0)]
        elif near:
            s = s + bias_ref[a, jnp.clip(offset, 0, NEAR_TILES)]
        return s

    def visits(t, c):
        return not isinstance(t, int) or t - (Q_BLOCKS - 1 - chains[c][0]) >= 0

    def chosen_row(t, c, j):
        h, a = chains[c]
        return sel_ref[step * Q_BLOCKS + h, a, pl.ds(j, 1), :] > jnp.where(t <= j_last, 0.5, 2.0)

    def score_stage(t, slot, near):
        kj = key_block(block_of(t))
        for c in range(len(chains)):
            s = scores(t, c, kj, near)
            s_ref[slot, c] = s
            mt_ref[slot, c] = jnp.max(s, axis=0, keepdims=True)

    def softmax_stage(t, slot, m_prev):
        j = block_of(t)
        m_next, alphas, pvs = [], [], []
        for c, (h, a) in enumerate(chains):
            chosen = chosen_row(t, c, j)
            m_new = jnp.where(chosen, jnp.maximum(m_prev[c], mt_ref[slot, c]), m_prev[c])
            alphas.append(jnp.exp2(m_prev[c] - m_new))
            p = jnp.exp2(s_ref[slot, c] - jnp.where(chosen, m_new, BIG)).astype(BF16)
            pvs.append(_dot(vt_ref[0, 0, j, a], p))
            m_next.append(m_new)
        for c in range(len(chains)):
            acc_ref[c] = alphas[c] * acc_ref[c] + pvs[c]
        return tuple(m_next)

    def exact_pair(tt, m):
        t = 2 * tt
        score_stage(t + 1, 1, True)
        m = softmax_stage(t, 0, m)
        score_stage(t + 2, 0, True)
        m = softmax_stage(t + 1, 1, m)
        return m

    def lazy_stage(carry, near, score=None, pv=None):
        r, excess = list(carry[0]), list(carry[1])
        if score is not None:
            t, slot = score
            j = block_of(t)
            kj = key_block(j)
            for c in range(len(chains)):
                if not visits(t, c):
                    continue
                s = scores(t, c, kj, near)
                chosen = chosen_row(t, c, j)
                tile_max = jnp.max(s, axis=0, keepdims=True)
                p_ref[slot, c] = jnp.exp2(s - jnp.where(chosen, r[c], BIG)).astype(BF16)
                r_new = jnp.where(chosen, jnp.maximum(r[c], tile_max), r[c])
                alpha_ref[slot, c] = jnp.exp2(r[c] - r_new)
                excess[c] = jnp.maximum(excess[c], jnp.where(chosen, tile_max - r[c], -BIG))
                r[c] = r_new
        if pv is not None:
            t, slot = pv
            j = block_of(t)
            live = [c for c in range(len(chains)) if visits(t, c)]
            pvs = [_dot(vt_ref[0, 0, j, chains[c][1]], p_ref[slot, c]) for c in live]
            for c, prod in zip(live, pvs):
                acc_ref[c] = (acc_ref[c] + prod) * alpha_ref[slot, c]
        return tuple(r), tuple(excess)

    def lazy_pairs(it, carry, *, near, first_pair, pairs_per_trip, single_trip=False):
        if single_trip:
            t = 2 * first_pair
            carry = lazy_stage(carry, near, score=(t, 0))
        else:
            t = 2 * (first_pair + it * pairs_per_trip)
        for _ in range(pairs_per_trip):
            carry = lazy_stage(carry, near, score=(t + 1, 1), pv=(t, 0))
            carry = lazy_stage(carry, near, score=(t + 2, 0), pv=(t + 1, 1))
            t = t + 2
        return carry

    def finalize():
        for h in range(Q_BLOCKS):
            outs = []
            for a in range(2):
                acc = acc_ref[2 * h + a]
                outs.append(acc[0:HEAD_DIM] * (1.0 / acc[HEAD_DIM:HEAD_DIM + 1]))
            o_ref[0, 0, h * MOBA_BLOCK:(h + 1) * MOBA_BLOCK, :] = jnp.concatenate(outs, axis=0).T.astype(BF16)

    near_pairs = (NEAR_TILES + Q_BLOCKS) // 2
    n_pairs = (j_last + 2) // 2
    far_pairs = jnp.maximum(n_pairs - near_pairs, 0)
    lowest = (jnp.full((1, MOBA_BLOCK), -BIG, F32),) * len(chains)

    ones_rows = jnp.ones((BF16_ROWS, PAIR_W), BF16)
    seeds = []
    for c, (h, a) in enumerate(chains):
        own_scores = _dot_nt(ones_rows, qa_ref[c] * key_block(step * Q_BLOCKS + h))
        seeds.append(own_scores[0:1] + bias_ref[a, 0, 0:1, 0:1])

    acc_ref[...] = jnp.zeros_like(acc_ref)
    carry = (tuple(seeds), lowest)
    counts = {(Q_BLOCKS * (i + 1) + 1) // 2 for i in range(nb // Q_BLOCKS)}
    for n in sorted(c for c in counts if c < near_pairs) + [near_pairs]:
        runs = (n_pairs >= n) if n == near_pairs else (n_pairs == n)
        carry = lax.fori_loop(0, jnp.where(runs, 1, 0),
                              functools.partial(lazy_pairs, near=True, first_pair=0, pairs_per_trip=n,
                                                single_trip=True), carry)
    first, left = near_pairs, far_pairs
    for pairs_per_trip in FAR_TRIP_PAIRS:
        trips = left // pairs_per_trip
        carry = lax.fori_loop(0, trips, functools.partial(lazy_pairs, near=False, first_pair=first,
                                                          pairs_per_trip=pairs_per_trip), carry)
        first, left = first + trips * pairs_per_trip, left - trips * pairs_per_trip
    finalize()

    worst = functools.reduce(jnp.maximum, carry[1])
    @pl.when(jnp.max(worst) > LAZY_MAX_EXCESS)
    def _():
        acc_ref[...] = jnp.zeros_like(acc_ref)
        score_stage(0, 0, True)
        lax.fori_loop(0, n_pairs, exact_pair, lowest)
        finalize()


def _attention(q, k, vt, bias_t):
    batch, _, seq, _ = q.shape
    nb = seq // MOBA_BLOCK
    tq = Q_BLOCKS * MOBA_BLOCK
    n_chains = 2 * Q_BLOCKS
    return pl.pallas_call(
        functools.partial(_attn_kernel, nb=nb),
        grid=(batch, HEAD_PAIRS, seq // tq),
        in_specs=[
            pl.BlockSpec((1, 1, seq, PAIR_W), lambda b, p, i: (b, p, 0, 0)),
            pl.BlockSpec((1, 1, seq, PAIR_W), lambda b, p, i: (b, p, 0, 0)),
            pl.BlockSpec((1, 1, nb, 2, V_ROWS, MOBA_BLOCK), lambda b, p, i: (b, p, 0, 0, 0, 0)),
            pl.BlockSpec((2, NEAR_TILES + 1, MOBA_BLOCK, MOBA_BLOCK), lambda b, p, i: (p, 0, 0, 0)),
        ],
        out_specs=pl.BlockSpec((1, 1, tq, PAIR_W), lambda b, p, i: (b, p, i, 0)),
        out_shape=jax.ShapeDtypeStruct((batch, HEAD_PAIRS, seq, PAIR_W), BF16),
        scratch_shapes=[
            pltpu.VMEM((nb, PAIR_W), F32),
            pltpu.VMEM((4 * nb, PAIR_W), BF16),
            pltpu.VMEM((n_chains, MOBA_BLOCK, PAIR_W), BF16),
            pltpu.VMEM((nb, 2, nb, MOBA_BLOCK), F32),
            pltpu.VMEM((n_chains, V_ROWS, MOBA_BLOCK), F32),
            pltpu.VMEM((2, n_chains, MOBA_BLOCK, MOBA_BLOCK), F32),
            pltpu.VMEM((2, n_chains, 1, MOBA_BLOCK), F32),
            pltpu.VMEM((2, n_chains, MOBA_BLOCK, MOBA_BLOCK), BF16),
            pltpu.VMEM((2, n_chains, 1, MOBA_BLOCK), F32),
        ],
        compiler_params=pltpu.CompilerParams(dimension_semantics=("arbitrary", "arbitrary", "arbitrary"),
                                             vmem_limit_bytes=VMEM_LIMIT),
        name="moba_attention",
    )(q, k, vt, bias_t)


def _post_kernel(x_ref, u_ref, vn_ref, sga_ref, sgb_ref, yb_ref, p_ref,
                 ws_ref, bs_ref, wo_ref, gffn_ref, w1_ref, w2_ref, gple_ref, wpg_ref, wpp_ref, gfin_ref,
                 o_ref, merged_ref, *, tm, final_norm):
    row = lax.broadcasted_iota(jnp.int32, (SGU_CHUNK, SGU_CHUNK), 0)
    col = lax.broadcasted_iota(jnp.int32, (SGU_CHUNK, SGU_CHUNK), 1)
    for g in range(SGU_GROUPS):
        w = jnp.where(row >= col, ws_ref[g], 0.0).astype(BF16)
        b = bs_ref[g]
        cols = slice(g * LANES, (g + 1) * LANES)
        for t in range(tm // SGU_CHUNK):
            rows = slice(t * SGU_CHUNK, (t + 1) * SGU_CHUNK)
            mixed = _dot(w, vn_ref[rows, cols]) + b
            y_a = u_ref[rows, cols].astype(F32) * mixed
            y_b = yb_ref[0, g, rows, :].astype(F32)
            merged = sga_ref[rows, cols].astype(F32) * y_a + sgb_ref[rows, cols].astype(F32) * y_b
            merged_ref[rows, cols] = merged.astype(BF16)

    x1 = x_ref[...] + _dot(merged_ref[...], wo_ref[...])

    h = _rms(x1, gffn_ref[...]).astype(BF16)
    x2 = x1
    for c in range(D_FF // D_MODEL):
        cs = slice(c * D_MODEL, (c + 1) * D_MODEL)
        a = jnp.square(jnp.maximum(_dot(h, w1_ref[:, cs]), 0.0)).astype(BF16)
        x2 = x2 + _dot(a, w2_ref[cs, :])

    gate = jax.nn.sigmoid(_dot(_rms(x2, gple_ref[...]).astype(BF16), wpg_ref[...]))
    x3 = x2 + gate * _dot(p_ref[...].astype(BF16), wpp_ref[...])
    o_ref[...] = _rms(x3, gfin_ref[...]) if final_norm else x3


def _post(x2, u, vn, sga, sgb, yb, p2, ws, bs, wo, gffn, w1, w2, gple, wpg, wpp, gfin, *, seq, tm, final_norm):
    n = x2.shape[0]
    tiles_per_seq = seq // tm
    row = lambda r: (r, 0)
    resident = functools.partial(pl.BlockSpec, pipeline_mode=pl.Buffered(1))
    c2 = lambda r: (0, 0)
    c3 = lambda r: (0, 0, 0)
    tokb = pl.BlockSpec((tm, D_MODEL), row)
    return pl.pallas_call(
        functools.partial(_post_kernel, tm=tm, final_norm=final_norm),
        grid=(n // tm,),
        in_specs=[
            tokb, tokb, tokb, tokb, tokb,
            pl.BlockSpec((1, HEAD_PAIRS, tm, PAIR_W), lambda r: (r // tiles_per_seq, 0, r % tiles_per_seq, 0)),
            pl.BlockSpec((tm, PLE_DIM), row),
            resident((SGU_GROUPS, SGU_CHUNK, SGU_CHUNK), c3),
            resident((SGU_GROUPS, SGU_CHUNK, 1), c3),
            resident((D_MODEL, D_MODEL), c2),
            resident((1, D_MODEL), c2),
            resident((D_MODEL, D_FF), c2),
            resident((D_FF, D_MODEL), c2),
            resident((1, D_MODEL), c2),
            resident((D_MODEL, D_MODEL), c2),
            resident((PLE_DIM, D_MODEL), c2),
            resident((1, D_MODEL), c2),
        ],
        out_specs=pl.BlockSpec((tm, D_MODEL), row),
        out_shape=jax.ShapeDtypeStruct((n, D_MODEL), F32),
        scratch_shapes=[pltpu.VMEM((tm, D_MODEL), BF16)],
        compiler_params=pltpu.CompilerParams(dimension_semantics=("arbitrary",),
                                             vmem_limit_bytes=VMEM_LIMIT),
        name="post",
    )(x2, u, vn, sga, sgb, yb, p2, ws, bs, wo, gffn, w1, w2, gple, wpg, wpp, gfin)


def kernel(x, p, norm_mix_g, w_in, w_sgu_spatial, b_sgu_spatial, ln_v_g, ln_v_b, rel_bias, w_out, norm_ffn_g,
           w_ff1, w_ff2, norm_ple_g, w_ple_gate, w_ple_proj, norm_final_g):
    batch, seq, _ = x.shape
    depth = w_in.shape[0]
    n = batch * seq
    assert seq % (Q_BLOCKS * MOBA_BLOCK) == 0 and seq // MOBA_BLOCK >= MOBA_TOPK
    tm_in = 512
    tm_post = 512
    vec = lambda g: g.reshape(1, D_MODEL)

    bias_t = _bias_tiles(rel_bias)
    x2 = x.reshape(n, D_MODEL)
    for i in range(depth):
        wi = w_in[i].astype(BF16)
        wvt = wi[:, 4 * D_MODEL:5 * D_MODEL].T
        u, vn, q, k, vt, sga, sgb = _in_proj(
            x2, vec(norm_mix_g[i]), wi, wvt, vec(ln_v_g[i]), vec(ln_v_b[i]),
            batch=batch, seq=seq, tm=tm_in)
        yb = _attention(q, k, vt, bias_t)
        x2 = _post(
            x2, u, vn, sga, sgb, yb, p[i].reshape(n, PLE_DIM),
            w_sgu_spatial[i], b_sgu_spatial[i].reshape(SGU_GROUPS, SGU_CHUNK, 1),
            w_out[i].astype(BF16), vec(norm_ffn_g[i]), w_ff1[i].astype(BF16), w_ff2[i].astype(BF16),
            vec(norm_ple_g[i]), w_ple_gate[i].astype(BF16), w_ple_proj[i].astype(BF16),
            vec(norm_final_g), seq=seq, tm=tm_post, final_norm=(i == depth - 1))
    return x2.reshape(batch, seq, D_MODEL)
```

```python
import functools
import math

import jax
import jax.numpy as jnp
from jax import lax
from jax.experimental import pallas as pl
from jax.experimental.pallas import tpu as pltpu

D_MODEL = 1024
PLE_DIM = 256
SGU_CHUNK = 128
SGU_GROUPS = 8
ATT_HEADS = 16
HEAD_DIM = 64
MOBA_BLOCK = 256
MOBA_TOPK = 3
REL_BUCKETS = 32
REL_MAX_DIST = 1024
D_FF = 4 * D_MODEL
EPS = 1e-6

LANES = 128
BF16_ROWS = 16
HEAD_PAIRS = ATT_HEADS // 2
PAIR_W = 2 * HEAD_DIM
V_ROWS = HEAD_DIM + BF16_ROWS
Q_BLOCKS = 4
FAR_TRIP_PAIRS = (4, 2, 1)
SEL_CHUNK = 8
LAZY_MAX_EXCESS = 16.0
BIG = 1e30
VMEM_LIMIT = 60 * 1024 * 1024
LOG2E = math.log2(math.e)

F32 = jnp.float32
BF16 = jnp.bfloat16


def _bucket_thresholds():
    max_exact = REL_BUCKETS // 2
    n_log = REL_BUCKETS - max_exact
    ratio = REL_MAX_DIST // max_exact
    out = []
    for k in range(1, n_log):
        d = max_exact
        target = (max_exact ** n_log) * (ratio ** k)
        while d ** n_log < target:
            d += 1
        out.append(d)
    return tuple(out)


BUCKET_THRESHOLDS = _bucket_thresholds()
NEAR_TILES = -(-(BUCKET_THRESHOLDS[-1] + MOBA_BLOCK - 1) // MOBA_BLOCK)
assert NEAR_TILES == 5


def _dot(a, b):
    return jnp.dot(a, b, preferred_element_type=F32)


def _dot_nt(a, b):
    return lax.dot_general(a, b, (((1,), (1,)), ((), ())), preferred_element_type=F32)


def _rms(x, g):
    return x * lax.rsqrt(jnp.mean(x * x, axis=-1, keepdims=True) + EPS) * g


def _in_proj_kernel(x_ref, g_ref, wa_ref, wvt_ref, wga_ref, wgb_ref, lng_ref, lnb_ref,
                    u_ref, vn_ref, q_ref, k_ref, vt_ref, sga_ref, sgb_ref, *, tm):
    h = _rms(x_ref[...], g_ref[...]).astype(BF16)

    u_ref[...] = jax.nn.gelu(_dot(h, wa_ref[:, 0:D_MODEL])).astype(BF16)

    gv = jax.nn.gelu(_dot(h, wa_ref[:, D_MODEL:2 * D_MODEL]))
    mu = jnp.mean(gv, axis=-1, keepdims=True)
    gc = gv - mu
    vn = gc * lax.rsqrt(jnp.mean(gc * gc, axis=-1, keepdims=True) + EPS)
    vn_ref[...] = (vn * lng_ref[...] + lnb_ref[...]).astype(BF16)

    q = _dot(h, wa_ref[:, 2 * D_MODEL:3 * D_MODEL]) * (HEAD_DIM ** -0.5 * LOG2E)
    k = _dot(h, wa_ref[:, 3 * D_MODEL:4 * D_MODEL])
    for p in range(HEAD_PAIRS):
        q_ref[0, p] = q[:, p * PAIR_W:(p + 1) * PAIR_W].astype(BF16)
        k_ref[0, p] = k[:, p * PAIR_W:(p + 1) * PAIR_W].astype(BF16)

    vt = _dot_nt(wvt_ref[...], h).astype(BF16)
    ones = jnp.ones((BF16_ROWS, MOBA_BLOCK), BF16)
    for p in range(HEAD_PAIRS):
        for jb in range(tm // MOBA_BLOCK):
            for a in range(2):
                r0 = p * PAIR_W + a * HEAD_DIM
                vt_ref[0, p, jb, a, 0:HEAD_DIM, :] = vt[r0:r0 + HEAD_DIM, jb * MOBA_BLOCK:(jb + 1) * MOBA_BLOCK]
                vt_ref[0, p, jb, a, HEAD_DIM:V_ROWS, :] = ones

    sga_ref[...] = jax.nn.sigmoid(_dot(h, wga_ref[...])).astype(BF16)
    sgb_ref[...] = jax.nn.sigmoid(_dot(h, wgb_ref[...])).astype(BF16)


def _in_proj(x2, g, w, wvt, lng, lnb, *, batch, seq, tm):
    n = x2.shape[0]
    tiles_per_seq = seq // tm
    nb = seq // MOBA_BLOCK
    row = lambda r: (r, 0)
    const = lambda r: (0, 0)
    resident = functools.partial(pl.BlockSpec, pipeline_mode=pl.Buffered(1))
    tok = jax.ShapeDtypeStruct((n, D_MODEL), BF16)
    pair_major = jax.ShapeDtypeStruct((batch, HEAD_PAIRS, seq, PAIR_W), BF16)
    return pl.pallas_call(
        functools.partial(_in_proj_kernel, tm=tm),
        grid=(n // tm,),
        in_specs=[
            pl.BlockSpec((tm, D_MODEL), row),
            resident((1, D_MODEL), const),
            resident((D_MODEL, 4 * D_MODEL), const),
            resident((D_MODEL, D_MODEL), const),
            resident((D_MODEL, D_MODEL), lambda r: (0, 5)),
            resident((D_MODEL, D_MODEL), lambda r: (0, 6)),
            resident((1, D_MODEL), const),
            resident((1, D_MODEL), const),
        ],
        out_specs=[
            pl.BlockSpec((tm, D_MODEL), row),
            pl.BlockSpec((tm, D_MODEL), row),
            pl.BlockSpec((1, HEAD_PAIRS, tm, PAIR_W), lambda r: (r // tiles_per_seq, 0, r % tiles_per_seq, 0)),
            pl.BlockSpec((1, HEAD_PAIRS, tm, PAIR_W), lambda r: (r // tiles_per_seq, 0, r % tiles_per_seq, 0)),
            pl.BlockSpec((1, HEAD_PAIRS, tm // MOBA_BLOCK, 2, V_ROWS, MOBA_BLOCK),
                         lambda r: (r // tiles_per_seq, 0, r % tiles_per_seq, 0, 0, 0)),
            pl.BlockSpec((tm, D_MODEL), row),
            pl.BlockSpec((tm, D_MODEL), row),
        ],
        out_shape=[tok, tok, pair_major, pair_major,
                   jax.ShapeDtypeStruct((batch, HEAD_PAIRS, nb, 2, V_ROWS, MOBA_BLOCK), BF16),
                   tok, tok],
        compiler_params=pltpu.CompilerParams(dimension_semantics=("arbitrary",),
                                             vmem_limit_bytes=VMEM_LIMIT),
        name="in_proj",
    )(x2, g, w, wvt, w, w, lng, lnb)


def _bias_tiles_kernel(tab_ref, o_ref):
    h = pl.program_id(0)
    key = lax.broadcasted_iota(jnp.int32, (MOBA_BLOCK, MOBA_BLOCK), 0)
    qry = lax.broadcasted_iota(jnp.int32, (MOBA_BLOCK, MOBA_BLOCK), 1)
    max_exact = REL_BUCKETS // 2
    last = tab_ref[REL_BUCKETS - 1, h]
    for t in range(NEAR_TILES):
        dist = t * MOBA_BLOCK + qry - key
        n = jnp.maximum(dist, 0)
        val = jnp.full((MOBA_BLOCK, MOBA_BLOCK), tab_ref[0, h], F32)
        for b in range(1, max_exact + 1):
            val = jnp.where(n >= b, tab_ref[b, h], val)
        for kk, thr in enumerate(BUCKET_THRESHOLDS):
            val = jnp.where(n >= thr, tab_ref[max_exact + 1 + kk, h], val)
        val = (val - last) * LOG2E
        if t == 0:
            val = jnp.where(dist >= 0, val, -BIG)
        o_ref[0, t] = val
    o_ref[0, NEAR_TILES] = jnp.zeros((MOBA_BLOCK, MOBA_BLOCK), F32)


def _bias_tiles(rel_bias):
    return pl.pallas_call(
        _bias_tiles_kernel,
        grid=(ATT_HEADS,),
        in_specs=[pl.BlockSpec(memory_space=pltpu.SMEM)],
        out_specs=pl.BlockSpec((1, NEAR_TILES + 1, MOBA_BLOCK, MOBA_BLOCK), lambda h: (h, 0, 0, 0)),
        out_shape=jax.ShapeDtypeStruct((ATT_HEADS, NEAR_TILES + 1, MOBA_BLOCK, MOBA_BLOCK), F32),
        compiler_params=pltpu.CompilerParams(dimension_semantics=("arbitrary",)),
        name="bias_tiles",
    )(rel_bias)


def _attn_kernel(q_ref, k_ref, vt_ref, bias_ref, o_ref,
                 km_ref, kms_ref, qa_ref, sel_ref, seed_ref, acc_ref, s_ref, mt_ref, p_ref, alpha_ref, *, nb):
    step = pl.program_id(2)
    j_last = step * Q_BLOCKS + (Q_BLOCKS - 1)
    chains = [(h, a) for h in range(Q_BLOCKS) for a in range(2)]
    lane = lax.broadcasted_iota(jnp.int32, (1, PAIR_W), 1)

    @pl.when(step == 0)
    def _():
        def body(j, c):
            kb = k_ref[0, 0, pl.ds(pl.multiple_of(j * MOBA_BLOCK, MOBA_BLOCK), MOBA_BLOCK), :]
            km_ref[pl.ds(j, 1), :] = jnp.sum(kb.astype(F32), axis=0, keepdims=True) * (1.0 / MOBA_BLOCK)
            return c
        lax.fori_loop(0, nb, body, 0)
        km = km_ref[...]
        km_hi = km.astype(BF16)
        km_lo = (km - km_hi.astype(F32)).astype(BF16)
        for a in range(2):
            in_head = (lane >= a * HEAD_DIM) & (lane < (a + 1) * HEAD_DIM)
            kms_ref[(2 * a) * nb:(2 * a + 1) * nb, :] = jnp.where(in_head, km_hi, jnp.zeros_like(km_hi))
            kms_ref[(2 * a + 1) * nb:(2 * a + 2) * nb, :] = jnp.where(in_head, km_lo, jnp.zeros_like(km_lo))

        n_q = SEL_CHUNK * MOBA_BLOCK
        def select_chunk(ci, c):
            q = q_ref[0, 0, pl.ds(pl.multiple_of(ci * n_q, n_q), n_q), :]
            g = _dot_nt(kms_ref[...], q)
            gate = jnp.concatenate([g[(2 * a) * nb:(2 * a + 1) * nb] + g[(2 * a + 1) * nb:(2 * a + 2) * nb]
                                    for a in range(2)], axis=1)
            pos = lax.broadcasted_iota(jnp.int32, (1, n_q), 1)
            own = ci * SEL_CHUNK + jnp.right_shift(pos, MOBA_BLOCK.bit_length() - 1)
            own = jnp.concatenate([own, own], axis=1)
            blk = lax.broadcasted_iota(jnp.int32, gate.shape, 0)
            gate = jnp.where(blk < own, gate, -jnp.inf)
            sel = blk == own
            for _ in range(MOBA_TOPK):
                top = jnp.max(gate, axis=0, keepdims=True)
                first = jnp.min(jnp.where(gate == top, blk, nb), axis=0, keepdims=True)
                pick = (blk == first) & (top > -jnp.inf)
                sel = sel | pick
                gate = jnp.where(pick, -jnp.inf, gate)
            sel_f = jnp.where(sel, 1.0, 0.0)
            k_same = k_ref[0, 0, pl.ds(pl.multiple_of(ci * n_q, n_q), n_q), :]
            head_rows = jnp.where(lane // HEAD_DIM == lax.broadcasted_iota(jnp.int32, (BF16_ROWS, PAIR_W), 0),
                                  1.0, 0.0).astype(BF16)
            own_scores = _dot_nt(head_rows, q * k_same)
            for a in range(2):
                for b in range(SEL_CHUNK):
                    lo = a * n_q + b * MOBA_BLOCK
                    sel_ref[ci * SEL_CHUNK + b, a] = sel_f[:, lo:lo + MOBA_BLOCK]
                    seed_ref[ci * SEL_CHUNK + b, a] = (own_scores[a:a + 1, b * MOBA_BLOCK:(b + 1) * MOBA_BLOCK]
                                                       + bias_ref[a, 0, 0:1, 0:1])
            return c
        lax.fori_loop(0, nb // SEL_CHUNK, select_chunk, 0)

    for c, (h, a) in enumerate(chains):
        q = q_ref[0, 0, pl.ds(pl.multiple_of((step * Q_BLOCKS + h) * MOBA_BLOCK, MOBA_BLOCK), MOBA_BLOCK), :]
        qa_ref[c] = jnp.where((lane >= a * HEAD_DIM) & (lane < (a + 1) * HEAD_DIM), q, jnp.zeros_like(q))

    def block_of(t):
        return jnp.maximum(j_last - t, 0)

    def key_block(j):
        return k_ref[0, 0, pl.ds(pl.multiple_of(j * MOBA_BLOCK, MOBA_BLOCK), MOBA_BLOCK), :]

    def scores(t, c, kj, near):
        h, a = chains[c]
        s = _dot_nt(kj, qa_ref[c])
        offset = t - (Q_BLOCKS - 1 - h)
        if isinstance(t, int):
            if offset < NEAR_TILES:
                s = s + bias_ref[a, max(offset, 0)]
        elif near:
            s = s + bias_ref[a, jnp.clip(offset, 0, NEAR_TILES)]
        return s

    def visits(t, c):
        return not isinstance(t, int) or t - (Q_BLOCKS - 1 - chains[c][0]) >= 0

    def chosen_row(t, c, j):
        h, a = chains[c]
        return sel_ref[step * Q_BLOCKS + h, a, pl.ds(j, 1), :] > jnp.where(t <= j_last, 0.5, 2.0)

    def score_stage(t, slot, near):
        kj = key_block(block_of(t))
        for c in range(len(chains)):
            s = scores(t, c, kj, near)
            s_ref[slot, c] = s
            mt_ref[slot, c] = jnp.max(s, axis=0, keepdims=True)

    def softmax_stage(t, slot, m_prev):
        j = block_of(t)
        m_next, alphas, pvs = [], [], []
        for c, (h, a) in enumerate(chains):
            chosen = chosen_row(t, c, j)
            m_new = jnp.where(chosen, jnp.maximum(m_prev[c], mt_ref[slot, c]), m_prev[c])
            alphas.append(jnp.exp2(m_prev[c] - m_new))
            p = jnp.exp2(s_ref[slot, c] - jnp.where(chosen, m_new, BIG)).astype(BF16)
            pvs.append(_dot(vt_ref[0, 0, j, a], p))
            m_next.append(m_new)
        for c in range(len(chains)):
            acc_ref[c] = alphas[c] * acc_ref[c] + pvs[c]
        return tuple(m_next)

    def exact_pair(tt, m):
        t = 2 * tt
        score_stage(t + 1, 1, True)
        m = softmax_stage(t, 0, m)
        score_stage(t + 2, 0, True)
        m = softmax_stage(t + 1, 1, m)
        return m

    def lazy_stage(carry, near, score=None, pv=None):
        r, excess = list(carry[0]), list(carry[1])
        if score is not None:
            t, slot = score
            j = block_of(t)
            kj = key_block(j)
            for c in range(len(chains)):
                if not visits(t, c):
                    continue
                s = scores(t, c, kj, near)
                chosen = chosen_row(t, c, j)
                tile_max = jnp.max(s, axis=0, keepdims=True)
                p_ref[slot, c] = jnp.exp2(s - jnp.where(chosen, r[c], BIG)).astype(BF16)
                r_new = jnp.where(chosen, jnp.maximum(r[c], tile_max), r[c])
                alpha_ref[slot, c] = jnp.exp2(r[c] - r_new)
                excess[c] = jnp.maximum(excess[c], jnp.where(chosen, tile_max - r[c], -BIG))
                r[c] = r_new
        if pv is not None:
            t, slot = pv
            j = block_of(t)
            live = [c for c in range(len(chains)) if visits(t, c)]
            pvs = [_dot(vt_ref[0, 0, j, chains[c][1]], p_ref[slot, c]) for c in live]
            for c, prod in zip(live, pvs):
                acc_ref[c] = (acc_ref[c] + prod) * alpha_ref[slot, c]
        return tuple(r), tuple(excess)

    def lazy_pairs(it, carry, *, near, first_pair, pairs_per_trip, single_trip=False):
        if single_trip:
            t = 2 * first_pair
            carry = lazy_stage(carry, near, score=(t, 0))
        else:
            t = 2 * (first_pair + it * pairs_per_trip)
        for _ in range(pairs_per_trip):
            carry = lazy_stage(carry, near, score=(t + 1, 1), pv=(t, 0))
            carry = lazy_stage(carry, near, score=(t + 2, 0), pv=(t + 1, 1))
            t = t + 2
        return carry

    def finalize():
        for h in range(Q_BLOCKS):
            outs = []
            for a in range(2):
                acc = acc_ref[2 * h + a]
                outs.append(acc[0:HEAD_DIM] * (1.0 / acc[HEAD_DIM:HEAD_DIM + 1]))
            o_ref[0, 0, h * MOBA_BLOCK:(h + 1) * MOBA_BLOCK, :] = jnp.concatenate(outs, axis=0).T.astype(BF16)

    near_pairs = (NEAR_TILES + Q_BLOCKS) // 2
    n_pairs = (j_last + 2) // 2
    far_pairs = jnp.maximum(n_pairs - near_pairs, 0)
    lowest = (jnp.full((1, MOBA_BLOCK), -BIG, F32),) * len(chains)

    seeds = [seed_ref[step * Q_BLOCKS + h, a] for h, a in chains]

    acc_ref[...] = jnp.zeros_like(acc_ref)
    carry = (tuple(seeds), lowest)
    counts = {(Q_BLOCKS * (i + 1) + 1) // 2 for i in range(nb // Q_BLOCKS)}
    for n in sorted(c for c in counts if c < near_pairs) + [near_pairs]:
        runs = (n_pairs >= n) if n == near_pairs else (n_pairs == n)
        carry = lax.fori_loop(0, jnp.where(runs, 1, 0),
                              functools.partial(lazy_pairs, near=True, first_pair=0, pairs_per_trip=n,
                                                single_trip=True), carry)
    first, left = near_pairs, far_pairs
    for pairs_per_trip in FAR_TRIP_PAIRS:
        trips = left // pairs_per_trip
        carry = lax.fori_loop(0, trips, functools.partial(lazy_pairs, near=False, first_pair=first,
                                                          pairs_per_trip=pairs_per_trip), carry)
        first, left = first + trips * pairs_per_trip, left - trips * pairs_per_trip
    finalize()

    worst = functools.reduce(jnp.maximum, carry[1])
    @pl.when(jnp.max(worst) > LAZY_MAX_EXCESS)
    def _():
        acc_ref[...] = jnp.zeros_like(acc_ref)
        score_stage(0, 0, True)
        lax.fori_loop(0, n_pairs, exact_pair, lowest)
        finalize()


def _attention(q, k, vt, bias_t):
    batch, _, seq, _ = q.shape
    nb = seq // MOBA_BLOCK
    tq = Q_BLOCKS * MOBA_BLOCK
    n_chains = 2 * Q_BLOCKS
    return pl.pallas_call(
        functools.partial(_attn_kernel, nb=nb),
        grid=(batch, HEAD_PAIRS, seq // tq),
        in_specs=[
            pl.BlockSpec((1, 1, seq, PAIR_W), lambda b, p, i: (b, p, 0, 0)),
            pl.BlockSpec((1, 1, seq, PAIR_W), lambda b, p, i: (b, p, 0, 0)),
            pl.BlockSpec((1, 1, nb, 2, V_ROWS, MOBA_BLOCK), lambda b, p, i: (b, p, 0, 0, 0, 0)),
            pl.BlockSpec((2, NEAR_TILES + 1, MOBA_BLOCK, MOBA_BLOCK), lambda b, p, i: (p, 0, 0, 0)),
        ],
        out_specs=pl.BlockSpec((1, 1, tq, PAIR_W), lambda b, p, i: (b, p, i, 0)),
        out_shape=jax.ShapeDtypeStruct((batch, HEAD_PAIRS, seq, PAIR_W), BF16),
        scratch_shapes=[
            pltpu.VMEM((nb, PAIR_W), F32),
            pltpu.VMEM((4 * nb, PAIR_W), BF16),
            pltpu.VMEM((n_chains, MOBA_BLOCK, PAIR_W), BF16),
            pltpu.VMEM((nb, 2, nb, MOBA_BLOCK), F32),
            pltpu.VMEM((nb, 2, 1, MOBA_BLOCK), F32),
            pltpu.VMEM((n_chains, V_ROWS, MOBA_BLOCK), F32),
            pltpu.VMEM((2, n_chains, MOBA_BLOCK, MOBA_BLOCK), F32),
            pltpu.VMEM((2, n_chains, 1, MOBA_BLOCK), F32),
            pltpu.VMEM((2, n_chains, MOBA_BLOCK, MOBA_BLOCK), BF16),
            pltpu.VMEM((2, n_chains, 1, MOBA_BLOCK), F32),
        ],
        compiler_params=pltpu.CompilerParams(dimension_semantics=("arbitrary", "arbitrary", "arbitrary"),
                                             vmem_limit_bytes=VMEM_LIMIT),
        name="moba_attention",
    )(q, k, vt, bias_t)


def _post_kernel(x_ref, u_ref, vn_ref, sga_ref, sgb_ref, yb_ref, p_ref,
                 ws_ref, bs_ref, wo_ref, gffn_ref, w1_ref, w2_ref, gple_ref, wpg_ref, wpp_ref, gfin_ref,
                 o_ref, merged_ref, *, tm, final_norm):
    row = lax.broadcasted_iota(jnp.int32, (SGU_CHUNK, SGU_CHUNK), 0)
    col = lax.broadcasted_iota(jnp.int32, (SGU_CHUNK, SGU_CHUNK), 1)
    for g in range(SGU_GROUPS):
        w = jnp.where(row >= col, ws_ref[g], 0.0).astype(BF16)
        b = bs_ref[g]
        cols = slice(g * LANES, (g + 1) * LANES)
        for t in range(tm // SGU_CHUNK):
            rows = slice(t * SGU_CHUNK, (t + 1) * SGU_CHUNK)
            mixed = _dot(w, vn_ref[rows, cols]) + b
            y_a = u_ref[rows, cols].astype(F32) * mixed
            y_b = yb_ref[0, g, rows, :].astype(F32)
            merged = sga_ref[rows, cols].astype(F32) * y_a + sgb_ref[rows, cols].astype(F32) * y_b
            merged_ref[rows, cols] = merged.astype(BF16)

    x1 = x_ref[...] + _dot(merged_ref[...], wo_ref[...])

    h = _rms(x1, gffn_ref[...]).astype(BF16)
    x2 = x1
    for c in range(D_FF // D_MODEL):
        cs = slice(c * D_MODEL, (c + 1) * D_MODEL)
        a = jnp.square(jnp.maximum(_dot(h, w1_ref[:, cs]), 0.0)).astype(BF16)
        x2 = x2 + _dot(a, w2_ref[cs, :])

    gate = jax.nn.sigmoid(_dot(_rms(x2, gple_ref[...]).astype(BF16), wpg_ref[...]))
    x3 = x2 + gate * _dot(p_ref[...].astype(BF16), wpp_ref[...])
    o_ref[...] = _rms(x3, gfin_ref[...]) if final_norm else x3


def _post(x2, u, vn, sga, sgb, yb, p2, ws, bs, wo, gffn, w1, w2, gple, wpg, wpp, gfin, *, seq, tm, final_norm):
    n = x2.shape[0]
    tiles_per_seq = seq // tm
    row = lambda r: (r, 0)
    resident = functools.partial(pl.BlockSpec, pipeline_mode=pl.Buffered(1))
    c2 = lambda r: (0, 0)
    c3 = lambda r: (0, 0, 0)
    tokb = pl.BlockSpec((tm, D_MODEL), row)
    return pl.pallas_call(
        functools.partial(_post_kernel, tm=tm, final_norm=final_norm),
        grid=(n // tm,),
        in_specs=[
            tokb, tokb, tokb, tokb, tokb,
            pl.BlockSpec((1, HEAD_PAIRS, tm, PAIR_W), lambda r: (r // tiles_per_seq, 0, r % tiles_per_seq, 0)),
            pl.BlockSpec((tm, PLE_DIM), row),
            resident((SGU_GROUPS, SGU_CHUNK, SGU_CHUNK), c3),
            resident((SGU_GROUPS, SGU_CHUNK, 1), c3),
            resident((D_MODEL, D_MODEL), c2),
            resident((1, D_MODEL), c2),
            resident((D_MODEL, D_FF), c2),
            resident((D_FF, D_MODEL), c2),
            resident((1, D_MODEL), c2),
            resident((D_MODEL, D_MODEL), c2),
            resident((PLE_DIM, D_MODEL), c2),
            resident((1, D_MODEL), c2),
        ],
        out_specs=pl.BlockSpec((tm, D_MODEL), row),
        out_shape=jax.ShapeDtypeStruct((n, D_MODEL), F32),
        scratch_shapes=[pltpu.VMEM((tm, D_MODEL), BF16)],
        compiler_params=pltpu.CompilerParams(dimension_semantics=("arbitrary",),
                                             vmem_limit_bytes=VMEM_LIMIT),
        name="post",
    )(x2, u, vn, sga, sgb, yb, p2, ws, bs, wo, gffn, w1, w2, gple, wpg, wpp, gfin)


def kernel(x, p, norm_mix_g, w_in, w_sgu_spatial, b_sgu_spatial, ln_v_g, ln_v_b, rel_bias, w_out, norm_ffn_g,
           w_ff1, w_ff2, norm_ple_g, w_ple_gate, w_ple_proj, norm_final_g):
    batch, seq, _ = x.shape
    depth = w_in.shape[0]
    n = batch * seq
    assert seq % (Q_BLOCKS * MOBA_BLOCK) == 0 and seq // MOBA_BLOCK >= MOBA_TOPK
    tm_in = 512
    tm_post = 512
    vec = lambda g: g.reshape(1, D_MODEL)

    bias_t = _bias_tiles(rel_bias)
    x2 = x.reshape(n, D_MODEL)
    for i in range(depth):
        wi = w_in[i].astype(BF16)
        wvt = wi[:, 4 * D_MODEL:5 * D_MODEL].T
        u, vn, q, k, vt, sga, sgb = _in_proj(
            x2, vec(norm_mix_g[i]), wi, wvt, vec(ln_v_g[i]), vec(ln_v_b[i]),
            batch=batch, seq=seq, tm=tm_in)
        yb = _attention(q, k, vt, bias_t)
        x2 = _post(
            x2, u, vn, sga, sgb, yb, p[i].reshape(n, PLE_DIM),
            w_sgu_spatial[i], b_sgu_spatial[i].reshape(SGU_GROUPS, SGU_CHUNK, 1),
            w_out[i].astype(BF16), vec(norm_ffn_g[i]), w_ff1[i].astype(BF16), w_ff2[i].astype(BF16),
            vec(norm_ple_g[i]), w_ple_gate[i].astype(BF16), w_ple_proj[i].astype(BF16),
            vec(norm_final_g), seq=seq, tm=tm_post, final_norm=(i == depth - 1))
    return x2.reshape(batch, seq, D_MODEL)
```

```python
import functools
import math

import jax
import jax.numpy as jnp
from jax import lax
from jax.experimental import pallas as pl
from jax.experimental.pallas import tpu as pltpu

D_MODEL = 1024
PLE_DIM = 256
SGU_CHUNK = 128
SGU_GROUPS = 8
ATT_HEADS = 16
HEAD_DIM = 64
MOBA_BLOCK = 256
MOBA_TOPK = 3
REL_BUCKETS = 32
REL_MAX_DIST = 1024
D_FF = 4 * D_MODEL
EPS = 1e-6

LANES = 128
BF16_ROWS = 16
HEAD_PAIRS = ATT_HEADS // 2
PAIR_W = 2 * HEAD_DIM
V_ROWS = HEAD_DIM + BF16_ROWS
Q_BLOCKS = 4
FAR_TRIP_PAIRS = (4, 2, 1)
SEL_CHUNK = 8
LAZY_MAX_EXCESS = 16.0
BIG = 1e30
VMEM_LIMIT = 60 * 1024 * 1024
LOG2E = math.log2(math.e)

F32 = jnp.float32
BF16 = jnp.bfloat16


def _bucket_thresholds():
    max_exact = REL_BUCKETS // 2
    n_log = REL_BUCKETS - max_exact
    ratio = REL_MAX_DIST // max_exact
    out = []
    for k in range(1, n_log):
        d = max_exact
        target = (max_exact ** n_log) * (ratio ** k)
        while d ** n_log < target:
            d += 1
        out.append(d)
    return tuple(out)


BUCKET_THRESHOLDS = _bucket_thresholds()
NEAR_TILES = -(-(BUCKET_THRESHOLDS[-1] + MOBA_BLOCK - 1) // MOBA_BLOCK)
assert NEAR_TILES == 5


def _dot(a, b):
    return jnp.dot(a, b, preferred_element_type=F32)


def _dot_nt(a, b):
    return lax.dot_general(a, b, (((1,), (1,)), ((), ())), preferred_element_type=F32)


def _rms(x, g):
    return x * lax.rsqrt(jnp.mean(x * x, axis=-1, keepdims=True) + EPS) * g


def _in_proj_kernel(x_ref, g_ref, wa_ref, wvt_ref, wga_ref, wgb_ref, lng_ref, lnb_ref,
                    u_ref, vn_ref, q_ref, k_ref, vt_ref, sga_ref, sgb_ref, *, tm):
    h = _rms(x_ref[...], g_ref[...]).astype(BF16)

    u_ref[...] = jax.nn.gelu(_dot(h, wa_ref[:, 0:D_MODEL])).astype(BF16)

    gv = jax.nn.gelu(_dot(h, wa_ref[:, D_MODEL:2 * D_MODEL]))
    mu = jnp.mean(gv, axis=-1, keepdims=True)
    gc = gv - mu
    vn = gc * lax.rsqrt(jnp.mean(gc * gc, axis=-1, keepdims=True) + EPS)
    vn_ref[...] = (vn * lng_ref[...] + lnb_ref[...]).astype(BF16)

    q = _dot(h, wa_ref[:, 2 * D_MODEL:3 * D_MODEL]) * (HEAD_DIM ** -0.5 * LOG2E)
    k = _dot(h, wa_ref[:, 3 * D_MODEL:4 * D_MODEL])
    for p in range(HEAD_PAIRS):
        q_ref[0, p] = q[:, p * PAIR_W:(p + 1) * PAIR_W].astype(BF16)
        k_ref[0, p] = k[:, p * PAIR_W:(p + 1) * PAIR_W].astype(BF16)

    vt = _dot_nt(wvt_ref[...], h).astype(BF16)
    ones = jnp.ones((BF16_ROWS, MOBA_BLOCK), BF16)
    for p in range(HEAD_PAIRS):
        for jb in range(tm // MOBA_BLOCK):
            for a in range(2):
                r0 = p * PAIR_W + a * HEAD_DIM
                vt_ref[0, p, jb, a, 0:HEAD_DIM, :] = vt[r0:r0 + HEAD_DIM, jb * MOBA_BLOCK:(jb + 1) * MOBA_BLOCK]
                vt_ref[0, p, jb, a, HEAD_DIM:V_ROWS, :] = ones

    sga_ref[...] = jax.nn.sigmoid(_dot(h, wga_ref[...])).astype(BF16)
    sgb_ref[...] = jax.nn.sigmoid(_dot(h, wgb_ref[...])).astype(BF16)


def _in_proj(x2, g, w, wvt, lng, lnb, *, batch, seq, tm):
    n = x2.shape[0]
    tiles_per_seq = seq // tm
    nb = seq // MOBA_BLOCK
    row = lambda r: (r, 0)
    const = lambda r: (0, 0)
    resident = functools.partial(pl.BlockSpec, pipeline_mode=pl.Buffered(1))
    tok = jax.ShapeDtypeStruct((n, D_MODEL), BF16)
    pair_major = jax.ShapeDtypeStruct((batch, HEAD_PAIRS, seq, PAIR_W), BF16)
    return pl.pallas_call(
        functools.partial(_in_proj_kernel, tm=tm),
        grid=(n // tm,),
        in_specs=[
            pl.BlockSpec((tm, D_MODEL), row),
            resident((1, D_MODEL), const),
            resident((D_MODEL, 4 * D_MODEL), const),
            resident((D_MODEL, D_MODEL), const),
            resident((D_MODEL, D_MODEL), lambda r: (0, 5)),
            resident((D_MODEL, D_MODEL), lambda r: (0, 6)),
            resident((1, D_MODEL), const),
            resident((1, D_MODEL), const),
        ],
        out_specs=[
            pl.BlockSpec((tm, D_MODEL), row),
            pl.BlockSpec((tm, D_MODEL), row),
            pl.BlockSpec((1, HEAD_PAIRS, tm, PAIR_W), lambda r: (r // tiles_per_seq, 0, r % tiles_per_seq, 0)),
            pl.BlockSpec((1, HEAD_PAIRS, tm, PAIR_W), lambda r: (r // tiles_per_seq, 0, r % tiles_per_seq, 0)),
            pl.BlockSpec((1, HEAD_PAIRS, tm // MOBA_BLOCK, 2, V_ROWS, MOBA_BLOCK),
                         lambda r: (r // tiles_per_seq, 0, r % tiles_per_seq, 0, 0, 0)),
            pl.BlockSpec((tm, D_MODEL), row),
            pl.BlockSpec((tm, D_MODEL), row),
        ],
        out_shape=[tok, tok, pair_major, pair_major,
                   jax.ShapeDtypeStruct((batch, HEAD_PAIRS, nb, 2, V_ROWS, MOBA_BLOCK), BF16),
                   tok, tok],
        compiler_params=pltpu.CompilerParams(dimension_semantics=("arbitrary",),
                                             vmem_limit_bytes=VMEM_LIMIT),
        name="in_proj",
    )(x2, g, w, wvt, w, w, lng, lnb)


def _bias_tiles_kernel(tab_ref, o_ref):
    h = pl.program_id(0)
    key = lax.broadcasted_iota(jnp.int32, (MOBA_BLOCK, MOBA_BLOCK), 0)
    qry = lax.broadcasted_iota(jnp.int32, (MOBA_BLOCK, MOBA_BLOCK), 1)
    max_exact = REL_BUCKETS // 2
    last = tab_ref[REL_BUCKETS - 1, h]
    for t in range(NEAR_TILES):
        dist = t * MOBA_BLOCK + qry - key
        n = jnp.maximum(dist, 0)
        val = jnp.full((MOBA_BLOCK, MOBA_BLOCK), tab_ref[0, h], F32)
        for b in range(1, max_exact + 1):
            val = jnp.where(n >= b, tab_ref[b, h], val)
        for kk, thr in enumerate(BUCKET_THRESHOLDS):
            val = jnp.where(n >= thr, tab_ref[max_exact + 1 + kk, h], val)
        val = (val - last) * LOG2E
        if t == 0:
            val = jnp.where(dist >= 0, val, -BIG)
        o_ref[0, t] = val
    o_ref[0, NEAR_TILES] = jnp.zeros((MOBA_BLOCK, MOBA_BLOCK), F32)


def _bias_tiles(rel_bias):
    return pl.pallas_call(
        _bias_tiles_kernel,
        grid=(ATT_HEADS,),
        in_specs=[pl.BlockSpec(memory_space=pltpu.SMEM)],
        out_specs=pl.BlockSpec((1, NEAR_TILES + 1, MOBA_BLOCK, MOBA_BLOCK), lambda h: (h, 0, 0, 0)),
        out_shape=jax.ShapeDtypeStruct((ATT_HEADS, NEAR_TILES + 1, MOBA_BLOCK, MOBA_BLOCK), F32),
        compiler_params=pltpu.CompilerParams(dimension_semantics=("arbitrary",)),
        name="bias_tiles",
    )(rel_bias)


def _attn_kernel(q_ref, k_ref, vt_ref, bias_ref, o_ref,
                 km_ref, kms_ref, qa_ref, sel_ref, seed_ref, acc_ref, s_ref, mt_ref, p_ref, alpha_ref, *, nb):
    step = pl.program_id(2)
    j_last = step * Q_BLOCKS + (Q_BLOCKS - 1)
    chains = [(h, a) for h in range(Q_BLOCKS) for a in range(2)]
    lane = lax.broadcasted_iota(jnp.int32, (1, PAIR_W), 1)

    @pl.when(step == 0)
    def _():
        def body(j, c):
            kb = k_ref[0, 0, pl.ds(pl.multiple_of(j * MOBA_BLOCK, MOBA_BLOCK), MOBA_BLOCK), :]
            km_ref[pl.ds(j, 1), :] = jnp.sum(kb.astype(F32), axis=0, keepdims=True) * (1.0 / MOBA_BLOCK)
            return c
        lax.fori_loop(0, nb, body, 0)
        km = km_ref[...]
        km_hi = km.astype(BF16)
        km_lo = (km - km_hi.astype(F32)).astype(BF16)
        for a in range(2):
            in_head = (lane >= a * HEAD_DIM) & (lane < (a + 1) * HEAD_DIM)
            kms_ref[(2 * a) * nb:(2 * a + 1) * nb, :] = jnp.where(in_head, km_hi, jnp.zeros_like(km_hi))
            kms_ref[(2 * a + 1) * nb:(2 * a + 2) * nb, :] = jnp.where(in_head, km_lo, jnp.zeros_like(km_lo))

        n_q = SEL_CHUNK * MOBA_BLOCK
        head_rows = jnp.where(lane // HEAD_DIM == lax.broadcasted_iota(jnp.int32, (BF16_ROWS, PAIR_W), 0),
                              1.0, 0.0).astype(BF16)
        for ci in range(nb // SEL_CHUNK):
            rows = SEL_CHUNK * (ci + 1)
            q = q_ref[0, 0, ci * n_q:(ci + 1) * n_q, :]
            g = _dot_nt(kms_ref[...], q)
            gate = jnp.concatenate([g[(2 * a) * nb:(2 * a) * nb + rows] + g[(2 * a + 1) * nb:(2 * a + 1) * nb + rows]
                                    for a in range(2)], axis=1)
            pos = lax.broadcasted_iota(jnp.int32, (1, n_q), 1)
            own = ci * SEL_CHUNK + jnp.right_shift(pos, MOBA_BLOCK.bit_length() - 1)
            own = jnp.concatenate([own, own], axis=1)
            blk = lax.broadcasted_iota(jnp.int32, gate.shape, 0)
            gate = jnp.where(blk < own, gate, -jnp.inf)
            sel = blk == own
            for _ in range(MOBA_TOPK):
                top = jnp.max(gate, axis=0, keepdims=True)
                first = jnp.min(jnp.where(gate == top, blk, nb), axis=0, keepdims=True)
                pick = (blk == first) & (top > -jnp.inf)
                sel = sel | pick
                gate = jnp.where(pick, -jnp.inf, gate)
            sel_f = jnp.where(sel, 1.0, 0.0)
            own_scores = _dot_nt(head_rows, q * k_ref[0, 0, ci * n_q:(ci + 1) * n_q, :])
            later = jnp.zeros((nb - rows, MOBA_BLOCK), F32)
            for a in range(2):
                for b in range(SEL_CHUNK):
                    lo = a * n_q + b * MOBA_BLOCK
                    chosen_blocks = sel_f[:, lo:lo + MOBA_BLOCK]
                    sel_ref[ci * SEL_CHUNK + b, a] = (jnp.concatenate([chosen_blocks, later], axis=0)
                                                      if rows < nb else chosen_blocks)
                    seed_ref[ci * SEL_CHUNK + b, a] = (own_scores[a:a + 1, b * MOBA_BLOCK:(b + 1) * MOBA_BLOCK]
                                                       + bias_ref[a, 0, 0:1, 0:1])

    for c, (h, a) in enumerate(chains):
        q = q_ref[0, 0, pl.ds(pl.multiple_of((step * Q_BLOCKS + h) * MOBA_BLOCK, MOBA_BLOCK), MOBA_BLOCK), :]
        qa_ref[c] = jnp.where((lane >= a * HEAD_DIM) & (lane < (a + 1) * HEAD_DIM), q, jnp.zeros_like(q))

    def block_of(t):
        return jnp.maximum(j_last - t, 0)

    def key_block(j):
        return k_ref[0, 0, pl.ds(pl.multiple_of(j * MOBA_BLOCK, MOBA_BLOCK), MOBA_BLOCK), :]

    def scores(t, c, kj, near):
        h, a = chains[c]
        s = _dot_nt(kj, qa_ref[c])
        offset = t - (Q_BLOCKS - 1 - h)
        if isinstance(t, int):
            if offset < NEAR_TILES:
                s = s + bias_ref[a, max(offset, 0)]
        elif near:
            s = s + bias_ref[a, jnp.clip(offset, 0, NEAR_TILES)]
        return s

    def visits(t, c):
        return not isinstance(t, int) or t - (Q_BLOCKS - 1 - chains[c][0]) >= 0

    def chosen_row(t, c, j):
        h, a = chains[c]
        return sel_ref[step * Q_BLOCKS + h, a, pl.ds(j, 1), :] > jnp.where(t <= j_last, 0.5, 2.0)

    def score_stage(t, slot, near):
        kj = key_block(block_of(t))
        for c in range(len(chains)):
            s = scores(t, c, kj, near)
            s_ref[slot, c] = s
            mt_ref[slot, c] = jnp.max(s, axis=0, keepdims=True)

    def softmax_stage(t, slot, m_prev):
        j = block_of(t)
        m_next, alphas, pvs = [], [], []
        for c, (h, a) in enumerate(chains):
            chosen = chosen_row(t, c, j)
            m_new = jnp.where(chosen, jnp.maximum(m_prev[c], mt_ref[slot, c]), m_prev[c])
            alphas.append(jnp.exp2(m_prev[c] - m_new))
            p = jnp.exp2(s_ref[slot, c] - jnp.where(chosen, m_new, BIG)).astype(BF16)
            pvs.append(_dot(vt_ref[0, 0, j, a], p))
            m_next.append(m_new)
        for c in range(len(chains)):
            acc_ref[c] = alphas[c] * acc_ref[c] + pvs[c]
        return tuple(m_next)

    def exact_pair(tt, m):
        t = 2 * tt
        score_stage(t + 1, 1, True)
        m = softmax_stage(t, 0, m)
        score_stage(t + 2, 0, True)
        m = softmax_stage(t + 1, 1, m)
        return m

    def lazy_stage(carry, near, score=None, pv=None):
        r, excess = list(carry[0]), list(carry[1])
        if score is not None:
            t, slot = score
            j = block_of(t)
            kj = key_block(j)
            for c in range(len(chains)):
                if not visits(t, c):
                    continue
                s = scores(t, c, kj, near)
                chosen = chosen_row(t, c, j)
                tile_max = jnp.max(s, axis=0, keepdims=True)
                p_ref[slot, c] = jnp.exp2(s - jnp.where(chosen, r[c], BIG)).astype(BF16)
                r_new = jnp.where(chosen, jnp.maximum(r[c], tile_max), r[c])
                alpha_ref[slot, c] = jnp.exp2(r[c] - r_new)
                excess[c] = jnp.maximum(excess[c], jnp.where(chosen, tile_max - r[c], -BIG))
                r[c] = r_new
        if pv is not None:
            t, slot = pv
            j = block_of(t)
            live = [c for c in range(len(chains)) if visits(t, c)]
            pvs = [_dot(vt_ref[0, 0, j, chains[c][1]], p_ref[slot, c]) for c in live]
            for c, prod in zip(live, pvs):
                acc_ref[c] = (acc_ref[c] + prod) * alpha_ref[slot, c]
        return tuple(r), tuple(excess)

    def lazy_pairs(it, carry, *, near, first_pair, pairs_per_trip, single_trip=False):
        if single_trip:
            t = 2 * first_pair
            carry = lazy_stage(carry, near, score=(t, 0))
        else:
            t = 2 * (first_pair + it * pairs_per_trip)
        for _ in range(pairs_per_trip):
            carry = lazy_stage(carry, near, score=(t + 1, 1), pv=(t, 0))
            carry = lazy_stage(carry, near, score=(t + 2, 0), pv=(t + 1, 1))
            t = t + 2
        return carry

    def finalize():
        for h in range(Q_BLOCKS):
            outs = []
            for a in range(2):
                acc = acc_ref[2 * h + a]
                outs.append(acc[0:HEAD_DIM] * (1.0 / acc[HEAD_DIM:HEAD_DIM + 1]))
            o_ref[0, 0, h * MOBA_BLOCK:(h + 1) * MOBA_BLOCK, :] = jnp.concatenate(outs, axis=0).T.astype(BF16)

    near_pairs = (NEAR_TILES + Q_BLOCKS) // 2
    n_pairs = (j_last + 2) // 2
    far_pairs = jnp.maximum(n_pairs - near_pairs, 0)
    lowest = (jnp.full((1, MOBA_BLOCK), -BIG, F32),) * len(chains)

    seeds = [seed_ref[step * Q_BLOCKS + h, a] for h, a in chains]

    acc_ref[...] = jnp.zeros_like(acc_ref)
    carry = (tuple(seeds), lowest)
    counts = {(Q_BLOCKS * (i + 1) + 1) // 2 for i in range(nb // Q_BLOCKS)}
    for n in sorted(c for c in counts if c < near_pairs) + [near_pairs]:
        runs = (n_pairs >= n) if n == near_pairs else (n_pairs == n)
        carry = lax.fori_loop(0, jnp.where(runs, 1, 0),
                              functools.partial(lazy_pairs, near=True, first_pair=0, pairs_per_trip=n,
                                                single_trip=True), carry)
    first, left = near_pairs, far_pairs
    for pairs_per_trip in FAR_TRIP_PAIRS:
        trips = left // pairs_per_trip
        carry = lax.fori_loop(0, trips, functools.partial(lazy_pairs, near=False, first_pair=first,
                                                          pairs_per_trip=pairs_per_trip), carry)
        first, left = first + trips * pairs_per_trip, left - trips * pairs_per_trip
    finalize()

    worst = functools.reduce(jnp.maximum, carry[1])
    @pl.when(jnp.max(worst) > LAZY_MAX_EXCESS)
    def _():
        acc_ref[...] = jnp.zeros_like(acc_ref)
        score_stage(0, 0, True)
        lax.fori_loop(0, n_pairs, exact_pair, lowest)
        finalize()


def _attention(q, k, vt, bias_t):
    batch, _, seq, _ = q.shape
    nb = seq // MOBA_BLOCK
    tq = Q_BLOCKS * MOBA_BLOCK
    n_chains = 2 * Q_BLOCKS
    return pl.pallas_call(
        functools.partial(_attn_kernel, nb=nb),
        grid=(batch, HEAD_PAIRS, seq // tq),
        in_specs=[
            pl.BlockSpec((1, 1, seq, PAIR_W), lambda b, p, i: (b, p, 0, 0)),
            pl.BlockSpec((1, 1, seq, PAIR_W), lambda b, p, i: (b, p, 0, 0)),
            pl.BlockSpec((1, 1, nb, 2, V_ROWS, MOBA_BLOCK), lambda b, p, i: (b, p, 0, 0, 0, 0)),
            pl.BlockSpec((2, NEAR_TILES + 1, MOBA_BLOCK, MOBA_BLOCK), lambda b, p, i: (p, 0, 0, 0)),
        ],
        out_specs=pl.BlockSpec((1, 1, tq, PAIR_W), lambda b, p, i: (b, p, i, 0)),
        out_shape=jax.ShapeDtypeStruct((batch, HEAD_PAIRS, seq, PAIR_W), BF16),
        scratch_shapes=[
            pltpu.VMEM((nb, PAIR_W), F32),
            pltpu.VMEM((4 * nb, PAIR_W), BF16),
            pltpu.VMEM((n_chains, MOBA_BLOCK, PAIR_W), BF16),
            pltpu.VMEM((nb, 2, nb, MOBA_BLOCK), F32),
            pltpu.VMEM((nb, 2, 1, MOBA_BLOCK), F32),
            pltpu.VMEM((n_chains, V_ROWS, MOBA_BLOCK), F32),
            pltpu.VMEM((2, n_chains, MOBA_BLOCK, MOBA_BLOCK), F32),
            pltpu.VMEM((2, n_chains, 1, MOBA_BLOCK), F32),
            pltpu.VMEM((2, n_chains, MOBA_BLOCK, MOBA_BLOCK), BF16),
            pltpu.VMEM((2, n_chains, 1, MOBA_BLOCK), F32),
        ],
        compiler_params=pltpu.CompilerParams(dimension_semantics=("arbitrary", "arbitrary", "arbitrary"),
                                             vmem_limit_bytes=VMEM_LIMIT),
        name="moba_attention",
    )(q, k, vt, bias_t)


def _post_kernel(x_ref, u_ref, vn_ref, sga_ref, sgb_ref, yb_ref, p_ref,
                 ws_ref, bs_ref, wo_ref, gffn_ref, w1_ref, w2_ref, gple_ref, wpg_ref, wpp_ref, gfin_ref,
                 o_ref, merged_ref, *, tm, final_norm):
    row = lax.broadcasted_iota(jnp.int32, (SGU_CHUNK, SGU_CHUNK), 0)
    col = lax.broadcasted_iota(jnp.int32, (SGU_CHUNK, SGU_CHUNK), 1)
    for g in range(SGU_GROUPS):
        w = jnp.where(row >= col, ws_ref[g], 0.0).astype(BF16)
        b = bs_ref[g]
        cols = slice(g * LANES, (g + 1) * LANES)
        for t in range(tm // SGU_CHUNK):
            rows = slice(t * SGU_CHUNK, (t + 1) * SGU_CHUNK)
            mixed = _dot(w, vn_ref[rows, cols]) + b
            y_a = u_ref[rows, cols].astype(F32) * mixed
            y_b = yb_ref[0, g, rows, :].astype(F32)
            merged = sga_ref[rows, cols].astype(F32) * y_a + sgb_ref[rows, cols].astype(F32) * y_b
            merged_ref[rows, cols] = merged.astype(BF16)

    x1 = x_ref[...] + _dot(merged_ref[...], wo_ref[...])

    h = _rms(x1, gffn_ref[...]).astype(BF16)
    x2 = x1
    for c in range(D_FF // D_MODEL):
        cs = slice(c * D_MODEL, (c + 1) * D_MODEL)
        a = jnp.square(jnp.maximum(_dot(h, w1_ref[:, cs]), 0.0)).astype(BF16)
        x2 = x2 + _dot(a, w2_ref[cs, :])

    gate = jax.nn.sigmoid(_dot(_rms(x2, gple_ref[...]).astype(BF16), wpg_ref[...]))
    x3 = x2 + gate * _dot(p_ref[...].astype(BF16), wpp_ref[...])
    o_ref[...] = _rms(x3, gfin_ref[...]) if final_norm else x3


def _post(x2, u, vn, sga, sgb, yb, p2, ws, bs, wo, gffn, w1, w2, gple, wpg, wpp, gfin, *, seq, tm, final_norm):
    n = x2.shape[0]
    tiles_per_seq = seq // tm
    row = lambda r: (r, 0)
    resident = functools.partial(pl.BlockSpec, pipeline_mode=pl.Buffered(1))
    c2 = lambda r: (0, 0)
    c3 = lambda r: (0, 0, 0)
    tokb = pl.BlockSpec((tm, D_MODEL), row)
    return pl.pallas_call(
        functools.partial(_post_kernel, tm=tm, final_norm=final_norm),
        grid=(n // tm,),
        in_specs=[
            tokb, tokb, tokb, tokb, tokb,
            pl.BlockSpec((1, HEAD_PAIRS, tm, PAIR_W), lambda r: (r // tiles_per_seq, 0, r % tiles_per_seq, 0)),
            pl.BlockSpec((tm, PLE_DIM), row),
            resident((SGU_GROUPS, SGU_CHUNK, SGU_CHUNK), c3),
            resident((SGU_GROUPS, SGU_CHUNK, 1), c3),
            resident((D_MODEL, D_MODEL), c2),
            resident((1, D_MODEL), c2),
            resident((D_MODEL, D_FF), c2),
            resident((D_FF, D_MODEL), c2),
            resident((1, D_MODEL), c2),
            resident((D_MODEL, D_MODEL), c2),
            resident((PLE_DIM, D_MODEL), c2),
            resident((1, D_MODEL), c2),
        ],
        out_specs=pl.BlockSpec((tm, D_MODEL), row),
        out_shape=jax.ShapeDtypeStruct((n, D_MODEL), F32),
        scratch_shapes=[pltpu.VMEM((tm, D_MODEL), BF16)],
        compiler_params=pltpu.CompilerParams(dimension_semantics=("arbitrary",),
                                             vmem_limit_bytes=VMEM_LIMIT),
        name="post",
    )(x2, u, vn, sga, sgb, yb, p2, ws, bs, wo, gffn, w1, w2, gple, wpg, wpp, gfin)


def kernel(x, p, norm_mix_g, w_in, w_sgu_spatial, b_sgu_spatial, ln_v_g, ln_v_b, rel_bias, w_out, norm_ffn_g,
           w_ff1, w_ff2, norm_ple_g, w_ple_gate, w_ple_proj, norm_final_g):
    batch, seq, _ = x.shape
    depth = w_in.shape[0]
    n = batch * seq
    assert seq % (Q_BLOCKS * MOBA_BLOCK) == 0 and seq // MOBA_BLOCK >= MOBA_TOPK
    tm_in = 512
    tm_post = 512
    vec = lambda g: g.reshape(1, D_MODEL)

    bias_t = _bias_tiles(rel_bias)
    x2 = x.reshape(n, D_MODEL)
    for i in range(depth):
        wi = w_in[i].astype(BF16)
        wvt = wi[:, 4 * D_MODEL:5 * D_MODEL].T
        u, vn, q, k, vt, sga, sgb = _in_proj(
            x2, vec(norm_mix_g[i]), wi, wvt, vec(ln_v_g[i]), vec(ln_v_b[i]),
            batch=batch, seq=seq, tm=tm_in)
        yb = _attention(q, k, vt, bias_t)
        x2 = _post(
            x2, u, vn, sga, sgb, yb, p[i].reshape(n, PLE_DIM),
            w_sgu_spatial[i], b_sgu_spatial[i].reshape(SGU_GROUPS, SGU_CHUNK, 1),
            w_out[i].astype(BF16), vec(norm_ffn_g[i]), w_ff1[i].astype(BF16), w_ff2[i].astype(BF16),
            vec(norm_ple_g[i]), w_ple_gate[i].astype(BF16), w_ple_proj[i].astype(BF16),
            vec(norm_final_g), seq=seq, tm=tm_post, final_norm=(i == depth - 1))
    return x2.reshape(batch, seq, D_MODEL)
```

```python
import functools
import math

import jax
import jax.numpy as jnp
from jax import lax
from jax.experimental import pallas as pl
from jax.experimental.pallas import tpu as pltpu

D_MODEL = 1024
PLE_DIM = 256
SGU_CHUNK = 128
SGU_GROUPS = 8
ATT_HEADS = 16
HEAD_DIM = 64
MOBA_BLOCK = 256
MOBA_TOPK = 3
REL_BUCKETS = 32
REL_MAX_DIST = 1024
D_FF = 4 * D_MODEL
EPS = 1e-6

LANES = 128
BF16_ROWS = 16
HEAD_PAIRS = ATT_HEADS // 2
PAIR_W = 2 * HEAD_DIM
V_ROWS = HEAD_DIM + BF16_ROWS
Q_BLOCKS = 4
FAR_TRIP_PAIRS = (4, 2, 1)
SEL_CHUNK = 8
LAZY_MAX_EXCESS = 16.0
BIG = 1e30
VMEM_LIMIT = 60 * 1024 * 1024
LOG2E = math.log2(math.e)

F32 = jnp.float32
BF16 = jnp.bfloat16


def _bucket_thresholds():
    max_exact = REL_BUCKETS // 2
    n_log = REL_BUCKETS - max_exact
    ratio = REL_MAX_DIST // max_exact
    out = []
    for k in range(1, n_log):
        d = max_exact
        target = (max_exact ** n_log) * (ratio ** k)
        while d ** n_log < target:
            d += 1
        out.append(d)
    return tuple(out)


BUCKET_THRESHOLDS = _bucket_thresholds()
NEAR_TILES = -(-(BUCKET_THRESHOLDS[-1] + MOBA_BLOCK - 1) // MOBA_BLOCK)
assert NEAR_TILES == 5


def _dot(a, b):
    return jnp.dot(a, b, preferred_element_type=F32)


def _dot_nt(a, b):
    return lax.dot_general(a, b, (((1,), (1,)), ((), ())), preferred_element_type=F32)


def _rms(x, g):
    return x * lax.rsqrt(jnp.mean(x * x, axis=-1, keepdims=True) + EPS) * g


def _in_proj_kernel(x_ref, g_ref, wa_ref, wvt_ref, wga_ref, wgb_ref, lng_ref, lnb_ref,
                    u_ref, vn_ref, q_ref, k_ref, vt_ref, sga_ref, sgb_ref, *, tm):
    h = _rms(x_ref[...], g_ref[...]).astype(BF16)

    u_ref[...] = jax.nn.gelu(_dot(h, wa_ref[:, 0:D_MODEL])).astype(BF16)

    gv = jax.nn.gelu(_dot(h, wa_ref[:, D_MODEL:2 * D_MODEL]))
    mu = jnp.mean(gv, axis=-1, keepdims=True)
    gc = gv - mu
    vn = gc * lax.rsqrt(jnp.mean(gc * gc, axis=-1, keepdims=True) + EPS)
    vn_ref[...] = (vn * lng_ref[...] + lnb_ref[...]).astype(BF16)

    q = _dot(h, wa_ref[:, 2 * D_MODEL:3 * D_MODEL]) * (HEAD_DIM ** -0.5 * LOG2E)
    k = _dot(h, wa_ref[:, 3 * D_MODEL:4 * D_MODEL])
    for p in range(HEAD_PAIRS):
        q_ref[0, p] = q[:, p * PAIR_W:(p + 1) * PAIR_W].astype(BF16)
        k_ref[0, p] = k[:, p * PAIR_W:(p + 1) * PAIR_W].astype(BF16)

    vt = _dot_nt(wvt_ref[...], h).astype(BF16)
    ones = jnp.ones((BF16_ROWS, MOBA_BLOCK), BF16)
    for p in range(HEAD_PAIRS):
        for jb in range(tm // MOBA_BLOCK):
            for a in range(2):
                r0 = p * PAIR_W + a * HEAD_DIM
                vt_ref[0, p, jb, a, 0:HEAD_DIM, :] = vt[r0:r0 + HEAD_DIM, jb * MOBA_BLOCK:(jb + 1) * MOBA_BLOCK]
                vt_ref[0, p, jb, a, HEAD_DIM:V_ROWS, :] = ones

    sga_ref[...] = jax.nn.sigmoid(_dot(h, wga_ref[...])).astype(BF16)
    sgb_ref[...] = jax.nn.sigmoid(_dot(h, wgb_ref[...])).astype(BF16)


def _in_proj(x2, g, w, wvt, lng, lnb, *, batch, seq, tm):
    n = x2.shape[0]
    tiles_per_seq = seq // tm
    nb = seq // MOBA_BLOCK
    row = lambda r: (r, 0)
    const = lambda r: (0, 0)
    resident = functools.partial(pl.BlockSpec, pipeline_mode=pl.Buffered(1))
    tok = jax.ShapeDtypeStruct((n, D_MODEL), BF16)
    pair_major = jax.ShapeDtypeStruct((batch, HEAD_PAIRS, seq, PAIR_W), BF16)
    return pl.pallas_call(
        functools.partial(_in_proj_kernel, tm=tm),
        grid=(n // tm,),
        in_specs=[
            pl.BlockSpec((tm, D_MODEL), row),
            resident((1, D_MODEL), const),
            resident((D_MODEL, 4 * D_MODEL), const),
            resident((D_MODEL, D_MODEL), const),
            resident((D_MODEL, D_MODEL), lambda r: (0, 5)),
            resident((D_MODEL, D_MODEL), lambda r: (0, 6)),
            resident((1, D_MODEL), const),
            resident((1, D_MODEL), const),
        ],
        out_specs=[
            pl.BlockSpec((tm, D_MODEL), row),
            pl.BlockSpec((tm, D_MODEL), row),
            pl.BlockSpec((1, HEAD_PAIRS, tm, PAIR_W), lambda r: (r // tiles_per_seq, 0, r % tiles_per_seq, 0)),
            pl.BlockSpec((1, HEAD_PAIRS, tm, PAIR_W), lambda r: (r // tiles_per_seq, 0, r % tiles_per_seq, 0)),
            pl.BlockSpec((1, HEAD_PAIRS, tm // MOBA_BLOCK, 2, V_ROWS, MOBA_BLOCK),
                         lambda r: (r // tiles_per_seq, 0, r % tiles_per_seq, 0, 0, 0)),
            pl.BlockSpec((tm, D_MODEL), row),
            pl.BlockSpec((tm, D_MODEL), row),
        ],
        out_shape=[tok, tok, pair_major, pair_major,
                   jax.ShapeDtypeStruct((batch, HEAD_PAIRS, nb, 2, V_ROWS, MOBA_BLOCK), BF16),
                   tok, tok],
        compiler_params=pltpu.CompilerParams(dimension_semantics=("arbitrary",),
                                             vmem_limit_bytes=VMEM_LIMIT),
        name="in_proj",
    )(x2, g, w, wvt, w, w, lng, lnb)


def _bias_tiles_kernel(tab_ref, o_ref):
    h = pl.program_id(0)
    key = lax.broadcasted_iota(jnp.int32, (MOBA_BLOCK, MOBA_BLOCK), 0)
    qry = lax.broadcasted_iota(jnp.int32, (MOBA_BLOCK, MOBA_BLOCK), 1)
    max_exact = REL_BUCKETS // 2
    last = tab_ref[REL_BUCKETS - 1, h]
    for t in range(NEAR_TILES):
        dist = t * MOBA_BLOCK + qry - key
        n = jnp.maximum(dist, 0)
        val = jnp.full((MOBA_BLOCK, MOBA_BLOCK), tab_ref[0, h], F32)
        for b in range(1, max_exact + 1):
            val = jnp.where(n >= b, tab_ref[b, h], val)
        for kk, thr in enumerate(BUCKET_THRESHOLDS):
            val = jnp.where(n >= thr, tab_ref[max_exact + 1 + kk, h], val)
        val = (val - last) * LOG2E
        if t == 0:
            val = jnp.where(dist >= 0, val, -BIG)
        o_ref[0, t] = val
    o_ref[0, NEAR_TILES] = jnp.zeros((MOBA_BLOCK, MOBA_BLOCK), F32)


def _bias_tiles(rel_bias):
    return pl.pallas_call(
        _bias_tiles_kernel,
        grid=(ATT_HEADS,),
        in_specs=[pl.BlockSpec(memory_space=pltpu.SMEM)],
        out_specs=pl.BlockSpec((1, NEAR_TILES + 1, MOBA_BLOCK, MOBA_BLOCK), lambda h: (h, 0, 0, 0)),
        out_shape=jax.ShapeDtypeStruct((ATT_HEADS, NEAR_TILES + 1, MOBA_BLOCK, MOBA_BLOCK), F32),
        compiler_params=pltpu.CompilerParams(dimension_semantics=("arbitrary",)),
        name="bias_tiles",
    )(rel_bias)


def _attn_kernel(*refs, nb):
    def tile(step, carry):
        _attn_tile(step, *refs, nb=nb)
        return carry
    lax.fori_loop(0, nb // Q_BLOCKS, tile, 0)


def _attn_tile(step, q_ref, k_ref, vt_ref, bias_ref, o_ref,
               km_ref, kms_ref, qa_ref, sel_ref, seed_ref, acc_ref, s_ref, mt_ref, p_ref, alpha_ref, *, nb):
    j_last = step * Q_BLOCKS + (Q_BLOCKS - 1)
    chains = [(h, a) for h in range(Q_BLOCKS) for a in range(2)]
    lane = lax.broadcasted_iota(jnp.int32, (1, PAIR_W), 1)

    @pl.when(step == 0)
    def _():
        def body(j, c):
            kb = k_ref[0, 0, pl.ds(pl.multiple_of(j * MOBA_BLOCK, MOBA_BLOCK), MOBA_BLOCK), :]
            km_ref[pl.ds(j, 1), :] = jnp.sum(kb.astype(F32), axis=0, keepdims=True) * (1.0 / MOBA_BLOCK)
            return c
        lax.fori_loop(0, nb, body, 0)
        km = km_ref[...]
        km_hi = km.astype(BF16)
        km_lo = (km - km_hi.astype(F32)).astype(BF16)
        for a in range(2):
            in_head = (lane >= a * HEAD_DIM) & (lane < (a + 1) * HEAD_DIM)
            kms_ref[(2 * a) * nb:(2 * a + 1) * nb, :] = jnp.where(in_head, km_hi, jnp.zeros_like(km_hi))
            kms_ref[(2 * a + 1) * nb:(2 * a + 2) * nb, :] = jnp.where(in_head, km_lo, jnp.zeros_like(km_lo))

        n_q = SEL_CHUNK * MOBA_BLOCK
        head_rows = jnp.where(lane // HEAD_DIM == lax.broadcasted_iota(jnp.int32, (BF16_ROWS, PAIR_W), 0),
                              1.0, 0.0).astype(BF16)
        for ci in range(nb // SEL_CHUNK):
            rows = SEL_CHUNK * (ci + 1)
            q = q_ref[0, 0, ci * n_q:(ci + 1) * n_q, :]
            g = _dot_nt(kms_ref[...], q)
            gate = jnp.concatenate([g[(2 * a) * nb:(2 * a) * nb + rows] + g[(2 * a + 1) * nb:(2 * a + 1) * nb + rows]
                                    for a in range(2)], axis=1)
            pos = lax.broadcasted_iota(jnp.int32, (1, n_q), 1)
            own = ci * SEL_CHUNK + jnp.right_shift(pos, MOBA_BLOCK.bit_length() - 1)
            own = jnp.concatenate([own, own], axis=1)
            blk = lax.broadcasted_iota(jnp.int32, gate.shape, 0)
            gate = jnp.where(blk < own, gate, -jnp.inf)
            sel = blk == own
            for _ in range(MOBA_TOPK):
                top = jnp.max(gate, axis=0, keepdims=True)
                first = jnp.min(jnp.where(gate == top, blk, nb), axis=0, keepdims=True)
                pick = (blk == first) & (top > -jnp.inf)
                sel = sel | pick
                gate = jnp.where(pick, -jnp.inf, gate)
            sel_f = jnp.where(sel, 1.0, 0.0)
            own_scores = _dot_nt(head_rows, q * k_ref[0, 0, ci * n_q:(ci + 1) * n_q, :])
            later = jnp.zeros((nb - rows, MOBA_BLOCK), F32)
            for a in range(2):
                for b in range(SEL_CHUNK):
                    lo = a * n_q + b * MOBA_BLOCK
                    chosen_blocks = sel_f[:, lo:lo + MOBA_BLOCK]
                    sel_ref[ci * SEL_CHUNK + b, a] = (jnp.concatenate([chosen_blocks, later], axis=0)
                                                      if rows < nb else chosen_blocks)
                    seed_ref[ci * SEL_CHUNK + b, a] = (own_scores[a:a + 1, b * MOBA_BLOCK:(b + 1) * MOBA_BLOCK]
                                                       + bias_ref[a, 0, 0:1, 0:1])

    for c, (h, a) in enumerate(chains):
        q = q_ref[0, 0, pl.ds(pl.multiple_of((step * Q_BLOCKS + h) * MOBA_BLOCK, MOBA_BLOCK), MOBA_BLOCK), :]
        qa_ref[c] = jnp.where((lane >= a * HEAD_DIM) & (lane < (a + 1) * HEAD_DIM), q, jnp.zeros_like(q))

    def block_of(t):
        return jnp.maximum(j_last - t, 0)

    def key_block(j):
        return k_ref[0, 0, pl.ds(pl.multiple_of(j * MOBA_BLOCK, MOBA_BLOCK), MOBA_BLOCK), :]

    def scores(t, c, kj, near):
        h, a = chains[c]
        s = _dot_nt(kj, qa_ref[c])
        offset = t - (Q_BLOCKS - 1 - h)
        if isinstance(t, int):
            if offset < NEAR_TILES:
                s = s + bias_ref[a, max(offset, 0)]
        elif near:
            s = s + bias_ref[a, jnp.clip(offset, 0, NEAR_TILES)]
        return s

    def visits(t, c):
        return not isinstance(t, int) or t - (Q_BLOCKS - 1 - chains[c][0]) >= 0

    def chosen_row(t, c, j):
        h, a = chains[c]
        return sel_ref[step * Q_BLOCKS + h, a, pl.ds(j, 1), :] > jnp.where(t <= j_last, 0.5, 2.0)

    def score_stage(t, slot, near):
        kj = key_block(block_of(t))
        for c in range(len(chains)):
            s = scores(t, c, kj, near)
            s_ref[slot, c] = s
            mt_ref[slot, c] = jnp.max(s, axis=0, keepdims=True)

    def softmax_stage(t, slot, m_prev):
        j = block_of(t)
        m_next, alphas, pvs = [], [], []
        for c, (h, a) in enumerate(chains):
            chosen = chosen_row(t, c, j)
            m_new = jnp.where(chosen, jnp.maximum(m_prev[c], mt_ref[slot, c]), m_prev[c])
            alphas.append(jnp.exp2(m_prev[c] - m_new))
            p = jnp.exp2(s_ref[slot, c] - jnp.where(chosen, m_new, BIG)).astype(BF16)
            pvs.append(_dot(vt_ref[0, 0, j, a], p))
            m_next.append(m_new)
        for c in range(len(chains)):
            acc_ref[c] = alphas[c] * acc_ref[c] + pvs[c]
        return tuple(m_next)

    def exact_pair(tt, m):
        t = 2 * tt
        score_stage(t + 1, 1, True)
        m = softmax_stage(t, 0, m)
        score_stage(t + 2, 0, True)
        m = softmax_stage(t + 1, 1, m)
        return m

    def lazy_stage(carry, near, score=None, pv=None):
        r, excess = list(carry[0]), list(carry[1])
        if score is not None:
            t, slot = score
            j = block_of(t)
            kj = key_block(j)
            for c in range(len(chains)):
                if not visits(t, c):
                    continue
                s = scores(t, c, kj, near)
                chosen = chosen_row(t, c, j)
                tile_max = jnp.max(s, axis=0, keepdims=True)
                p_ref[slot, c] = jnp.exp2(s - jnp.where(chosen, r[c], BIG)).astype(BF16)
                r_new = jnp.where(chosen, jnp.maximum(r[c], tile_max), r[c])
                alpha_ref[slot, c] = jnp.exp2(r[c] - r_new)
                excess[c] = jnp.maximum(excess[c], jnp.where(chosen, tile_max - r[c], -BIG))
                r[c] = r_new
        if pv is not None:
            t, slot = pv
            j = block_of(t)
            live = [c for c in range(len(chains)) if visits(t, c)]
            pvs = [_dot(vt_ref[0, 0, j, chains[c][1]], p_ref[slot, c]) for c in live]
            for c, prod in zip(live, pvs):
                acc_ref[c] = (acc_ref[c] + prod) * alpha_ref[slot, c]
        return tuple(r), tuple(excess)

    def lazy_pairs(it, carry, *, near, first_pair, pairs_per_trip, single_trip=False):
        if single_trip:
            t = 2 * first_pair
            carry = lazy_stage(carry, near, score=(t, 0))
        else:
            t = 2 * (first_pair + it * pairs_per_trip)
        for _ in range(pairs_per_trip):
            carry = lazy_stage(carry, near, score=(t + 1, 1), pv=(t, 0))
            carry = lazy_stage(carry, near, score=(t + 2, 0), pv=(t + 1, 1))
            t = t + 2
        return carry

    def finalize():
        for h in range(Q_BLOCKS):
            outs = []
            for a in range(2):
                acc = acc_ref[2 * h + a]
                outs.append(acc[0:HEAD_DIM] * (1.0 / acc[HEAD_DIM:HEAD_DIM + 1]))
            row0 = pl.multiple_of((step * Q_BLOCKS + h) * MOBA_BLOCK, MOBA_BLOCK)
            o_ref[0, 0, pl.ds(row0, MOBA_BLOCK), :] = jnp.concatenate(outs, axis=0).T.astype(BF16)

    near_pairs = (NEAR_TILES + Q_BLOCKS) // 2
    n_pairs = (j_last + 2) // 2
    far_pairs = jnp.maximum(n_pairs - near_pairs, 0)
    lowest = (jnp.full((1, MOBA_BLOCK), -BIG, F32),) * len(chains)

    seeds = [seed_ref[step * Q_BLOCKS + h, a] for h, a in chains]

    acc_ref[...] = jnp.zeros_like(acc_ref)
    carry = (tuple(seeds), lowest)
    counts = {(Q_BLOCKS * (i + 1) + 1) // 2 for i in range(nb // Q_BLOCKS)}
    for n in sorted(c for c in counts if c < near_pairs) + [near_pairs]:
        runs = (n_pairs >= n) if n == near_pairs else (n_pairs == n)
        carry = lax.fori_loop(0, jnp.where(runs, 1, 0),
                              functools.partial(lazy_pairs, near=True, first_pair=0, pairs_per_trip=n,
                                                single_trip=True), carry)
    first, left = near_pairs, far_pairs
    for pairs_per_trip in FAR_TRIP_PAIRS:
        trips = left // pairs_per_trip
        carry = lax.fori_loop(0, trips, functools.partial(lazy_pairs, near=False, first_pair=first,
                                                          pairs_per_trip=pairs_per_trip), carry)
        first, left = first + trips * pairs_per_trip, left - trips * pairs_per_trip
    finalize()

    worst = functools.reduce(jnp.maximum, carry[1])
    @pl.when(jnp.max(worst) > LAZY_MAX_EXCESS)
    def _():
        acc_ref[...] = jnp.zeros_like(acc_ref)
        score_stage(0, 0, True)
        lax.fori_loop(0, n_pairs, exact_pair, lowest)
        finalize()


def _attention(q, k, vt, bias_t):
    batch, _, seq, _ = q.shape
    nb = seq // MOBA_BLOCK
    tq = Q_BLOCKS * MOBA_BLOCK
    n_chains = 2 * Q_BLOCKS
    return pl.pallas_call(
        functools.partial(_attn_kernel, nb=nb),
        grid=(batch, HEAD_PAIRS),
        in_specs=[
            pl.BlockSpec((1, 1, seq, PAIR_W), lambda b, p: (b, p, 0, 0)),
            pl.BlockSpec((1, 1, seq, PAIR_W), lambda b, p: (b, p, 0, 0)),
            pl.BlockSpec((1, 1, nb, 2, V_ROWS, MOBA_BLOCK), lambda b, p: (b, p, 0, 0, 0, 0)),
            pl.BlockSpec((2, NEAR_TILES + 1, MOBA_BLOCK, MOBA_BLOCK), lambda b, p: (p, 0, 0, 0)),
        ],
        out_specs=pl.BlockSpec((1, 1, seq, PAIR_W), lambda b, p: (b, p, 0, 0)),
        out_shape=jax.ShapeDtypeStruct((batch, HEAD_PAIRS, seq, PAIR_W), BF16),
        scratch_shapes=[
            pltpu.VMEM((nb, PAIR_W), F32),
            pltpu.VMEM((4 * nb, PAIR_W), BF16),
            pltpu.VMEM((n_chains, MOBA_BLOCK, PAIR_W), BF16),
            pltpu.VMEM((nb, 2, nb, MOBA_BLOCK), F32),
            pltpu.VMEM((nb, 2, 1, MOBA_BLOCK), F32),
            pltpu.VMEM((n_chains, V_ROWS, MOBA_BLOCK), F32),
            pltpu.VMEM((2, n_chains, MOBA_BLOCK, MOBA_BLOCK), F32),
            pltpu.VMEM((2, n_chains, 1, MOBA_BLOCK), F32),
            pltpu.VMEM((2, n_chains, MOBA_BLOCK, MOBA_BLOCK), BF16),
            pltpu.VMEM((2, n_chains, 1, MOBA_BLOCK), F32),
        ],
        compiler_params=pltpu.CompilerParams(dimension_semantics=("arbitrary", "arbitrary"),
                                             vmem_limit_bytes=VMEM_LIMIT),
        name="moba_attention",
    )(q, k, vt, bias_t)


def _post_kernel(x_ref, u_ref, vn_ref, sga_ref, sgb_ref, yb_ref, p_ref,
                 ws_ref, bs_ref, wo_ref, gffn_ref, w1_ref, w2_ref, gple_ref, wpg_ref, wpp_ref, gfin_ref,
                 o_ref, merged_ref, *, tm, final_norm):
    row = lax.broadcasted_iota(jnp.int32, (SGU_CHUNK, SGU_CHUNK), 0)
    col = lax.broadcasted_iota(jnp.int32, (SGU_CHUNK, SGU_CHUNK), 1)
    for g in range(SGU_GROUPS):
        w = jnp.where(row >= col, ws_ref[g], 0.0).astype(BF16)
        b = bs_ref[g]
        cols = slice(g * LANES, (g + 1) * LANES)
        for t in range(tm // SGU_CHUNK):
            rows = slice(t * SGU_CHUNK, (t + 1) * SGU_CHUNK)
            mixed = _dot(w, vn_ref[rows, cols]) + b
            y_a = u_ref[rows, cols].astype(F32) * mixed
            y_b = yb_ref[0, g, rows, :].astype(F32)
            merged = sga_ref[rows, cols].astype(F32) * y_a + sgb_ref[rows, cols].astype(F32) * y_b
            merged_ref[rows, cols] = merged.astype(BF16)

    x1 = x_ref[...] + _dot(merged_ref[...], wo_ref[...])

    h = _rms(x1, gffn_ref[...]).astype(BF16)
    x2 = x1
    for c in range(D_FF // D_MODEL):
        cs = slice(c * D_MODEL, (c + 1) * D_MODEL)
        a = jnp.square(jnp.maximum(_dot(h, w1_ref[:, cs]), 0.0)).astype(BF16)
        x2 = x2 + _dot(a, w2_ref[cs, :])

    gate = jax.nn.sigmoid(_dot(_rms(x2, gple_ref[...]).astype(BF16), wpg_ref[...]))
    x3 = x2 + gate * _dot(p_ref[...].astype(BF16), wpp_ref[...])
    o_ref[...] = _rms(x3, gfin_ref[...]) if final_norm else x3


def _post(x2, u, vn, sga, sgb, yb, p2, ws, bs, wo, gffn, w1, w2, gple, wpg, wpp, gfin, *, seq, tm, final_norm):
    n = x2.shape[0]
    tiles_per_seq = seq // tm
    row = lambda r: (r, 0)
    resident = functools.partial(pl.BlockSpec, pipeline_mode=pl.Buffered(1))
    c2 = lambda r: (0, 0)
    c3 = lambda r: (0, 0, 0)
    tokb = pl.BlockSpec((tm, D_MODEL), row)
    return pl.pallas_call(
        functools.partial(_post_kernel, tm=tm, final_norm=final_norm),
        grid=(n // tm,),
        in_specs=[
            tokb, tokb, tokb, tokb, tokb,
            pl.BlockSpec((1, HEAD_PAIRS, tm, PAIR_W), lambda r: (r // tiles_per_seq, 0, r % tiles_per_seq, 0)),
            pl.BlockSpec((tm, PLE_DIM), row),
            resident((SGU_GROUPS, SGU_CHUNK, SGU_CHUNK), c3),
            resident((SGU_GROUPS, SGU_CHUNK, 1), c3),
            resident((D_MODEL, D_MODEL), c2),
            resident((1, D_MODEL), c2),
            resident((D_MODEL, D_FF), c2),
            resident((D_FF, D_MODEL), c2),
            resident((1, D_MODEL), c2),
            resident((D_MODEL, D_MODEL), c2),
            resident((PLE_DIM, D_MODEL), c2),
            resident((1, D_MODEL), c2),
        ],
        out_specs=pl.BlockSpec((tm, D_MODEL), row),
        out_shape=jax.ShapeDtypeStruct((n, D_MODEL), F32),
        scratch_shapes=[pltpu.VMEM((tm, D_MODEL), BF16)],
        compiler_params=pltpu.CompilerParams(dimension_semantics=("arbitrary",),
                                             vmem_limit_bytes=VMEM_LIMIT),
        name="post",
    )(x2, u, vn, sga, sgb, yb, p2, ws, bs, wo, gffn, w1, w2, gple, wpg, wpp, gfin)


def kernel(x, p, norm_mix_g, w_in, w_sgu_spatial, b_sgu_spatial, ln_v_g, ln_v_b, rel_bias, w_out, norm_ffn_g,
           w_ff1, w_ff2, norm_ple_g, w_ple_gate, w_ple_proj, norm_final_g):
    batch, seq, _ = x.shape
    depth = w_in.shape[0]
    n = batch * seq
    assert seq % (Q_BLOCKS * MOBA_BLOCK) == 0 and seq // MOBA_BLOCK >= MOBA_TOPK
    tm_in = 512
    tm_post = 512
    vec = lambda g: g.reshape(1, D_MODEL)

    bias_t = _bias_tiles(rel_bias)
    x2 = x.reshape(n, D_MODEL)
    for i in range(depth):
        wi = w_in[i].astype(BF16)
        wvt = wi[:, 4 * D_MODEL:5 * D_MODEL].T
        u, vn, q, k, vt, sga, sgb = _in_proj(
            x2, vec(norm_mix_g[i]), wi, wvt, vec(ln_v_g[i]), vec(ln_v_b[i]),
            batch=batch, seq=seq, tm=tm_in)
        yb = _attention(q, k, vt, bias_t)
        x2 = _post(
            x2, u, vn, sga, sgb, yb, p[i].reshape(n, PLE_DIM),
            w_sgu_spatial[i], b_sgu_spatial[i].reshape(SGU_GROUPS, SGU_CHUNK, 1),
            w_out[i].astype(BF16), vec(norm_ffn_g[i]), w_ff1[i].astype(BF16), w_ff2[i].astype(BF16),
            vec(norm_ple_g[i]), w_ple_gate[i].astype(BF16), w_ple_proj[i].astype(BF16),
            vec(norm_final_g), seq=seq, tm=tm_post, final_norm=(i == depth - 1))
    return x2.reshape(batch, seq, D_MODEL)
```

```python
import functools
import math

import jax
import jax.numpy as jnp
from jax import lax
from jax.experimental import pallas as pl
from jax.experimental.pallas import tpu as pltpu

D_MODEL = 1024
PLE_DIM = 256
SGU_CHUNK = 128
SGU_GROUPS = 8
ATT_HEADS = 16
HEAD_DIM = 64
MOBA_BLOCK = 256
MOBA_TOPK = 3
REL_BUCKETS = 32
REL_MAX_DIST = 1024
D_FF = 4 * D_MODEL
EPS = 1e-6

LANES = 128
BF16_ROWS = 16
HEAD_PAIRS = ATT_HEADS // 2
PAIR_W = 2 * HEAD_DIM
V_ROWS = HEAD_DIM + BF16_ROWS
Q_BLOCKS = 4
FAR_TRIP_PAIRS = (4, 2, 1)
SEL_CHUNK = 8
LAZY_MAX_EXCESS = 16.0
BIG = 1e30
VMEM_LIMIT = 60 * 1024 * 1024
LOG2E = math.log2(math.e)

F32 = jnp.float32
BF16 = jnp.bfloat16


def _bucket_thresholds():
    max_exact = REL_BUCKETS // 2
    n_log = REL_BUCKETS - max_exact
    ratio = REL_MAX_DIST // max_exact
    out = []
    for k in range(1, n_log):
        d = max_exact
        target = (max_exact ** n_log) * (ratio ** k)
        while d ** n_log < target:
            d += 1
        out.append(d)
    return tuple(out)


BUCKET_THRESHOLDS = _bucket_thresholds()
NEAR_TILES = -(-(BUCKET_THRESHOLDS[-1] + MOBA_BLOCK - 1) // MOBA_BLOCK)
assert NEAR_TILES == 5


def _dot(a, b):
    return jnp.dot(a, b, preferred_element_type=F32)


def _dot_nt(a, b):
    return lax.dot_general(a, b, (((1,), (1,)), ((), ())), preferred_element_type=F32)


def _rms(x, g):
    return x * lax.rsqrt(jnp.mean(x * x, axis=-1, keepdims=True) + EPS) * g


def _in_proj_kernel(x_ref, g_ref, wa_ref, wvt_ref, wga_ref, wgb_ref, lng_ref, lnb_ref,
                    u_ref, vn_ref, q_ref, k_ref, vt_ref, sga_ref, sgb_ref, *, tm):
    h = _rms(x_ref[...], g_ref[...]).astype(BF16)

    u_ref[...] = jax.nn.gelu(_dot(h, wa_ref[:, 0:D_MODEL])).astype(BF16)

    gv = jax.nn.gelu(_dot(h, wa_ref[:, D_MODEL:2 * D_MODEL]))
    mu = jnp.mean(gv, axis=-1, keepdims=True)
    gc = gv - mu
    vn = gc * lax.rsqrt(jnp.mean(gc * gc, axis=-1, keepdims=True) + EPS)
    vn_ref[...] = (vn * lng_ref[...] + lnb_ref[...]).astype(BF16)

    q = _dot(h, wa_ref[:, 2 * D_MODEL:3 * D_MODEL]) * (HEAD_DIM ** -0.5 * LOG2E)
    k = _dot(h, wa_ref[:, 3 * D_MODEL:4 * D_MODEL])
    for p in range(HEAD_PAIRS):
        q_ref[0, p] = q[:, p * PAIR_W:(p + 1) * PAIR_W].astype(BF16)
        k_ref[0, p] = k[:, p * PAIR_W:(p + 1) * PAIR_W].astype(BF16)

    vt = _dot_nt(wvt_ref[...], h).astype(BF16)
    ones = jnp.ones((BF16_ROWS, MOBA_BLOCK), BF16)
    for p in range(HEAD_PAIRS):
        for jb in range(tm // MOBA_BLOCK):
            for a in range(2):
                r0 = p * PAIR_W + a * HEAD_DIM
                vt_ref[0, p, jb, a, 0:HEAD_DIM, :] = vt[r0:r0 + HEAD_DIM, jb * MOBA_BLOCK:(jb + 1) * MOBA_BLOCK]
                vt_ref[0, p, jb, a, HEAD_DIM:V_ROWS, :] = ones

    sga_ref[...] = jax.nn.sigmoid(_dot(h, wga_ref[...])).astype(BF16)
    sgb_ref[...] = jax.nn.sigmoid(_dot(h, wgb_ref[...])).astype(BF16)


def _in_proj(x2, g, w, wvt, lng, lnb, *, batch, seq, tm):
    n = x2.shape[0]
    tiles_per_seq = seq // tm
    nb = seq // MOBA_BLOCK
    row = lambda r: (r, 0)
    const = lambda r: (0, 0)
    resident = functools.partial(pl.BlockSpec, pipeline_mode=pl.Buffered(1))
    tok = jax.ShapeDtypeStruct((n, D_MODEL), BF16)
    pair_major = jax.ShapeDtypeStruct((batch, HEAD_PAIRS, seq, PAIR_W), BF16)
    return pl.pallas_call(
        functools.partial(_in_proj_kernel, tm=tm),
        grid=(n // tm,),
        in_specs=[
            pl.BlockSpec((tm, D_MODEL), row),
            resident((1, D_MODEL), const),
            resident((D_MODEL, 4 * D_MODEL), const),
            resident((D_MODEL, D_MODEL), const),
            resident((D_MODEL, D_MODEL), lambda r: (0, 5)),
            resident((D_MODEL, D_MODEL), lambda r: (0, 6)),
            resident((1, D_MODEL), const),
            resident((1, D_MODEL), const),
        ],
        out_specs=[
            pl.BlockSpec((tm, D_MODEL), row),
            pl.BlockSpec((tm, D_MODEL), row),
            pl.BlockSpec((1, HEAD_PAIRS, tm, PAIR_W), lambda r: (r // tiles_per_seq, 0, r % tiles_per_seq, 0)),
            pl.BlockSpec((1, HEAD_PAIRS, tm, PAIR_W), lambda r: (r // tiles_per_seq, 0, r % tiles_per_seq, 0)),
            pl.BlockSpec((1, HEAD_PAIRS, tm // MOBA_BLOCK, 2, V_ROWS, MOBA_BLOCK),
                         lambda r: (r // tiles_per_seq, 0, r % tiles_per_seq, 0, 0, 0)),
            pl.BlockSpec((tm, D_MODEL), row),
            pl.BlockSpec((tm, D_MODEL), row),
        ],
        out_shape=[tok, tok, pair_major, pair_major,
                   jax.ShapeDtypeStruct((batch, HEAD_PAIRS, nb, 2, V_ROWS, MOBA_BLOCK), BF16),
                   tok, tok],
        compiler_params=pltpu.CompilerParams(dimension_semantics=("arbitrary",),
                                             vmem_limit_bytes=VMEM_LIMIT),
        name="in_proj",
    )(x2, g, w, wvt, w, w, lng, lnb)


def _bias_tiles_kernel(tab_ref, o_ref):
    h = pl.program_id(0)
    key = lax.broadcasted_iota(jnp.int32, (MOBA_BLOCK, MOBA_BLOCK), 0)
    qry = lax.broadcasted_iota(jnp.int32, (MOBA_BLOCK, MOBA_BLOCK), 1)
    max_exact = REL_BUCKETS // 2
    last = tab_ref[REL_BUCKETS - 1, h]
    for t in range(NEAR_TILES):
        dist = t * MOBA_BLOCK + qry - key
        n = jnp.maximum(dist, 0)
        val = jnp.full((MOBA_BLOCK, MOBA_BLOCK), tab_ref[0, h], F32)
        for b in range(1, max_exact + 1):
            val = jnp.where(n >= b, tab_ref[b, h], val)
        for kk, thr in enumerate(BUCKET_THRESHOLDS):
            val = jnp.where(n >= thr, tab_ref[max_exact + 1 + kk, h], val)
        val = (val - last) * LOG2E
        if t == 0:
            val = jnp.where(dist >= 0, val, -BIG)
        o_ref[0, t] = val
    o_ref[0, NEAR_TILES] = jnp.zeros((MOBA_BLOCK, MOBA_BLOCK), F32)


def _bias_tiles(rel_bias):
    return pl.pallas_call(
        _bias_tiles_kernel,
        grid=(ATT_HEADS,),
        in_specs=[pl.BlockSpec(memory_space=pltpu.SMEM)],
        out_specs=pl.BlockSpec((1, NEAR_TILES + 1, MOBA_BLOCK, MOBA_BLOCK), lambda h: (h, 0, 0, 0)),
        out_shape=jax.ShapeDtypeStruct((ATT_HEADS, NEAR_TILES + 1, MOBA_BLOCK, MOBA_BLOCK), F32),
        compiler_params=pltpu.CompilerParams(dimension_semantics=("arbitrary",)),
        name="bias_tiles",
    )(rel_bias)


def _attn_kernel(*refs, nb):
    n_tiles = nb // Q_BLOCKS
    worst = lax.fori_loop(0, n_tiles, lambda step, w: jnp.maximum(w, _attn_tile(step, *refs, nb=nb, exact=False)),
                          jnp.full((1, MOBA_BLOCK), -BIG, F32))

    @pl.when(jnp.max(worst) > LAZY_MAX_EXCESS)
    def _():
        def tile(step, carry):
            _attn_tile(step, *refs, nb=nb, exact=True)
            return carry
        lax.fori_loop(0, n_tiles, tile, 0)


def _attn_tile(step, q_ref, k_ref, vt_ref, bias_ref, o_ref,
               km_ref, kms_ref, qa_ref, sel_ref, seed_ref, acc_ref, s_ref, mt_ref, p_ref, alpha_ref, *, nb, exact):
    j_last = step * Q_BLOCKS + (Q_BLOCKS - 1)
    chains = [(h, a) for h in range(Q_BLOCKS) for a in range(2)]
    lane = lax.broadcasted_iota(jnp.int32, (1, PAIR_W), 1)

    def prepare():
        def body(j, c):
            kb = k_ref[0, 0, pl.ds(pl.multiple_of(j * MOBA_BLOCK, MOBA_BLOCK), MOBA_BLOCK), :]
            km_ref[pl.ds(j, 1), :] = jnp.sum(kb.astype(F32), axis=0, keepdims=True) * (1.0 / MOBA_BLOCK)
            return c
        lax.fori_loop(0, nb, body, 0)
        km = km_ref[...]
        km_hi = km.astype(BF16)
        km_lo = (km - km_hi.astype(F32)).astype(BF16)
        for a in range(2):
            in_head = (lane >= a * HEAD_DIM) & (lane < (a + 1) * HEAD_DIM)
            kms_ref[(2 * a) * nb:(2 * a + 1) * nb, :] = jnp.where(in_head, km_hi, jnp.zeros_like(km_hi))
            kms_ref[(2 * a + 1) * nb:(2 * a + 2) * nb, :] = jnp.where(in_head, km_lo, jnp.zeros_like(km_lo))

        n_q = SEL_CHUNK * MOBA_BLOCK
        head_rows = jnp.where(lane // HEAD_DIM == lax.broadcasted_iota(jnp.int32, (BF16_ROWS, PAIR_W), 0),
                              1.0, 0.0).astype(BF16)
        for ci in range(nb // SEL_CHUNK):
            rows = SEL_CHUNK * (ci + 1)
            q = q_ref[0, 0, ci * n_q:(ci + 1) * n_q, :]
            g = _dot_nt(kms_ref[...], q)
            gate = jnp.concatenate([g[(2 * a) * nb:(2 * a) * nb + rows] + g[(2 * a + 1) * nb:(2 * a + 1) * nb + rows]
                                    for a in range(2)], axis=1)
            pos = lax.broadcasted_iota(jnp.int32, (1, n_q), 1)
            own = ci * SEL_CHUNK + jnp.right_shift(pos, MOBA_BLOCK.bit_length() - 1)
            own = jnp.concatenate([own, own], axis=1)
            blk = lax.broadcasted_iota(jnp.int32, gate.shape, 0)
            gate = jnp.where(blk < own, gate, -jnp.inf)
            sel = blk == own
            for _ in range(MOBA_TOPK):
                top = jnp.max(gate, axis=0, keepdims=True)
                first = jnp.min(jnp.where(gate == top, blk, nb), axis=0, keepdims=True)
                pick = (blk == first) & (top > -jnp.inf)
                sel = sel | pick
                gate = jnp.where(pick, -jnp.inf, gate)
            sel_f = jnp.where(sel, 1.0, 0.0)
            own_scores = _dot_nt(head_rows, q * k_ref[0, 0, ci * n_q:(ci + 1) * n_q, :])
            later = jnp.zeros((nb - rows, MOBA_BLOCK), F32)
            for a in range(2):
                for b in range(SEL_CHUNK):
                    lo = a * n_q + b * MOBA_BLOCK
                    chosen_blocks = sel_f[:, lo:lo + MOBA_BLOCK]
                    sel_ref[ci * SEL_CHUNK + b, a] = (jnp.concatenate([chosen_blocks, later], axis=0)
                                                      if rows < nb else chosen_blocks)
                    seed_ref[ci * SEL_CHUNK + b, a] = (own_scores[a:a + 1, b * MOBA_BLOCK:(b + 1) * MOBA_BLOCK]
                                                       + bias_ref[a, 0, 0:1, 0:1])

    if not exact:
        pl.when(step == 0)(prepare)

    for c, (h, a) in enumerate(chains):
        q = q_ref[0, 0, pl.ds(pl.multiple_of((step * Q_BLOCKS + h) * MOBA_BLOCK, MOBA_BLOCK), MOBA_BLOCK), :]
        qa_ref[c] = jnp.where((lane >= a * HEAD_DIM) & (lane < (a + 1) * HEAD_DIM), q, jnp.zeros_like(q))

    def block_of(t):
        return jnp.maximum(j_last - t, 0)

    def key_block(j):
        return k_ref[0, 0, pl.ds(pl.multiple_of(j * MOBA_BLOCK, MOBA_BLOCK), MOBA_BLOCK), :]

    def scores(t, c, kj, near):
        h, a = chains[c]
        s = _dot_nt(kj, qa_ref[c])
        offset = t - (Q_BLOCKS - 1 - h)
        if isinstance(t, int):
            if offset < NEAR_TILES:
                s = s + bias_ref[a, max(offset, 0)]
        elif near:
            s = s + bias_ref[a, jnp.clip(offset, 0, NEAR_TILES)]
        return s

    def visits(t, c):
        return not isinstance(t, int) or t - (Q_BLOCKS - 1 - chains[c][0]) >= 0

    def chosen_row(t, c, j):
        h, a = chains[c]
        return sel_ref[step * Q_BLOCKS + h, a, pl.ds(j, 1), :] > jnp.where(t <= j_last, 0.5, 2.0)

    def score_stage(t, slot, near):
        kj = key_block(block_of(t))
        for c in range(len(chains)):
            s = scores(t, c, kj, near)
            s_ref[slot, c] = s
            mt_ref[slot, c] = jnp.max(s, axis=0, keepdims=True)

    def softmax_stage(t, slot, m_prev):
        j = block_of(t)
        m_next, alphas, pvs = [], [], []
        for c, (h, a) in enumerate(chains):
            chosen = chosen_row(t, c, j)
            m_new = jnp.where(chosen, jnp.maximum(m_prev[c], mt_ref[slot, c]), m_prev[c])
            alphas.append(jnp.exp2(m_prev[c] - m_new))
            p = jnp.exp2(s_ref[slot, c] - jnp.where(chosen, m_new, BIG)).astype(BF16)
            pvs.append(_dot(vt_ref[0, 0, j, a], p))
            m_next.append(m_new)
        for c in range(len(chains)):
            acc_ref[c] = alphas[c] * acc_ref[c] + pvs[c]
        return tuple(m_next)

    def exact_pair(tt, m):
        t = 2 * tt
        score_stage(t + 1, 1, True)
        m = softmax_stage(t, 0, m)
        score_stage(t + 2, 0, True)
        m = softmax_stage(t + 1, 1, m)
        return m

    def lazy_stage(carry, near, score=None, pv=None):
        r, excess = list(carry[0]), list(carry[1])
        if score is not None:
            t, slot = score
            j = block_of(t)
            kj = key_block(j)
            for c in range(len(chains)):
                if not visits(t, c):
                    continue
                s = scores(t, c, kj, near)
                chosen = chosen_row(t, c, j)
                tile_max = jnp.max(s, axis=0, keepdims=True)
                p_ref[slot, c] = jnp.exp2(s - jnp.where(chosen, r[c], BIG)).astype(BF16)
                r_new = jnp.where(chosen, jnp.maximum(r[c], tile_max), r[c])
                alpha_ref[slot, c] = jnp.exp2(r[c] - r_new)
                excess[c] = jnp.maximum(excess[c], jnp.where(chosen, tile_max - r[c], -BIG))
                r[c] = r_new
        if pv is not None:
            t, slot = pv
            j = block_of(t)
            live = [c for c in range(len(chains)) if visits(t, c)]
            pvs = [_dot(vt_ref[0, 0, j, chains[c][1]], p_ref[slot, c]) for c in live]
            for c, prod in zip(live, pvs):
                acc_ref[c] = (acc_ref[c] + prod) * alpha_ref[slot, c]
        return tuple(r), tuple(excess)

    def lazy_pairs(it, carry, *, near, first_pair, pairs_per_trip, single_trip=False):
        if single_trip:
            t = 2 * first_pair
            carry = lazy_stage(carry, near, score=(t, 0))
        else:
            t = 2 * (first_pair + it * pairs_per_trip)
        for _ in range(pairs_per_trip):
            carry = lazy_stage(carry, near, score=(t + 1, 1), pv=(t, 0))
            carry = lazy_stage(carry, near, score=(t + 2, 0), pv=(t + 1, 1))
            t = t + 2
        return carry

    def finalize():
        for h in range(Q_BLOCKS):
            outs = []
            for a in range(2):
                acc = acc_ref[2 * h + a]
                outs.append(acc[0:HEAD_DIM] * (1.0 / acc[HEAD_DIM:HEAD_DIM + 1]))
            row0 = pl.multiple_of((step * Q_BLOCKS + h) * MOBA_BLOCK, MOBA_BLOCK)
            o_ref[0, 0, pl.ds(row0, MOBA_BLOCK), :] = jnp.concatenate(outs, axis=0).T.astype(BF16)

    near_pairs = (NEAR_TILES + Q_BLOCKS) // 2
    n_pairs = (j_last + 2) // 2
    far_pairs = jnp.maximum(n_pairs - near_pairs, 0)
    lowest = (jnp.full((1, MOBA_BLOCK), -BIG, F32),) * len(chains)

    acc_ref[...] = jnp.zeros_like(acc_ref)
    if exact:
        score_stage(0, 0, True)
        lax.fori_loop(0, n_pairs, exact_pair, lowest)
        finalize()
        return None

    seeds = [seed_ref[step * Q_BLOCKS + h, a] for h, a in chains]
    carry = (tuple(seeds), lowest)
    counts = {(Q_BLOCKS * (i + 1) + 1) // 2 for i in range(nb // Q_BLOCKS)}
    for n in sorted(c for c in counts if c < near_pairs) + [near_pairs]:
        runs = (n_pairs >= n) if n == near_pairs else (n_pairs == n)
        carry = lax.fori_loop(0, jnp.where(runs, 1, 0),
                              functools.partial(lazy_pairs, near=True, first_pair=0, pairs_per_trip=n,
                                                single_trip=True), carry)
    first, left = near_pairs, far_pairs
    for pairs_per_trip in FAR_TRIP_PAIRS:
        trips = left // pairs_per_trip
        carry = lax.fori_loop(0, trips, functools.partial(lazy_pairs, near=False, first_pair=first,
                                                          pairs_per_trip=pairs_per_trip), carry)
        first, left = first + trips * pairs_per_trip, left - trips * pairs_per_trip
    finalize()
    return functools.reduce(jnp.maximum, carry[1])


def _attention(q, k, vt, bias_t):
    batch, _, seq, _ = q.shape
    nb = seq // MOBA_BLOCK
    tq = Q_BLOCKS * MOBA_BLOCK
    n_chains = 2 * Q_BLOCKS
    return pl.pallas_call(
        functools.partial(_attn_kernel, nb=nb),
        grid=(batch, HEAD_PAIRS),
        in_specs=[
            pl.BlockSpec((1, 1, seq, PAIR_W), lambda b, p: (b, p, 0, 0)),
            pl.BlockSpec((1, 1, seq, PAIR_W), lambda b, p: (b, p, 0, 0)),
            pl.BlockSpec((1, 1, nb, 2, V_ROWS, MOBA_BLOCK), lambda b, p: (b, p, 0, 0, 0, 0)),
            pl.BlockSpec((2, NEAR_TILES + 1, MOBA_BLOCK, MOBA_BLOCK), lambda b, p: (p, 0, 0, 0)),
        ],
        out_specs=pl.BlockSpec((1, 1, seq, PAIR_W), lambda b, p: (b, p, 0, 0)),
        out_shape=jax.ShapeDtypeStruct((batch, HEAD_PAIRS, seq, PAIR_W), BF16),
        scratch_shapes=[
            pltpu.VMEM((nb, PAIR_W), F32),
            pltpu.VMEM((4 * nb, PAIR_W), BF16),
            pltpu.VMEM((n_chains, MOBA_BLOCK, PAIR_W), BF16),
            pltpu.VMEM((nb, 2, nb, MOBA_BLOCK), F32),
            pltpu.VMEM((nb, 2, 1, MOBA_BLOCK), F32),
            pltpu.VMEM((n_chains, V_ROWS, MOBA_BLOCK), F32),
            pltpu.VMEM((2, n_chains, MOBA_BLOCK, MOBA_BLOCK), F32),
            pltpu.VMEM((2, n_chains, 1, MOBA_BLOCK), F32),
            pltpu.VMEM((2, n_chains, MOBA_BLOCK, MOBA_BLOCK), BF16),
            pltpu.VMEM((2, n_chains, 1, MOBA_BLOCK), F32),
        ],
        compiler_params=pltpu.CompilerParams(dimension_semantics=("arbitrary", "arbitrary"),
                                             vmem_limit_bytes=VMEM_LIMIT),
        name="moba_attention",
    )(q, k, vt, bias_t)


def _post_kernel(x_ref, u_ref, vn_ref, sga_ref, sgb_ref, yb_ref, p_ref,
                 ws_ref, bs_ref, wo_ref, gffn_ref, w1_ref, w2_ref, gple_ref, wpg_ref, wpp_ref, gfin_ref,
                 o_ref, merged_ref, *, tm, final_norm):
    row = lax.broadcasted_iota(jnp.int32, (SGU_CHUNK, SGU_CHUNK), 0)
    col = lax.broadcasted_iota(jnp.int32, (SGU_CHUNK, SGU_CHUNK), 1)
    for g in range(SGU_GROUPS):
        w = jnp.where(row >= col, ws_ref[g], 0.0).astype(BF16)
        b = bs_ref[g]
        cols = slice(g * LANES, (g + 1) * LANES)
        for t in range(tm // SGU_CHUNK):
            rows = slice(t * SGU_CHUNK, (t + 1) * SGU_CHUNK)
            mixed = _dot(w, vn_ref[rows, cols]) + b
            y_a = u_ref[rows, cols].astype(F32) * mixed
            y_b = yb_ref[0, g, rows, :].astype(F32)
            merged = sga_ref[rows, cols].astype(F32) * y_a + sgb_ref[rows, cols].astype(F32) * y_b
            merged_ref[rows, cols] = merged.astype(BF16)

    x1 = x_ref[...] + _dot(merged_ref[...], wo_ref[...])

    h = _rms(x1, gffn_ref[...]).astype(BF16)
    x2 = x1
    for c in range(D_FF // D_MODEL):
        cs = slice(c * D_MODEL, (c + 1) * D_MODEL)
        a = jnp.square(jnp.maximum(_dot(h, w1_ref[:, cs]), 0.0)).astype(BF16)
        x2 = x2 + _dot(a, w2_ref[cs, :])

    gate = jax.nn.sigmoid(_dot(_rms(x2, gple_ref[...]).astype(BF16), wpg_ref[...]))
    x3 = x2 + gate * _dot(p_ref[...].astype(BF16), wpp_ref[...])
    o_ref[...] = _rms(x3, gfin_ref[...]) if final_norm else x3


def _post(x2, u, vn, sga, sgb, yb, p2, ws, bs, wo, gffn, w1, w2, gple, wpg, wpp, gfin, *, seq, tm, final_norm):
    n = x2.shape[0]
    tiles_per_seq = seq // tm
    row = lambda r: (r, 0)
    resident = functools.partial(pl.BlockSpec, pipeline_mode=pl.Buffered(1))
    c2 = lambda r: (0, 0)
    c3 = lambda r: (0, 0, 0)
    tokb = pl.BlockSpec((tm, D_MODEL), row)
    return pl.pallas_call(
        functools.partial(_post_kernel, tm=tm, final_norm=final_norm),
        grid=(n // tm,),
        in_specs=[
            tokb, tokb, tokb, tokb, tokb,
            pl.BlockSpec((1, HEAD_PAIRS, tm, PAIR_W), lambda r: (r // tiles_per_seq, 0, r % tiles_per_seq, 0)),
            pl.BlockSpec((tm, PLE_DIM), row),
            resident((SGU_GROUPS, SGU_CHUNK, SGU_CHUNK), c3),
            resident((SGU_GROUPS, SGU_CHUNK, 1), c3),
            resident((D_MODEL, D_MODEL), c2),
            resident((1, D_MODEL), c2),
            resident((D_MODEL, D_FF), c2),
            resident((D_FF, D_MODEL), c2),
            resident((1, D_MODEL), c2),
            resident((D_MODEL, D_MODEL), c2),
            resident((PLE_DIM, D_MODEL), c2),
            resident((1, D_MODEL), c2),
        ],
        out_specs=pl.BlockSpec((tm, D_MODEL), row),
        out_shape=jax.ShapeDtypeStruct((n, D_MODEL), F32),
        scratch_shapes=[pltpu.VMEM((tm, D_MODEL), BF16)],
        compiler_params=pltpu.CompilerParams(dimension_semantics=("arbitrary",),
                                             vmem_limit_bytes=VMEM_LIMIT),
        name="post",
    )(x2, u, vn, sga, sgb, yb, p2, ws, bs, wo, gffn, w1, w2, gple, wpg, wpp, gfin)


def kernel(x, p, norm_mix_g, w_in, w_sgu_spatial, b_sgu_spatial, ln_v_g, ln_v_b, rel_bias, w_out, norm_ffn_g,
           w_ff1, w_ff2, norm_ple_g, w_ple_gate, w_ple_proj, norm_final_g):
    batch, seq, _ = x.shape
    depth = w_in.shape[0]
    n = batch * seq
    assert seq % (Q_BLOCKS * MOBA_BLOCK) == 0 and seq // MOBA_BLOCK >= MOBA_TOPK
    tm_in = 512
    tm_post = 512
    vec = lambda g: g.reshape(1, D_MODEL)

    bias_t = _bias_tiles(rel_bias)
    x2 = x.reshape(n, D_MODEL)
    for i in range(depth):
        wi = w_in[i].astype(BF16)
        wvt = wi[:, 4 * D_MODEL:5 * D_MODEL].T
        u, vn, q, k, vt, sga, sgb = _in_proj(
            x2, vec(norm_mix_g[i]), wi, wvt, vec(ln_v_g[i]), vec(ln_v_b[i]),
            batch=batch, seq=seq, tm=tm_in)
        yb = _attention(q, k, vt, bias_t)
        x2 = _post(
            x2, u, vn, sga, sgb, yb, p[i].reshape(n, PLE_DIM),
            w_sgu_spatial[i], b_sgu_spatial[i].reshape(SGU_GROUPS, SGU_CHUNK, 1),
            w_out[i].astype(BF16), vec(norm_ffn_g[i]), w_ff1[i].astype(BF16), w_ff2[i].astype(BF16),
            vec(norm_ple_g[i]), w_ple_gate[i].astype(BF16), w_ple_proj[i].astype(BF16),
            vec(norm_final_g), seq=seq, tm=tm_post, final_norm=(i == depth - 1))
    return x2.reshape(batch, seq, D_MODEL)
```

```python
import functools
import math

import jax
import jax.numpy as jnp
from jax import lax
from jax.experimental import pallas as pl
from jax.experimental.pallas import tpu as pltpu

D_MODEL = 1024
PLE_DIM = 256
SGU_CHUNK = 128
SGU_GROUPS = 8
ATT_HEADS = 16
HEAD_DIM = 64
MOBA_BLOCK = 256
MOBA_TOPK = 3
REL_BUCKETS = 32
REL_MAX_DIST = 1024
D_FF = 4 * D_MODEL
EPS = 1e-6

LANES = 128
BF16_ROWS = 16
HEAD_PAIRS = ATT_HEADS // 2
PAIR_W = 2 * HEAD_DIM
V_ROWS = HEAD_DIM + BF16_ROWS
Q_BLOCKS = 4
FAR_TRIP_PAIRS = (4, 2, 1)
SEL_CHUNK = 8
LAZY_MAX_EXCESS = 16.0
BIG = 1e30
VMEM_LIMIT = 60 * 1024 * 1024
LOG2E = math.log2(math.e)

F32 = jnp.float32
BF16 = jnp.bfloat16


def _bucket_thresholds():
    max_exact = REL_BUCKETS // 2
    n_log = REL_BUCKETS - max_exact
    ratio = REL_MAX_DIST // max_exact
    out = []
    for k in range(1, n_log):
        d = max_exact
        target = (max_exact ** n_log) * (ratio ** k)
        while d ** n_log < target:
            d += 1
        out.append(d)
    return tuple(out)


BUCKET_THRESHOLDS = _bucket_thresholds()
NEAR_TILES = -(-(BUCKET_THRESHOLDS[-1] + MOBA_BLOCK - 1) // MOBA_BLOCK)
assert NEAR_TILES == 5


def _dot(a, b):
    return jnp.dot(a, b, preferred_element_type=F32)


def _dot_nt(a, b):
    return lax.dot_general(a, b, (((1,), (1,)), ((), ())), preferred_element_type=F32)


def _rms(x, g):
    return x * lax.rsqrt(jnp.mean(x * x, axis=-1, keepdims=True) + EPS) * g


def _in_proj_kernel(x_ref, g_ref, wa_ref, wvt_ref, wga_ref, wgb_ref, lng_ref, lnb_ref,
                    u_ref, vn_ref, q_ref, k_ref, vt_ref, sga_ref, sgb_ref, *, tm):
    h = _rms(x_ref[...], g_ref[...]).astype(BF16)

    u_ref[...] = jax.nn.gelu(_dot(h, wa_ref[:, 0:D_MODEL])).astype(BF16)

    gv = jax.nn.gelu(_dot(h, wa_ref[:, D_MODEL:2 * D_MODEL]))
    mu = jnp.mean(gv, axis=-1, keepdims=True)
    gc = gv - mu
    vn = gc * lax.rsqrt(jnp.mean(gc * gc, axis=-1, keepdims=True) + EPS)
    vn_ref[...] = (vn * lng_ref[...] + lnb_ref[...]).astype(BF16)

    q = _dot(h, wa_ref[:, 2 * D_MODEL:3 * D_MODEL]) * (HEAD_DIM ** -0.5 * LOG2E)
    k = _dot(h, wa_ref[:, 3 * D_MODEL:4 * D_MODEL])
    for p in range(HEAD_PAIRS):
        q_ref[0, p] = q[:, p * PAIR_W:(p + 1) * PAIR_W].astype(BF16)
        k_ref[0, p] = k[:, p * PAIR_W:(p + 1) * PAIR_W].astype(BF16)

    vt = _dot_nt(wvt_ref[...], h).astype(BF16)
    ones = jnp.ones((BF16_ROWS, MOBA_BLOCK), BF16)
    for p in range(HEAD_PAIRS):
        for jb in range(tm // MOBA_BLOCK):
            for a in range(2):
                r0 = p * PAIR_W + a * HEAD_DIM
                vt_ref[0, p, jb, a, 0:HEAD_DIM, :] = vt[r0:r0 + HEAD_DIM, jb * MOBA_BLOCK:(jb + 1) * MOBA_BLOCK]
                vt_ref[0, p, jb, a, HEAD_DIM:V_ROWS, :] = ones

    sga_ref[...] = jax.nn.sigmoid(_dot(h, wga_ref[...])).astype(BF16)
    sgb_ref[...] = jax.nn.sigmoid(_dot(h, wgb_ref[...])).astype(BF16)


def _in_proj(x2, g, w, wvt, lng, lnb, *, batch, seq, tm):
    n = x2.shape[0]
    tiles_per_seq = seq // tm
    nb = seq // MOBA_BLOCK
    row = lambda r: (r, 0)
    const = lambda r: (0, 0)
    resident = functools.partial(pl.BlockSpec, pipeline_mode=pl.Buffered(1))
    tok = jax.ShapeDtypeStruct((n, D_MODEL), BF16)
    pair_major = jax.ShapeDtypeStruct((batch, HEAD_PAIRS, seq, PAIR_W), BF16)
    return pl.pallas_call(
        functools.partial(_in_proj_kernel, tm=tm),
        grid=(n // tm,),
        in_specs=[
            pl.BlockSpec((tm, D_MODEL), row),
            resident((1, D_MODEL), const),
            resident((D_MODEL, 4 * D_MODEL), const),
            resident((D_MODEL, D_MODEL), const),
            resident((D_MODEL, D_MODEL), lambda r: (0, 5)),
            resident((D_MODEL, D_MODEL), lambda r: (0, 6)),
            resident((1, D_MODEL), const),
            resident((1, D_MODEL), const),
        ],
        out_specs=[
            pl.BlockSpec((tm, D_MODEL), row),
            pl.BlockSpec((tm, D_MODEL), row),
            pl.BlockSpec((1, HEAD_PAIRS, tm, PAIR_W), lambda r: (r // tiles_per_seq, 0, r % tiles_per_seq, 0)),
            pl.BlockSpec((1, HEAD_PAIRS, tm, PAIR_W), lambda r: (r // tiles_per_seq, 0, r % tiles_per_seq, 0)),
            pl.BlockSpec((1, HEAD_PAIRS, tm // MOBA_BLOCK, 2, V_ROWS, MOBA_BLOCK),
                         lambda r: (r // tiles_per_seq, 0, r % tiles_per_seq, 0, 0, 0)),
            pl.BlockSpec((tm, D_MODEL), row),
            pl.BlockSpec((tm, D_MODEL), row),
        ],
        out_shape=[tok, tok, pair_major, pair_major,
                   jax.ShapeDtypeStruct((batch, HEAD_PAIRS, nb, 2, V_ROWS, MOBA_BLOCK), BF16),
                   tok, tok],
        compiler_params=pltpu.CompilerParams(dimension_semantics=("arbitrary",),
                                             vmem_limit_bytes=VMEM_LIMIT),
        name="in_proj",
    )(x2, g, w, wvt, w, w, lng, lnb)


def _bias_tiles_kernel(tab_ref, o_ref):
    h = pl.program_id(0)
    key = lax.broadcasted_iota(jnp.int32, (MOBA_BLOCK, MOBA_BLOCK), 0)
    qry = lax.broadcasted_iota(jnp.int32, (MOBA_BLOCK, MOBA_BLOCK), 1)
    max_exact = REL_BUCKETS // 2
    last = tab_ref[REL_BUCKETS - 1, h]
    for t in range(NEAR_TILES):
        dist = t * MOBA_BLOCK + qry - key
        n = jnp.maximum(dist, 0)
        val = jnp.full((MOBA_BLOCK, MOBA_BLOCK), tab_ref[0, h], F32)
        for b in range(1, max_exact + 1):
            val = jnp.where(n >= b, tab_ref[b, h], val)
        for kk, thr in enumerate(BUCKET_THRESHOLDS):
            val = jnp.where(n >= thr, tab_ref[max_exact + 1 + kk, h], val)
        val = (val - last) * LOG2E
        if t == 0:
            val = jnp.where(dist >= 0, val, -BIG)
        o_ref[0, t] = val
    o_ref[0, NEAR_TILES] = jnp.zeros((MOBA_BLOCK, MOBA_BLOCK), F32)


def _bias_tiles(rel_bias):
    return pl.pallas_call(
        _bias_tiles_kernel,
        grid=(ATT_HEADS,),
        in_specs=[pl.BlockSpec(memory_space=pltpu.SMEM)],
        out_specs=pl.BlockSpec((1, NEAR_TILES + 1, MOBA_BLOCK, MOBA_BLOCK), lambda h: (h, 0, 0, 0)),
        out_shape=jax.ShapeDtypeStruct((ATT_HEADS, NEAR_TILES + 1, MOBA_BLOCK, MOBA_BLOCK), F32),
        compiler_params=pltpu.CompilerParams(dimension_semantics=("arbitrary",)),
        name="bias_tiles",
    )(rel_bias)


def _attn_kernel(*refs, nb):
    n_tiles = nb // Q_BLOCKS
    worst = lax.fori_loop(0, n_tiles, lambda step, w: jnp.maximum(w, _attn_tile(step, *refs, nb=nb, exact=False)),
                          jnp.full((1, MOBA_BLOCK), -BIG, F32))

    @pl.when(jnp.max(worst) > LAZY_MAX_EXCESS)
    def _():
        def tile(step, carry):
            _attn_tile(step, *refs, nb=nb, exact=True)
            return carry
        lax.fori_loop(0, n_tiles, tile, 0)


def _attn_tile(step, q_ref, k_ref, vt_ref, bias_ref, o_ref,
               km_ref, kms_ref, qa_ref, sel_ref, seed_ref, acc_ref, s_ref, mt_ref, p_ref, alpha_ref, *, nb, exact):
    j_last = step * Q_BLOCKS + (Q_BLOCKS - 1)
    chains = [(h, a) for h in range(Q_BLOCKS) for a in range(2)]
    lane = lax.broadcasted_iota(jnp.int32, (1, PAIR_W), 1)

    def prepare():
        def body(j, c):
            kb = k_ref[0, 0, pl.ds(pl.multiple_of(j * MOBA_BLOCK, MOBA_BLOCK), MOBA_BLOCK), :]
            km_ref[pl.ds(j, 1), :] = jnp.sum(kb.astype(F32), axis=0, keepdims=True) * (1.0 / MOBA_BLOCK)
            return c
        lax.fori_loop(0, nb, body, 0)
        km = km_ref[...]
        km_hi = km.astype(BF16)
        km_lo = (km - km_hi.astype(F32)).astype(BF16)
        for a in range(2):
            in_head = (lane >= a * HEAD_DIM) & (lane < (a + 1) * HEAD_DIM)
            kms_ref[(2 * a) * nb:(2 * a + 1) * nb, :] = jnp.where(in_head, km_hi, jnp.zeros_like(km_hi))
            kms_ref[(2 * a + 1) * nb:(2 * a + 2) * nb, :] = jnp.where(in_head, km_lo, jnp.zeros_like(km_lo))

        n_q = SEL_CHUNK * MOBA_BLOCK
        head_rows = jnp.where(lane // HEAD_DIM == lax.broadcasted_iota(jnp.int32, (BF16_ROWS, PAIR_W), 0),
                              1.0, 0.0).astype(BF16)
        for ci in range(nb // SEL_CHUNK):
            rows = SEL_CHUNK * (ci + 1)
            q = q_ref[0, 0, ci * n_q:(ci + 1) * n_q, :]
            g = _dot_nt(kms_ref[...], q)
            gate = jnp.concatenate([g[(2 * a) * nb:(2 * a) * nb + rows] + g[(2 * a + 1) * nb:(2 * a + 1) * nb + rows]
                                    for a in range(2)], axis=1)
            pos = lax.broadcasted_iota(jnp.int32, (1, n_q), 1)
            own = ci * SEL_CHUNK + jnp.right_shift(pos, MOBA_BLOCK.bit_length() - 1)
            own = jnp.concatenate([own, own], axis=1)
            blk = lax.broadcasted_iota(jnp.int32, gate.shape, 0)
            gate = jnp.where(blk < own, gate, -jnp.inf)
            sel = blk == own
            for _ in range(MOBA_TOPK):
                top = jnp.max(gate, axis=0, keepdims=True)
                first = jnp.min(jnp.where(gate == top, blk, nb), axis=0, keepdims=True)
                pick = (blk == first) & (top > -jnp.inf)
                sel = sel | pick
                gate = jnp.where(pick, -jnp.inf, gate)
            sel_f = jnp.where(sel, 1.0, 0.0)
            own_scores = _dot_nt(head_rows, q * k_ref[0, 0, ci * n_q:(ci + 1) * n_q, :])
            later = jnp.zeros((nb - rows, MOBA_BLOCK), F32)
            for a in range(2):
                for b in range(SEL_CHUNK):
                    lo = a * n_q + b * MOBA_BLOCK
                    chosen_blocks = sel_f[:, lo:lo + MOBA_BLOCK]
                    sel_ref[ci * SEL_CHUNK + b, a] = (jnp.concatenate([chosen_blocks, later], axis=0)
                                                      if rows < nb else chosen_blocks)
                    seed_ref[ci * SEL_CHUNK + b, a] = (own_scores[a:a + 1, b * MOBA_BLOCK:(b + 1) * MOBA_BLOCK]
                                                       + bias_ref[a, 0, 0:1, 0:1])

    if not exact:
        pl.when(step == 0)(prepare)

    for c, (h, a) in enumerate(chains):
        q = q_ref[0, 0, pl.ds(pl.multiple_of((step * Q_BLOCKS + h) * MOBA_BLOCK, MOBA_BLOCK), MOBA_BLOCK), :]
        qa_ref[c] = jnp.where((lane >= a * HEAD_DIM) & (lane < (a + 1) * HEAD_DIM), q, jnp.zeros_like(q))

    def block_of(t):
        return jnp.maximum(j_last - t, 0)

    def key_block(j):
        return k_ref[0, 0, pl.ds(pl.multiple_of(j * MOBA_BLOCK, MOBA_BLOCK), MOBA_BLOCK), :]

    def scores(t, c, kj, near):
        h, a = chains[c]
        s = _dot_nt(kj, qa_ref[c])
        offset = t - (Q_BLOCKS - 1 - h)
        if isinstance(t, int):
            if offset < NEAR_TILES:
                s = s + bias_ref[a, max(offset, 0)]
        elif near:
            s = s + bias_ref[a, jnp.clip(offset, 0, NEAR_TILES)]
        return s

    def visits(t, c):
        return not isinstance(t, int) or t - (Q_BLOCKS - 1 - chains[c][0]) >= 0

    def chosen_row(t, c, j):
        h, a = chains[c]
        return sel_ref[step * Q_BLOCKS + h, a, pl.ds(j, 1), :] > jnp.where(t <= j_last, 0.5, 2.0)

    def score_stage(t, slot, near):
        kj = key_block(block_of(t))
        for c in range(len(chains)):
            s = scores(t, c, kj, near)
            s_ref[slot, c] = s
            mt_ref[slot, c] = jnp.max(s, axis=0, keepdims=True)

    def softmax_stage(t, slot, m_prev):
        j = block_of(t)
        m_next, alphas, pvs = [], [], []
        for c, (h, a) in enumerate(chains):
            chosen = chosen_row(t, c, j)
            m_new = jnp.where(chosen, jnp.maximum(m_prev[c], mt_ref[slot, c]), m_prev[c])
            alphas.append(jnp.exp2(m_prev[c] - m_new))
            p = jnp.exp2(s_ref[slot, c] - jnp.where(chosen, m_new, BIG)).astype(BF16)
            pvs.append(_dot(vt_ref[0, 0, j, a], p))
            m_next.append(m_new)
        for c in range(len(chains)):
            acc_ref[c] = alphas[c] * acc_ref[c] + pvs[c]
        return tuple(m_next)

    def exact_pair(tt, m):
        t = 2 * tt
        score_stage(t + 1, 1, True)
        m = softmax_stage(t, 0, m)
        score_stage(t + 2, 0, True)
        m = softmax_stage(t + 1, 1, m)
        return m

    def lazy_stage(carry, near, score=None, pv=None):
        r, excess = list(carry[0]), list(carry[1])
        if score is not None:
            t, slot = score
            j = block_of(t)
            kj = key_block(j)
            for c in range(len(chains)):
                if not visits(t, c):
                    continue
                s = scores(t, c, kj, near)
                chosen = chosen_row(t, c, j)
                tile_max = jnp.max(s, axis=0, keepdims=True)
                p_ref[slot, c] = jnp.exp2(s - jnp.where(chosen, r[c], BIG)).astype(BF16)
                r_new = jnp.where(chosen, jnp.maximum(r[c], tile_max), r[c])
                alpha_ref[slot, c] = jnp.exp2(r[c] - r_new)
                excess[c] = jnp.maximum(excess[c], jnp.where(chosen, tile_max - r[c], -BIG))
                r[c] = r_new
        if pv is not None:
            t, slot = pv
            j = block_of(t)
            live = [c for c in range(len(chains)) if visits(t, c)]
            pvs = [_dot(vt_ref[0, 0, j, chains[c][1]], p_ref[slot, c]) for c in live]
            for c, prod in zip(live, pvs):
                acc_ref[c] = (acc_ref[c] + prod) * alpha_ref[slot, c]
        return tuple(r), tuple(excess)

    def lazy_pairs(it, carry, *, near, first_pair, pairs_per_trip, single_trip=False):
        if single_trip:
            t = 2 * first_pair
            carry = lazy_stage(carry, near, score=(t, 0))
        else:
            t = 2 * (first_pair + it * pairs_per_trip)
        for _ in range(pairs_per_trip):
            carry = lazy_stage(carry, near, score=(t + 1, 1), pv=(t, 0))
            carry = lazy_stage(carry, near, score=(t + 2, 0), pv=(t + 1, 1))
            t = t + 2
        return carry

    def finalize():
        for h in range(Q_BLOCKS):
            outs = []
            for a in range(2):
                acc = acc_ref[2 * h + a]
                outs.append(acc[0:HEAD_DIM] * (1.0 / acc[HEAD_DIM:HEAD_DIM + 1]))
            row0 = pl.multiple_of((step * Q_BLOCKS + h) * MOBA_BLOCK, MOBA_BLOCK)
            o_ref[0, 0, pl.ds(row0, MOBA_BLOCK), :] = jnp.concatenate(outs, axis=0).T.astype(BF16)

    near_pairs = (NEAR_TILES + Q_BLOCKS) // 2
    n_pairs = (j_last + 2) // 2
    far_pairs = jnp.maximum(n_pairs - near_pairs, 0)
    lowest = (jnp.full((1, MOBA_BLOCK), -BIG, F32),) * len(chains)

    acc_ref[...] = jnp.zeros_like(acc_ref)
    if exact:
        score_stage(0, 0, True)
        lax.fori_loop(0, n_pairs, exact_pair, lowest)
        finalize()
        return None

    seeds = [seed_ref[step * Q_BLOCKS + h, a] for h, a in chains]
    carry = (tuple(seeds), lowest)
    counts = {(Q_BLOCKS * (i + 1) + 1) // 2 for i in range(nb // Q_BLOCKS)}
    for n in sorted(c for c in counts if c < near_pairs) + [near_pairs]:
        runs = (n_pairs >= n) if n == near_pairs else (n_pairs == n)
        carry = lax.fori_loop(0, jnp.where(runs, 1, 0),
                              functools.partial(lazy_pairs, near=True, first_pair=0, pairs_per_trip=n,
                                                single_trip=True), carry)
    first, left = near_pairs, far_pairs
    for pairs_per_trip in FAR_TRIP_PAIRS:
        trips = left // pairs_per_trip
        carry = lax.fori_loop(0, trips, functools.partial(lazy_pairs, near=False, first_pair=first,
                                                          pairs_per_trip=pairs_per_trip), carry)
        first, left = first + trips * pairs_per_trip, left - trips * pairs_per_trip
    finalize()
    return functools.reduce(jnp.maximum, carry[1])


def _attention(q, k, vt, bias_t):
    batch, _, seq, _ = q.shape
    nb = seq // MOBA_BLOCK
    assert nb % Q_BLOCKS == 0 and nb % SEL_CHUNK == 0
    n_chains = 2 * Q_BLOCKS
    return pl.pallas_call(
        functools.partial(_attn_kernel, nb=nb),
        grid=(batch, HEAD_PAIRS),
        in_specs=[
            pl.BlockSpec((1, 1, seq, PAIR_W), lambda b, p: (b, p, 0, 0)),
            pl.BlockSpec((1, 1, seq, PAIR_W), lambda b, p: (b, p, 0, 0)),
            pl.BlockSpec((1, 1, nb, 2, V_ROWS, MOBA_BLOCK), lambda b, p: (b, p, 0, 0, 0, 0)),
            pl.BlockSpec((2, NEAR_TILES + 1, MOBA_BLOCK, MOBA_BLOCK), lambda b, p: (p, 0, 0, 0)),
        ],
        out_specs=pl.BlockSpec((1, 1, seq, PAIR_W), lambda b, p: (b, p, 0, 0)),
        out_shape=jax.ShapeDtypeStruct((batch, HEAD_PAIRS, seq, PAIR_W), BF16),
        scratch_shapes=[
            pltpu.VMEM((nb, PAIR_W), F32),
            pltpu.VMEM((4 * nb, PAIR_W), BF16),
            pltpu.VMEM((n_chains, MOBA_BLOCK, PAIR_W), BF16),
            pltpu.VMEM((nb, 2, nb, MOBA_BLOCK), F32),
            pltpu.VMEM((nb, 2, 1, MOBA_BLOCK), F32),
            pltpu.VMEM((n_chains, V_ROWS, MOBA_BLOCK), F32),
            pltpu.VMEM((2, n_chains, MOBA_BLOCK, MOBA_BLOCK), F32),
            pltpu.VMEM((2, n_chains, 1, MOBA_BLOCK), F32),
            pltpu.VMEM((2, n_chains, MOBA_BLOCK, MOBA_BLOCK), BF16),
            pltpu.VMEM((2, n_chains, 1, MOBA_BLOCK), F32),
        ],
        compiler_params=pltpu.CompilerParams(dimension_semantics=("arbitrary", "arbitrary"),
                                             vmem_limit_bytes=VMEM_LIMIT),
        name="moba_attention",
    )(q, k, vt, bias_t)


def _post_kernel(x_ref, u_ref, vn_ref, sga_ref, sgb_ref, yb_ref, p_ref,
                 ws_ref, bs_ref, wo_ref, gffn_ref, w1_ref, w2_ref, gple_ref, wpg_ref, wpp_ref, gfin_ref,
                 o_ref, merged_ref, *, tm, final_norm):
    row = lax.broadcasted_iota(jnp.int32, (SGU_CHUNK, SGU_CHUNK), 0)
    col = lax.broadcasted_iota(jnp.int32, (SGU_CHUNK, SGU_CHUNK), 1)
    for g in range(SGU_GROUPS):
        w = jnp.where(row >= col, ws_ref[g], 0.0).astype(BF16)
        b = bs_ref[g]
        cols = slice(g * LANES, (g + 1) * LANES)
        for t in range(tm // SGU_CHUNK):
            rows = slice(t * SGU_CHUNK, (t + 1) * SGU_CHUNK)
            mixed = _dot(w, vn_ref[rows, cols]) + b
            y_a = u_ref[rows, cols].astype(F32) * mixed
            y_b = yb_ref[0, g, rows, :].astype(F32)
            merged = sga_ref[rows, cols].astype(F32) * y_a + sgb_ref[rows, cols].astype(F32) * y_b
            merged_ref[rows, cols] = merged.astype(BF16)

    x1 = x_ref[...] + _dot(merged_ref[...], wo_ref[...])

    h = _rms(x1, gffn_ref[...]).astype(BF16)
    x2 = x1
    for c in range(D_FF // D_MODEL):
        cs = slice(c * D_MODEL, (c + 1) * D_MODEL)
        a = jnp.square(jnp.maximum(_dot(h, w1_ref[:, cs]), 0.0)).astype(BF16)
        x2 = x2 + _dot(a, w2_ref[cs, :])

    gate = jax.nn.sigmoid(_dot(_rms(x2, gple_ref[...]).astype(BF16), wpg_ref[...]))
    x3 = x2 + gate * _dot(p_ref[...].astype(BF16), wpp_ref[...])
    o_ref[...] = _rms(x3, gfin_ref[...]) if final_norm else x3


def _post(x2, u, vn, sga, sgb, yb, p2, ws, bs, wo, gffn, w1, w2, gple, wpg, wpp, gfin, *, seq, tm, final_norm):
    n = x2.shape[0]
    tiles_per_seq = seq // tm
    row = lambda r: (r, 0)
    resident = functools.partial(pl.BlockSpec, pipeline_mode=pl.Buffered(1))
    c2 = lambda r: (0, 0)
    c3 = lambda r: (0, 0, 0)
    tokb = pl.BlockSpec((tm, D_MODEL), row)
    return pl.pallas_call(
        functools.partial(_post_kernel, tm=tm, final_norm=final_norm),
        grid=(n // tm,),
        in_specs=[
            tokb, tokb, tokb, tokb, tokb,
            pl.BlockSpec((1, HEAD_PAIRS, tm, PAIR_W), lambda r: (r // tiles_per_seq, 0, r % tiles_per_seq, 0)),
            pl.BlockSpec((tm, PLE_DIM), row),
            resident((SGU_GROUPS, SGU_CHUNK, SGU_CHUNK), c3),
            resident((SGU_GROUPS, SGU_CHUNK, 1), c3),
            resident((D_MODEL, D_MODEL), c2),
            resident((1, D_MODEL), c2),
            resident((D_MODEL, D_FF), c2),
            resident((D_FF, D_MODEL), c2),
            resident((1, D_MODEL), c2),
            resident((D_MODEL, D_MODEL), c2),
            resident((PLE_DIM, D_MODEL), c2),
            resident((1, D_MODEL), c2),
        ],
        out_specs=pl.BlockSpec((tm, D_MODEL), row),
        out_shape=jax.ShapeDtypeStruct((n, D_MODEL), F32),
        scratch_shapes=[pltpu.VMEM((tm, D_MODEL), BF16)],
        compiler_params=pltpu.CompilerParams(dimension_semantics=("arbitrary",),
                                             vmem_limit_bytes=VMEM_LIMIT),
        name="post",
    )(x2, u, vn, sga, sgb, yb, p2, ws, bs, wo, gffn, w1, w2, gple, wpg, wpp, gfin)


def kernel(x, p, norm_mix_g, w_in, w_sgu_spatial, b_sgu_spatial, ln_v_g, ln_v_b, rel_bias, w_out, norm_ffn_g,
           w_ff1, w_ff2, norm_ple_g, w_ple_gate, w_ple_proj, norm_final_g):
    batch, seq, _ = x.shape
    depth = w_in.shape[0]
    n = batch * seq
    assert seq % (Q_BLOCKS * MOBA_BLOCK) == 0 and seq // MOBA_BLOCK >= MOBA_TOPK
    tm_in = 512
    tm_post = 512
    vec = lambda g: g.reshape(1, D_MODEL)

    bias_t = _bias_tiles(rel_bias)
    x2 = x.reshape(n, D_MODEL)
    for i in range(depth):
        wi = w_in[i].astype(BF16)
        wvt = wi[:, 4 * D_MODEL:5 * D_MODEL].T
        u, vn, q, k, vt, sga, sgb = _in_proj(
            x2, vec(norm_mix_g[i]), wi, wvt, vec(ln_v_g[i]), vec(ln_v_b[i]),
            batch=batch, seq=seq, tm=tm_in)
        yb = _attention(q, k, vt, bias_t)
        x2 = _post(
            x2, u, vn, sga, sgb, yb, p[i].reshape(n, PLE_DIM),
            w_sgu_spatial[i], b_sgu_spatial[i].reshape(SGU_GROUPS, SGU_CHUNK, 1),
            w_out[i].astype(BF16), vec(norm_ffn_g[i]), w_ff1[i].astype(BF16), w_ff2[i].astype(BF16),
            vec(norm_ple_g[i]), w_ple_gate[i].astype(BF16), w_ple_proj[i].astype(BF16),
            vec(norm_final_g), seq=seq, tm=tm_post, final_norm=(i == depth - 1))
    return x2.reshape(batch, seq, D_MODEL)
```

```python
import functools
import math

import jax
import jax.numpy as jnp
from jax import lax
from jax.experimental import pallas as pl
from jax.experimental.pallas import tpu as pltpu

D_MODEL = 1024
PLE_DIM = 256
SGU_CHUNK = 128
SGU_GROUPS = 8
ATT_HEADS = 16
HEAD_DIM = 64
MOBA_BLOCK = 256
MOBA_TOPK = 3
REL_BUCKETS = 32
REL_MAX_DIST = 1024
D_FF = 4 * D_MODEL
EPS = 1e-6

LANES = 128
BF16_ROWS = 16
HEAD_PAIRS = ATT_HEADS // 2
PAIR_W = 2 * HEAD_DIM
V_ROWS = HEAD_DIM + BF16_ROWS
Q_BLOCKS = 4
FAR_TRIP_PAIRS = (4, 2, 1)
SEL_CHUNK = 8
LAZY_MAX_EXCESS = 16.0
BIG = 1e30
VMEM_LIMIT = 60 * 1024 * 1024
LOG2E = math.log2(math.e)

F32 = jnp.float32
BF16 = jnp.bfloat16


def _bucket_thresholds():
    max_exact = REL_BUCKETS // 2
    n_log = REL_BUCKETS - max_exact
    ratio = REL_MAX_DIST // max_exact
    out = []
    for k in range(1, n_log):
        d = max_exact
        target = (max_exact ** n_log) * (ratio ** k)
        while d ** n_log < target:
            d += 1
        out.append(d)
    return tuple(out)


BUCKET_THRESHOLDS = _bucket_thresholds()
NEAR_TILES = -(-(BUCKET_THRESHOLDS[-1] + MOBA_BLOCK - 1) // MOBA_BLOCK)
assert NEAR_TILES == 5


def _dot(a, b):
    return jnp.dot(a, b, preferred_element_type=F32)


def _dot_nt(a, b):
    return lax.dot_general(a, b, (((1,), (1,)), ((), ())), preferred_element_type=F32)


def _rms(x, g):
    return x * lax.rsqrt(jnp.mean(x * x, axis=-1, keepdims=True) + EPS) * g


def _in_proj_kernel(x_ref, g_ref, wa_ref, wvt_ref, wga_ref, wgb_ref, lng_ref, lnb_ref,
                    u_ref, vn_ref, q_ref, k_ref, vt_ref, sga_ref, sgb_ref, *, tm):
    h = _rms(x_ref[...], g_ref[...]).astype(BF16)

    u_ref[...] = jax.nn.gelu(_dot(h, wa_ref[:, 0:D_MODEL])).astype(BF16)

    gv = jax.nn.gelu(_dot(h, wa_ref[:, D_MODEL:2 * D_MODEL]))
    mu = jnp.mean(gv, axis=-1, keepdims=True)
    gc = gv - mu
    vn = gc * lax.rsqrt(jnp.mean(gc * gc, axis=-1, keepdims=True) + EPS)
    vn_ref[...] = (vn * lng_ref[...] + lnb_ref[...]).astype(BF16)

    q = _dot(h, wa_ref[:, 2 * D_MODEL:3 * D_MODEL]) * (HEAD_DIM ** -0.5 * LOG2E)
    k = _dot(h, wa_ref[:, 3 * D_MODEL:4 * D_MODEL])
    for p in range(HEAD_PAIRS):
        q_ref[0, p] = q[:, p * PAIR_W:(p + 1) * PAIR_W].astype(BF16)
        k_ref[0, p] = k[:, p * PAIR_W:(p + 1) * PAIR_W].astype(BF16)

    vt = _dot_nt(wvt_ref[...], h).astype(BF16)
    ones = jnp.ones((BF16_ROWS, MOBA_BLOCK), BF16)
    for p in range(HEAD_PAIRS):
        for jb in range(tm // MOBA_BLOCK):
            for a in range(2):
                r0 = p * PAIR_W + a * HEAD_DIM
                vt_ref[0, p, jb, a, 0:HEAD_DIM, :] = vt[r0:r0 + HEAD_DIM, jb * MOBA_BLOCK:(jb + 1) * MOBA_BLOCK]
                vt_ref[0, p, jb, a, HEAD_DIM:V_ROWS, :] = ones

    sga_ref[...] = jax.nn.sigmoid(_dot(h, wga_ref[...])).astype(BF16)
    sgb_ref[...] = jax.nn.sigmoid(_dot(h, wgb_ref[...])).astype(BF16)


def _in_proj(x2, g, w, wvt, lng, lnb, *, batch, seq, tm):
    n = x2.shape[0]
    tiles_per_seq = seq // tm
    nb = seq // MOBA_BLOCK
    row = lambda r: (r, 0)
    const = lambda r: (0, 0)
    resident = functools.partial(pl.BlockSpec, pipeline_mode=pl.Buffered(1))
    tok = jax.ShapeDtypeStruct((n, D_MODEL), BF16)
    pair_major = jax.ShapeDtypeStruct((batch, HEAD_PAIRS, seq, PAIR_W), BF16)
    return pl.pallas_call(
        functools.partial(_in_proj_kernel, tm=tm),
        grid=(n // tm,),
        in_specs=[
            pl.BlockSpec((tm, D_MODEL), row),
            resident((1, D_MODEL), const),
            resident((D_MODEL, 4 * D_MODEL), const),
            resident((D_MODEL, D_MODEL), const),
            resident((D_MODEL, D_MODEL), lambda r: (0, 5)),
            resident((D_MODEL, D_MODEL), lambda r: (0, 6)),
            resident((1, D_MODEL), const),
            resident((1, D_MODEL), const),
        ],
        out_specs=[
            pl.BlockSpec((tm, D_MODEL), row),
            pl.BlockSpec((tm, D_MODEL), row),
            pl.BlockSpec((1, HEAD_PAIRS, tm, PAIR_W), lambda r: (r // tiles_per_seq, 0, r % tiles_per_seq, 0)),
            pl.BlockSpec((1, HEAD_PAIRS, tm, PAIR_W), lambda r: (r // tiles_per_seq, 0, r % tiles_per_seq, 0)),
            pl.BlockSpec((1, HEAD_PAIRS, tm // MOBA_BLOCK, 2, V_ROWS, MOBA_BLOCK),
                         lambda r: (r // tiles_per_seq, 0, r % tiles_per_seq, 0, 0, 0)),
            pl.BlockSpec((tm, D_MODEL), row),
            pl.BlockSpec((tm, D_MODEL), row),
        ],
        out_shape=[tok, tok, pair_major, pair_major,
                   jax.ShapeDtypeStruct((batch, HEAD_PAIRS, nb, 2, V_ROWS, MOBA_BLOCK), BF16),
                   tok, tok],
        compiler_params=pltpu.CompilerParams(dimension_semantics=("arbitrary",),
                                             vmem_limit_bytes=VMEM_LIMIT,
                                             allow_input_fusion=[False, False, True, True, True, True, False, False]),
        name="in_proj",
    )(x2, g, w, wvt, w, w, lng, lnb)


def _bias_tiles_kernel(tab_ref, o_ref):
    h = pl.program_id(0)
    key = lax.broadcasted_iota(jnp.int32, (MOBA_BLOCK, MOBA_BLOCK), 0)
    qry = lax.broadcasted_iota(jnp.int32, (MOBA_BLOCK, MOBA_BLOCK), 1)
    max_exact = REL_BUCKETS // 2
    last = tab_ref[REL_BUCKETS - 1, h]
    for t in range(NEAR_TILES):
        dist = t * MOBA_BLOCK + qry - key
        n = jnp.maximum(dist, 0)
        val = jnp.full((MOBA_BLOCK, MOBA_BLOCK), tab_ref[0, h], F32)
        for b in range(1, max_exact + 1):
            val = jnp.where(n >= b, tab_ref[b, h], val)
        for kk, thr in enumerate(BUCKET_THRESHOLDS):
            val = jnp.where(n >= thr, tab_ref[max_exact + 1 + kk, h], val)
        val = (val - last) * LOG2E
        if t == 0:
            val = jnp.where(dist >= 0, val, -BIG)
        o_ref[0, t] = val
    o_ref[0, NEAR_TILES] = jnp.zeros((MOBA_BLOCK, MOBA_BLOCK), F32)


def _bias_tiles(rel_bias):
    return pl.pallas_call(
        _bias_tiles_kernel,
        grid=(ATT_HEADS,),
        in_specs=[pl.BlockSpec(memory_space=pltpu.SMEM)],
        out_specs=pl.BlockSpec((1, NEAR_TILES + 1, MOBA_BLOCK, MOBA_BLOCK), lambda h: (h, 0, 0, 0)),
        out_shape=jax.ShapeDtypeStruct((ATT_HEADS, NEAR_TILES + 1, MOBA_BLOCK, MOBA_BLOCK), F32),
        compiler_params=pltpu.CompilerParams(dimension_semantics=("arbitrary",)),
        name="bias_tiles",
    )(rel_bias)


def _attn_kernel(*refs, nb):
    n_tiles = nb // Q_BLOCKS
    worst = lax.fori_loop(0, n_tiles, lambda step, w: jnp.maximum(w, _attn_tile(step, *refs, nb=nb, exact=False)),
                          jnp.full((1, MOBA_BLOCK), -BIG, F32))

    @pl.when(jnp.max(worst) > LAZY_MAX_EXCESS)
    def _():
        def tile(step, carry):
            _attn_tile(step, *refs, nb=nb, exact=True)
            return carry
        lax.fori_loop(0, n_tiles, tile, 0)


def _attn_tile(step, q_ref, k_ref, vt_ref, bias_ref, o_ref,
               km_ref, kms_ref, qa_ref, sel_ref, seed_ref, acc_ref, s_ref, mt_ref, p_ref, alpha_ref, *, nb, exact):
    j_last = step * Q_BLOCKS + (Q_BLOCKS - 1)
    chains = [(h, a) for h in range(Q_BLOCKS) for a in range(2)]
    lane = lax.broadcasted_iota(jnp.int32, (1, PAIR_W), 1)

    def prepare():
        def body(j, c):
            kb = k_ref[0, 0, pl.ds(pl.multiple_of(j * MOBA_BLOCK, MOBA_BLOCK), MOBA_BLOCK), :]
            km_ref[pl.ds(j, 1), :] = jnp.sum(kb.astype(F32), axis=0, keepdims=True) * (1.0 / MOBA_BLOCK)
            return c
        lax.fori_loop(0, nb, body, 0)
        km = km_ref[...]
        km_hi = km.astype(BF16)
        km_lo = (km - km_hi.astype(F32)).astype(BF16)
        for a in range(2):
            in_head = (lane >= a * HEAD_DIM) & (lane < (a + 1) * HEAD_DIM)
            kms_ref[(2 * a) * nb:(2 * a + 1) * nb, :] = jnp.where(in_head, km_hi, jnp.zeros_like(km_hi))
            kms_ref[(2 * a + 1) * nb:(2 * a + 2) * nb, :] = jnp.where(in_head, km_lo, jnp.zeros_like(km_lo))

        n_q = SEL_CHUNK * MOBA_BLOCK
        head_rows = jnp.where(lane // HEAD_DIM == lax.broadcasted_iota(jnp.int32, (BF16_ROWS, PAIR_W), 0),
                              1.0, 0.0).astype(BF16)
        for ci in range(nb // SEL_CHUNK):
            rows = SEL_CHUNK * (ci + 1)
            q = q_ref[0, 0, ci * n_q:(ci + 1) * n_q, :]
            g = _dot_nt(kms_ref[...], q)
            gate = jnp.concatenate([g[(2 * a) * nb:(2 * a) * nb + rows] + g[(2 * a + 1) * nb:(2 * a + 1) * nb + rows]
                                    for a in range(2)], axis=1)
            pos = lax.broadcasted_iota(jnp.int32, (1, n_q), 1)
            own = ci * SEL_CHUNK + jnp.right_shift(pos, MOBA_BLOCK.bit_length() - 1)
            own = jnp.concatenate([own, own], axis=1)
            blk = lax.broadcasted_iota(jnp.int32, gate.shape, 0)
            gate = jnp.where(blk < own, gate, -jnp.inf)
            sel = blk == own
            for _ in range(MOBA_TOPK):
                top = jnp.max(gate, axis=0, keepdims=True)
                first = jnp.min(jnp.where(gate == top, blk, nb), axis=0, keepdims=True)
                pick = (blk == first) & (top > -jnp.inf)
                sel = sel | pick
                gate = jnp.where(pick, -jnp.inf, gate)
            sel_f = jnp.where(sel, 1.0, 0.0)
            own_scores = _dot_nt(head_rows, q * k_ref[0, 0, ci * n_q:(ci + 1) * n_q, :])
            later = jnp.zeros((nb - rows, MOBA_BLOCK), F32)
            for a in range(2):
                for b in range(SEL_CHUNK):
                    lo = a * n_q + b * MOBA_BLOCK
                    chosen_blocks = sel_f[:, lo:lo + MOBA_BLOCK]
                    sel_ref[ci * SEL_CHUNK + b, a] = (jnp.concatenate([chosen_blocks, later], axis=0)
                                                      if rows < nb else chosen_blocks)
                    seed_ref[ci * SEL_CHUNK + b, a] = (own_scores[a:a + 1, b * MOBA_BLOCK:(b + 1) * MOBA_BLOCK]
                                                       + bias_ref[a, 0, 0:1, 0:1])

    if not exact:
        pl.when(step == 0)(prepare)

    for c, (h, a) in enumerate(chains):
        q = q_ref[0, 0, pl.ds(pl.multiple_of((step * Q_BLOCKS + h) * MOBA_BLOCK, MOBA_BLOCK), MOBA_BLOCK), :]
        qa_ref[c] = jnp.where((lane >= a * HEAD_DIM) & (lane < (a + 1) * HEAD_DIM), q, jnp.zeros_like(q))

    def block_of(t):
        return jnp.maximum(j_last - t, 0)

    def key_block(j):
        return k_ref[0, 0, pl.ds(pl.multiple_of(j * MOBA_BLOCK, MOBA_BLOCK), MOBA_BLOCK), :]

    def scores(t, c, kj, near):
        h, a = chains[c]
        s = _dot_nt(kj, qa_ref[c])
        offset = t - (Q_BLOCKS - 1 - h)
        if isinstance(t, int):
            if offset < NEAR_TILES:
                s = s + bias_ref[a, max(offset, 0)]
        elif near:
            s = s + bias_ref[a, jnp.clip(offset, 0, NEAR_TILES)]
        return s

    def visits(t, c):
        return not isinstance(t, int) or t - (Q_BLOCKS - 1 - chains[c][0]) >= 0

    def chosen_row(t, c, j):
        h, a = chains[c]
        return sel_ref[step * Q_BLOCKS + h, a, pl.ds(j, 1), :] > jnp.where(t <= j_last, 0.5, 2.0)

    def score_stage(t, slot, near):
        kj = key_block(block_of(t))
        for c in range(len(chains)):
            s = scores(t, c, kj, near)
            s_ref[slot, c] = s
            mt_ref[slot, c] = jnp.max(s, axis=0, keepdims=True)

    def softmax_stage(t, slot, m_prev):
        j = block_of(t)
        m_next, alphas, pvs = [], [], []
        for c, (h, a) in enumerate(chains):
            chosen = chosen_row(t, c, j)
            m_new = jnp.where(chosen, jnp.maximum(m_prev[c], mt_ref[slot, c]), m_prev[c])
            alphas.append(jnp.exp2(m_prev[c] - m_new))
            p = jnp.exp2(s_ref[slot, c] - jnp.where(chosen, m_new, BIG)).astype(BF16)
            pvs.append(_dot(vt_ref[0, 0, j, a], p))
            m_next.append(m_new)
        for c in range(len(chains)):
            acc_ref[c] = alphas[c] * acc_ref[c] + pvs[c]
        return tuple(m_next)

    def exact_pair(tt, m):
        t = 2 * tt
        score_stage(t + 1, 1, True)
        m = softmax_stage(t, 0, m)
        score_stage(t + 2, 0, True)
        m = softmax_stage(t + 1, 1, m)
        return m

    def lazy_stage(carry, near, score=None, pv=None):
        r, excess = list(carry[0]), list(carry[1])
        if score is not None:
            t, slot = score
            j = block_of(t)
            kj = key_block(j)
            for c in range(len(chains)):
                if not visits(t, c):
                    continue
                s = scores(t, c, kj, near)
                chosen = chosen_row(t, c, j)
                tile_max = jnp.max(s, axis=0, keepdims=True)
                p_ref[slot, c] = jnp.exp2(s - jnp.where(chosen, r[c], BIG)).astype(BF16)
                r_new = jnp.where(chosen, jnp.maximum(r[c], tile_max), r[c])
                alpha_ref[slot, c] = jnp.exp2(r[c] - r_new)
                excess[c] = jnp.maximum(excess[c], jnp.where(chosen, tile_max - r[c], -BIG))
                r[c] = r_new
        if pv is not None:
            t, slot = pv
            j = block_of(t)
            live = [c for c in range(len(chains)) if visits(t, c)]
            pvs = [_dot(vt_ref[0, 0, j, chains[c][1]], p_ref[slot, c]) for c in live]
            for c, prod in zip(live, pvs):
                acc_ref[c] = (acc_ref[c] + prod) * alpha_ref[slot, c]
        return tuple(r), tuple(excess)

    def lazy_pairs(it, carry, *, near, first_pair, pairs_per_trip, single_trip=False):
        if single_trip:
            t = 2 * first_pair
            carry = lazy_stage(carry, near, score=(t, 0))
        else:
            t = 2 * (first_pair + it * pairs_per_trip)
        for _ in range(pairs_per_trip):
            carry = lazy_stage(carry, near, score=(t + 1, 1), pv=(t, 0))
            carry = lazy_stage(carry, near, score=(t + 2, 0), pv=(t + 1, 1))
            t = t + 2
        return carry

    def finalize():
        for h in range(Q_BLOCKS):
            outs = []
            for a in range(2):
                acc = acc_ref[2 * h + a]
                outs.append(acc[0:HEAD_DIM] * (1.0 / acc[HEAD_DIM:HEAD_DIM + 1]))
            row0 = pl.multiple_of((step * Q_BLOCKS + h) * MOBA_BLOCK, MOBA_BLOCK)
            o_ref[0, 0, pl.ds(row0, MOBA_BLOCK), :] = jnp.concatenate(outs, axis=0).T.astype(BF16)

    near_pairs = (NEAR_TILES + Q_BLOCKS) // 2
    n_pairs = (j_last + 2) // 2
    far_pairs = jnp.maximum(n_pairs - near_pairs, 0)
    lowest = (jnp.full((1, MOBA_BLOCK), -BIG, F32),) * len(chains)

    acc_ref[...] = jnp.zeros_like(acc_ref)
    if exact:
        score_stage(0, 0, True)
        lax.fori_loop(0, n_pairs, exact_pair, lowest)
        finalize()
        return None

    seeds = [seed_ref[step * Q_BLOCKS + h, a] for h, a in chains]
    carry = (tuple(seeds), lowest)
    counts = {(Q_BLOCKS * (i + 1) + 1) // 2 for i in range(nb // Q_BLOCKS)}
    for n in sorted(c for c in counts if c < near_pairs) + [near_pairs]:
        runs = (n_pairs >= n) if n == near_pairs else (n_pairs == n)
        carry = lax.fori_loop(0, jnp.where(runs, 1, 0),
                              functools.partial(lazy_pairs, near=True, first_pair=0, pairs_per_trip=n,
                                                single_trip=True), carry)
    first, left = near_pairs, far_pairs
    for pairs_per_trip in FAR_TRIP_PAIRS:
        trips = left // pairs_per_trip
        carry = lax.fori_loop(0, trips, functools.partial(lazy_pairs, near=False, first_pair=first,
                                                          pairs_per_trip=pairs_per_trip), carry)
        first, left = first + trips * pairs_per_trip, left - trips * pairs_per_trip
    finalize()
    return functools.reduce(jnp.maximum, carry[1])


def _attention(q, k, vt, bias_t):
    batch, _, seq, _ = q.shape
    nb = seq // MOBA_BLOCK
    assert nb % Q_BLOCKS == 0 and nb % SEL_CHUNK == 0
    n_chains = 2 * Q_BLOCKS
    return pl.pallas_call(
        functools.partial(_attn_kernel, nb=nb),
        grid=(batch, HEAD_PAIRS),
        in_specs=[
            pl.BlockSpec((1, 1, seq, PAIR_W), lambda b, p: (b, p, 0, 0)),
            pl.BlockSpec((1, 1, seq, PAIR_W), lambda b, p: (b, p, 0, 0)),
            pl.BlockSpec((1, 1, nb, 2, V_ROWS, MOBA_BLOCK), lambda b, p: (b, p, 0, 0, 0, 0)),
            pl.BlockSpec((2, NEAR_TILES + 1, MOBA_BLOCK, MOBA_BLOCK), lambda b, p: (p, 0, 0, 0)),
        ],
        out_specs=pl.BlockSpec((1, 1, seq, PAIR_W), lambda b, p: (b, p, 0, 0)),
        out_shape=jax.ShapeDtypeStruct((batch, HEAD_PAIRS, seq, PAIR_W), BF16),
        scratch_shapes=[
            pltpu.VMEM((nb, PAIR_W), F32),
            pltpu.VMEM((4 * nb, PAIR_W), BF16),
            pltpu.VMEM((n_chains, MOBA_BLOCK, PAIR_W), BF16),
            pltpu.VMEM((nb, 2, nb, MOBA_BLOCK), F32),
            pltpu.VMEM((nb, 2, 1, MOBA_BLOCK), F32),
            pltpu.VMEM((n_chains, V_ROWS, MOBA_BLOCK), F32),
            pltpu.VMEM((2, n_chains, MOBA_BLOCK, MOBA_BLOCK), F32),
            pltpu.VMEM((2, n_chains, 1, MOBA_BLOCK), F32),
            pltpu.VMEM((2, n_chains, MOBA_BLOCK, MOBA_BLOCK), BF16),
            pltpu.VMEM((2, n_chains, 1, MOBA_BLOCK), F32),
        ],
        compiler_params=pltpu.CompilerParams(dimension_semantics=("arbitrary", "arbitrary"),
                                             vmem_limit_bytes=VMEM_LIMIT),
        name="moba_attention",
    )(q, k, vt, bias_t)


def _post_kernel(x_ref, u_ref, vn_ref, sga_ref, sgb_ref, yb_ref, p_ref,
                 ws_ref, bs_ref, wo_ref, gffn_ref, w1_ref, w2_ref, gple_ref, wpg_ref, wpp_ref, gfin_ref,
                 o_ref, merged_ref, *, tm, final_norm):
    row = lax.broadcasted_iota(jnp.int32, (SGU_CHUNK, SGU_CHUNK), 0)
    col = lax.broadcasted_iota(jnp.int32, (SGU_CHUNK, SGU_CHUNK), 1)
    for g in range(SGU_GROUPS):
        w = jnp.where(row >= col, ws_ref[g], 0.0).astype(BF16)
        b = bs_ref[g]
        cols = slice(g * LANES, (g + 1) * LANES)
        for t in range(tm // SGU_CHUNK):
            rows = slice(t * SGU_CHUNK, (t + 1) * SGU_CHUNK)
            mixed = _dot(w, vn_ref[rows, cols]) + b
            y_a = u_ref[rows, cols].astype(F32) * mixed
            y_b = yb_ref[0, g, rows, :].astype(F32)
            merged = sga_ref[rows, cols].astype(F32) * y_a + sgb_ref[rows, cols].astype(F32) * y_b
            merged_ref[rows, cols] = merged.astype(BF16)

    x1 = x_ref[...] + _dot(merged_ref[...], wo_ref[...])

    h = _rms(x1, gffn_ref[...]).astype(BF16)
    x2 = x1
    for c in range(D_FF // D_MODEL):
        cs = slice(c * D_MODEL, (c + 1) * D_MODEL)
        a = jnp.square(jnp.maximum(_dot(h, w1_ref[:, cs]), 0.0)).astype(BF16)
        x2 = x2 + _dot(a, w2_ref[cs, :])

    gate = jax.nn.sigmoid(_dot(_rms(x2, gple_ref[...]).astype(BF16), wpg_ref[...]))
    x3 = x2 + gate * _dot(p_ref[...].astype(BF16), wpp_ref[...])
    o_ref[...] = _rms(x3, gfin_ref[...]) if final_norm else x3


def _post(x2, u, vn, sga, sgb, yb, p2, ws, bs, wo, gffn, w1, w2, gple, wpg, wpp, gfin, *, seq, tm, final_norm):
    n = x2.shape[0]
    tiles_per_seq = seq // tm
    row = lambda r: (r, 0)
    resident = functools.partial(pl.BlockSpec, pipeline_mode=pl.Buffered(1))
    c2 = lambda r: (0, 0)
    c3 = lambda r: (0, 0, 0)
    tokb = pl.BlockSpec((tm, D_MODEL), row)
    return pl.pallas_call(
        functools.partial(_post_kernel, tm=tm, final_norm=final_norm),
        grid=(n // tm,),
        in_specs=[
            tokb, tokb, tokb, tokb, tokb,
            pl.BlockSpec((1, HEAD_PAIRS, tm, PAIR_W), lambda r: (r // tiles_per_seq, 0, r % tiles_per_seq, 0)),
            pl.BlockSpec((tm, PLE_DIM), row),
            resident((SGU_GROUPS, SGU_CHUNK, SGU_CHUNK), c3),
            resident((SGU_GROUPS, SGU_CHUNK, 1), c3),
            resident((D_MODEL, D_MODEL), c2),
            resident((1, D_MODEL), c2),
            resident((D_MODEL, D_FF), c2),
            resident((D_FF, D_MODEL), c2),
            resident((1, D_MODEL), c2),
            resident((D_MODEL, D_MODEL), c2),
            resident((PLE_DIM, D_MODEL), c2),
            resident((1, D_MODEL), c2),
        ],
        out_specs=pl.BlockSpec((tm, D_MODEL), row),
        out_shape=jax.ShapeDtypeStruct((n, D_MODEL), F32),
        scratch_shapes=[pltpu.VMEM((tm, D_MODEL), BF16)],
        compiler_params=pltpu.CompilerParams(dimension_semantics=("arbitrary",),
                                             vmem_limit_bytes=VMEM_LIMIT,
                                             allow_input_fusion=[i in (9, 11, 12, 14, 15) for i in range(17)]),
        name="post",
    )(x2, u, vn, sga, sgb, yb, p2, ws, bs, wo, gffn, w1, w2, gple, wpg, wpp, gfin)


def kernel(x, p, norm_mix_g, w_in, w_sgu_spatial, b_sgu_spatial, ln_v_g, ln_v_b, rel_bias, w_out, norm_ffn_g,
           w_ff1, w_ff2, norm_ple_g, w_ple_gate, w_ple_proj, norm_final_g):
    batch, seq, _ = x.shape
    depth = w_in.shape[0]
    n = batch * seq
    assert seq % (Q_BLOCKS * MOBA_BLOCK) == 0 and seq // MOBA_BLOCK >= MOBA_TOPK
    tm_in = 512
    tm_post = 512
    vec = lambda g: g.reshape(1, D_MODEL)

    bias_t = _bias_tiles(rel_bias)
    x2 = x.reshape(n, D_MODEL)
    for i in range(depth):
        wi = w_in[i].astype(BF16)
        wvt = wi[:, 4 * D_MODEL:5 * D_MODEL].T
        u, vn, q, k, vt, sga, sgb = _in_proj(
            x2, vec(norm_mix_g[i]), wi, wvt, vec(ln_v_g[i]), vec(ln_v_b[i]),
            batch=batch, seq=seq, tm=tm_in)
        yb = _attention(q, k, vt, bias_t)
        x2 = _post(
            x2, u, vn, sga, sgb, yb, p[i].reshape(n, PLE_DIM),
            w_sgu_spatial[i], b_sgu_spatial[i].reshape(SGU_GROUPS, SGU_CHUNK, 1),
            w_out[i].astype(BF16), vec(norm_ffn_g[i]), w_ff1[i].astype(BF16), w_ff2[i].astype(BF16),
            vec(norm_ple_g[i]), w_ple_gate[i].astype(BF16), w_ple_proj[i].astype(BF16),
            vec(norm_final_g), seq=seq, tm=tm_post, final_norm=(i == depth - 1))
    return x2.reshape(batch, seq, D_MODEL)
```
